```python
import math
import jax, jax.numpy as jnp
from jax import lax
import numpy as np

D_MODEL = 2048
BATCH = 4
SEQ = 4096
DEPTH = 1

SSM_WIDTH = 1024
SSM_GROUP = 16
SSM_GROUPS = SSM_WIDTH // SSM_GROUP
SSM_STATE = 64
CONV_WIDTH = 1024
CONV_GROUP = 64
CONV_K = 3
D_FF = 5632
EPS = 1e-6
DT_MIN = 1e-3
DT_MAX = 1e-1
IN_COLS = SSM_WIDTH + 3 * CONV_WIDTH + 2 * D_MODEL

kernel_name = "hybrid_s5_shortconv_gated_block"


def rmsnorm(x, g):
    xf = x.astype(jnp.float32)
    y = xf * lax.rsqrt(jnp.mean(xf * xf, axis=-1, keepdims=True) + EPS)
    return (y * g.astype(jnp.float32)).astype(x.dtype)


def causal_dwconv(x, w, b):
    c = x.shape[-1]
    y = lax.conv_general_dilated(
        x, w[:, None, :].astype(x.dtype), window_strides=(1,),
        padding=((CONV_K - 1, 0),), dimension_numbers=("NWC", "WIO", "NWC"),
        feature_group_count=c)
    return y + b.astype(x.dtype)


def s5_scan(u, a_re, a_im, log_dt, b_re, b_im, c_re, c_im, d_skip):
    bsz, seq_len, _ = u.shape
    f32 = jnp.float32
    uf = u.astype(f32).reshape(bsz, seq_len, SSM_GROUPS, SSM_GROUP)
    ar = a_re.astype(f32)
    ai = a_im.astype(f32)
    dt = jnp.exp(log_dt.astype(f32))[:, None]
    mag = jnp.exp(dt * ar)
    abar_re = mag * jnp.cos(dt * ai)
    abar_im = mag * jnp.sin(dt * ai)
    nr = abar_re - 1.0
    ni = abar_im
    den = ar * ar + ai * ai
    fr = (nr * ar + ni * ai) / den
    fi = (ni * ar - nr * ai) / den
    br = b_re.astype(f32)
    bi = b_im.astype(f32)
    bbar_re = fr[..., None] * br - fi[..., None] * bi
    bbar_im = fr[..., None] * bi + fi[..., None] * br
    bu_re = jnp.einsum("blgh,gph->blgp", uf, bbar_re)
    bu_im = jnp.einsum("blgh,gph->blgp", uf, bbar_im)
    a_seq_re = jnp.broadcast_to(abar_re[None, None], (1, seq_len, SSM_GROUPS, SSM_STATE))
    a_seq_im = jnp.broadcast_to(abar_im[None, None], (1, seq_len, SSM_GROUPS, SSM_STATE))

    def combine(e1, e2):
        a1r, a1i, b1r, b1i = e1
        a2r, a2i, b2r, b2i = e2
        return (a2r * a1r - a2i * a1i,
                a2r * a1i + a2i * a1r,
                a2r * b1r - a2i * b1i + b2r,
                a2r * b1i + a2i * b1r + b2i)

    _, _, xr, xi = lax.associative_scan(combine, (a_seq_re, a_seq_im, bu_re, bu_im), axis=1)
    y = (jnp.einsum("blgp,ghp->blgh", xr, c_re.astype(f32))
         - jnp.einsum("blgp,ghp->blgh", xi, c_im.astype(f32))
         + d_skip.astype(f32) * uf)
    return y.reshape(bsz, seq_len, SSM_WIDTH).astype(u.dtype)


def token_mixer(xn, w_in, a_re, a_im, log_dt, b_re, b_im, c_re, c_im, d_skip,
                w_glu, w_ssm_out, conv_w, conv_b, w_conv_out, w_o):
    proj = xn @ w_in
    s0 = SSM_WIDTH
    s1 = s0 + CONV_WIDTH
    s2 = s1 + CONV_WIDTH
    s3 = s2 + CONV_WIDTH
    s4 = s3 + D_MODEL
    u = proj[..., :s0]
    v = proj[..., s0:s1]
    gate_b = proj[..., s1:s2]
    gate_c = proj[..., s2:s3]
    merge_a = proj[..., s3:s4]
    merge_b = proj[..., s4:]
    ya = jax.nn.gelu(s5_scan(u, a_re, a_im, log_dt, b_re, b_im, c_re, c_im, d_skip))
    ya = ya * jax.nn.sigmoid(ya @ w_glu)
    ya = ya @ w_ssm_out
    yb = (gate_b * causal_dwconv(gate_c * v, conv_w, conv_b)) @ w_conv_out
    merged = jax.nn.sigmoid(merge_a) * ya + jax.nn.sigmoid(merge_b) * yb
    return merged @ w_o


def conv_ffn(xn, w_up, ffn_conv_w, ffn_conv_b, w_down):
    h = xn @ w_up
    a = causal_dwconv(h[..., :D_FF], ffn_conv_w, ffn_conv_b)
    return (jax.nn.gelu(a) * h[..., D_FF:]) @ w_down


def setup_inputs(seed: int = 0) -> dict:
    key = jax.random.key(seed)
    ks = jax.random.split(key, 24)
    f32 = jnp.float32
    nrm = lambda k, s, sc: jax.random.normal(k, s, f32) * sc
    n_idx = jnp.arange(SSM_STATE, dtype=f32)
    a_re = -0.5 * jnp.exp(nrm(ks[3], (DEPTH, SSM_GROUPS, SSM_STATE), 0.05))
    a_im = math.pi * n_idx[None, None, :] + nrm(ks[4], (DEPTH, SSM_GROUPS, SSM_STATE), 0.05)
    log_dt = jax.random.uniform(ks[5], (DEPTH, SSM_GROUPS), f32, math.log(DT_MIN), math.log(DT_MAX))
    return {
        "x": nrm(ks[0], (BATCH, SEQ, D_MODEL), 1.0),
        "norm_tok": 1.0 + nrm(ks[1], (DEPTH, D_MODEL), 0.01),
        "w_in": nrm(ks[2], (DEPTH, D_MODEL, IN_COLS), D_MODEL ** -0.5),
        "a_re": a_re,
        "a_im": a_im,
        "log_dt": log_dt,
        "b_re": nrm(ks[6], (DEPTH, SSM_GROUPS, SSM_STATE, SSM_GROUP), (2 * SSM_GROUP) ** -0.5),
        "b_im": nrm(ks[7], (DEPTH, SSM_GROUPS, SSM_STATE, SSM_GROUP), (2 * SSM_GROUP) ** -0.5),
        "c_re": nrm(ks[8], (DEPTH, SSM_GROUPS, SSM_GROUP, SSM_STATE), SSM_STATE ** -0.5),
        "c_im": nrm(ks[9], (DEPTH, SSM_GROUPS, SSM_GROUP, SSM_STATE), SSM_STATE ** -0.5),
        "d_skip": nrm(ks[10], (DEPTH, SSM_GROUPS, SSM_GROUP), 1.0),
        "w_glu": nrm(ks[11], (DEPTH, SSM_WIDTH, SSM_WIDTH), SSM_WIDTH ** -0.5),
        "w_ssm_out": nrm(ks[12], (DEPTH, SSM_WIDTH, D_MODEL), SSM_WIDTH ** -0.5),
        "conv_w": nrm(ks[13], (DEPTH, CONV_K, CONV_WIDTH), CONV_K ** -0.5),
        "conv_b": nrm(ks[14], (DEPTH, CONV_WIDTH), 0.01),
        "w_conv_out": nrm(ks[15], (DEPTH, CONV_WIDTH, D_MODEL), CONV_WIDTH ** -0.5),
        "w_o": nrm(ks[16], (DEPTH, D_MODEL, D_MODEL), D_MODEL ** -0.5),
        "norm_ffn": 1.0 + nrm(ks[17], (DEPTH, D_MODEL), 0.01),
        "w_up": nrm(ks[18], (DEPTH, D_MODEL, 2 * D_FF), D_MODEL ** -0.5),
        "ffn_conv_w": nrm(ks[19], (DEPTH, CONV_K, D_FF), CONV_K ** -0.5),
        "ffn_conv_b": nrm(ks[20], (DEPTH, D_FF), 0.01),
        "w_down": nrm(ks[21], (DEPTH, D_FF, D_MODEL), D_FF ** -0.5),
        "norm_final": 1.0 + nrm(ks[22], (D_MODEL,), 0.01),
    }


def reference(x, norm_tok, w_in, a_re, a_im, log_dt, b_re, b_im, c_re, c_im, d_skip,
              w_glu, w_ssm_out, conv_w, conv_b, w_conv_out, w_o,
              norm_ffn, w_up, ffn_conv_w, ffn_conv_b, w_down, norm_final):
    h = x
    for l in range(DEPTH):
        h = h + token_mixer(rmsnorm(h, norm_tok[l]), w_in[l], a_re[l], a_im[l], log_dt[l],
                            b_re[l], b_im[l], c_re[l], c_im[l], d_skip[l],
                            w_glu[l], w_ssm_out[l], conv_w[l], conv_b[l], w_conv_out[l], w_o[l])
        h = h + conv_ffn(rmsnorm(h, norm_ffn[l]), w_up[l], ffn_conv_w[l], ffn_conv_b[l], w_down[l])
    return rmsnorm(h, norm_final)
```

```python
import functools

import jax
import jax.numpy as jnp
from jax import lax
from jax.experimental import pallas as pl
from jax.experimental.pallas import tpu as pltpu

SSM_GROUP = 16
SSM_STATE = 64
CONV_K = 3
EPS = 1e-6

LANES = 128
SUBLANES = 8
MXU_DIM = 256
VMEM_LIMIT_BYTES = 56 * 1024 * 1024

TOKEN_TILE = 512
CHUNK = 16
PAIR = 2
GROUPS_PER_LANE_BLOCK = LANES // SSM_GROUP
STATE_LANES = GROUPS_PER_LANE_BLOCK * SSM_STATE
PAIRS = CHUNK // PAIR

F32 = jnp.float32
BF16 = jnp.bfloat16


def _rmsnorm(x, gain):
    return x * lax.rsqrt(jnp.mean(x * x, axis=-1, keepdims=True) + EPS) * gain


def _dot(a, b):
    return jnp.dot(a, b, preferred_element_type=F32)


def _causal_conv3(a, prev, w_ref, b_ref):
    w0 = w_ref[0:1, :]
    w1 = w_ref[1:2, :]
    w2 = w_ref[2:3, :]
    bias = b_ref[...]
    full = w2 * a + w1 * pltpu.roll(a, 1, 0) + w0 * pltpu.roll(a, 2, 0) + bias
    top = a[0:SUBLANES]
    rows = lax.broadcasted_iota(jnp.int32, top.shape, 0)
    s1 = jnp.where(rows < 1, pltpu.roll(prev, 1, 0), pltpu.roll(top, 1, 0))
    s2 = jnp.where(rows < 2, pltpu.roll(prev, 2, 0), pltpu.roll(top, 2, 0))
    fix = w2 * top + w1 * s1 + w0 * s2 + bias
    return jnp.concatenate([fix, full[SUBLANES:]], axis=0)


def _inproj_body(tiles_per_seq, x_ref, g_ref, wu_ref, wv_ref, wgb_ref, wgc_ref,
                 cw_ref, cb_ref, u_ref, zb_ref, prev_ref):
    i = pl.program_id(0)

    @pl.when(i % tiles_per_seq == 0)
    def _():
        prev_ref[...] = jnp.zeros_like(prev_ref)

    xn = _rmsnorm(x_ref[...], g_ref[...]).astype(BF16)
    u_ref[...] = _dot(xn, wu_ref[...])
    cin = _dot(xn, wgc_ref[...]) * _dot(xn, wv_ref[...])
    conv = _causal_conv3(cin, prev_ref[...], cw_ref, cb_ref)
    prev_ref[...] = cin[cin.shape[0] - SUBLANES:]
    zb_ref[...] = (_dot(xn, wgb_ref[...]) * conv).astype(BF16)


def _inproj(x2, norm_g, w_in_bf, conv_w, conv_b, seq_len):
    n, d = x2.shape
    cw = conv_w.shape[1]
    tiles_per_seq = seq_len // TOKEN_TILE
    const = lambda i: (0, 0)
    wspec = lambda col: pl.BlockSpec((d, cw), lambda i, col=col: (0, col),
                                     pipeline_mode=pl.Buffered(1))
    return pl.pallas_call(
        functools.partial(_inproj_body, tiles_per_seq),
        grid=(n // TOKEN_TILE,),
        in_specs=[
            pl.BlockSpec((TOKEN_TILE, d), lambda i: (i, 0)),
            pl.BlockSpec((1, d), const),
            wspec(0), wspec(1), wspec(2), wspec(3),
            pl.BlockSpec((CONV_K, cw), const),
            pl.BlockSpec((1, cw), const),
        ],
        out_specs=[
            pl.BlockSpec((TOKEN_TILE, cw), lambda i: (i, 0)),
            pl.BlockSpec((TOKEN_TILE, cw), lambda i: (i, 0)),
        ],
        out_shape=[
            jax.ShapeDtypeStruct((n, cw), F32),
            jax.ShapeDtypeStruct((n, cw), BF16),
        ],
        scratch_shapes=[pltpu.VMEM((SUBLANES, cw), F32)],
        compiler_params=pltpu.CompilerParams(
            dimension_semantics=("arbitrary",), vmem_limit_bytes=VMEM_LIMIT_BYTES),
        name="inproj",
    )(x2, norm_g, w_in_bf, w_in_bf, w_in_bf, w_in_bf, conv_w, conv_b)


def _cmul(ar, ai, xr, xi):
    return ar * xr - ai * xi, ar * xi + ai * xr


def _ssm_body(u_ref, tz_ref, p_ref, q_ref, pw_ref, d_ref, y_ref, s_scr, xin_scr):
    m = u_ref.shape[0] // CHUNK
    sl = STATE_LANES
    pieces = [u_ref[pl.ds(t, m, stride=CHUNK), :] for t in range(CHUNK)]
    x2 = [jnp.concatenate([pieces[PAIR * i], pieces[PAIR * i + 1]], axis=1).astype(BF16)
          for i in range(PAIRS)]

    yacc = [None] * PAIRS
    for off in range(PAIRS):
        r = _dot(jnp.concatenate(x2[0:PAIRS - off], axis=0), tz_ref[off])
        for i in range(PAIRS - off):
            blk = r[i * m:(i + 1) * m]
            yacc[i + off] = blk if yacc[i + off] is None else yacc[i + off] + blk

    a2r = pw_ref[0:1, :]
    a2i = pw_ref[1:2, :]
    a16r = pw_ref[2:3, :]
    a16i = pw_ref[3:4, :]
    sst = _dot(jnp.concatenate(x2, axis=0), p_ref[...])
    sr = sst[0:m, :sl]
    si = sst[0:m, sl:]
    for i in range(1, PAIRS):
        sr, si = _cmul(a2r, a2i, sr, si)
        sr = sr + sst[i * m:(i + 1) * m, :sl]
        si = si + sst[i * m:(i + 1) * m, sl:]
    s_scr[:, :sl] = sr
    s_scr[:, sl:] = si

    def step(k, carry):
        xr, xi = carry
        xin_scr[pl.ds(k, 1), :sl] = xr
        xin_scr[pl.ds(k, 1), sl:] = xi
        row = s_scr[pl.ds(k, 1), :]
        nr, ni = _cmul(a16r, a16i, xr, xi)
        return nr + row[:, :sl], ni + row[:, sl:]

    zero = jnp.zeros((1, sl), F32)
    lax.fori_loop(0, m, step, (zero, zero), unroll=8)

    xr = xin_scr[:, :sl]
    xi = xin_scr[:, sl:]
    xs = []
    for j in range(PAIRS):
        xs.append(jnp.concatenate([xr, xi], axis=1).astype(BF16))
        if j + 1 < PAIRS:
            xr, xi = _cmul(a2r, a2i, xr, xi)
    yq = _dot(jnp.concatenate(xs, axis=0), q_ref[...])

    d = d_ref[...]
    for j in range(PAIRS):
        yj = yacc[j] + yq[j * m:(j + 1) * m]
        for e in range(PAIR):
            t = PAIR * j + e
            yt = yj[:, e * LANES:(e + 1) * LANES] + d * pieces[t]
            y_ref[pl.ds(t, m, stride=CHUNK), :] = jax.nn.gelu(yt)


def _ssm(u, tz, pm, qm, pw, dsk, batch, seq_len):
    n, width = u.shape
    nblk = width // LANES
    m = seq_len // CHUNK
    return pl.pallas_call(
        _ssm_body,
        grid=(batch, nblk),
        in_specs=[
            pl.BlockSpec((seq_len, LANES), lambda b, c: (b, c)),
            pl.BlockSpec((None, PAIRS, MXU_DIM, MXU_DIM), lambda b, c: (c, 0, 0, 0)),
            pl.BlockSpec((None, MXU_DIM, 2 * STATE_LANES), lambda b, c: (c, 0, 0)),
            pl.BlockSpec((None, 2 * STATE_LANES, MXU_DIM), lambda b, c: (c, 0, 0)),
            pl.BlockSpec((None, 4, STATE_LANES), lambda b, c: (c, 0, 0)),
            pl.BlockSpec((None, 1, LANES), lambda b, c: (c, 0, 0)),
        ],
        out_specs=pl.BlockSpec((seq_len, LANES), lambda b, c: (b, c)),
        out_shape=jax.ShapeDtypeStruct((n, width), F32),
        scratch_shapes=[pltpu.VMEM((m, 2 * STATE_LANES), F32),
                        pltpu.VMEM((m, 2 * STATE_LANES), F32)],
        compiler_params=pltpu.CompilerParams(
            dimension_semantics=("arbitrary", "arbitrary"),
            vmem_limit_bytes=VMEM_LIMIT_BYTES),
        name="ssm",
    )(u, tz, pm, qm, pw, dsk)


def _ssm_operators(a_re, a_im, log_dt, b_re, b_im, c_re, c_im, d_skip):
    hp = lax.Precision.HIGHEST
    g = a_re.shape[0]
    nblk = g // GROUPS_PER_LANE_BLOCK
    gl = GROUPS_PER_LANE_BLOCK
    dt = jnp.exp(log_dt)[:, None]
    mag = jnp.exp(dt * a_re)
    abar_re = mag * jnp.cos(dt * a_im)
    abar_im = mag * jnp.sin(dt * a_im)
    nr = abar_re - 1.0
    ni = abar_im
    den = a_re * a_re + a_im * a_im
    fr = (nr * a_re + ni * a_im) / den
    fi = (ni * a_re - nr * a_im) / den
    bb_re = fr[..., None] * b_re - fi[..., None] * b_im
    bb_im = fr[..., None] * b_im + fi[..., None] * b_re
    nn = jnp.arange(CHUNK + 1, dtype=F32)[:, None, None]
    pmag = jnp.exp(nn * (dt * a_re)[None])
    pw_re = pmag * jnp.cos(nn * (dt * a_im)[None])
    pw_im = pmag * jnp.sin(nn * (dt * a_im)[None])

    w_re = pw_re[:CHUNK, :, :, None] * bb_re[None] - pw_im[:CHUNK, :, :, None] * bb_im[None]
    w_im = pw_re[:CHUNK, :, :, None] * bb_im[None] + pw_im[:CHUNK, :, :, None] * bb_re[None]
    kern = (jnp.einsum("gop,ngpi->ngoi", c_re, w_re, precision=hp)
            - jnp.einsum("gop,ngpi->ngoi", c_im, w_im, precision=hp))
    eye = jnp.eye(gl, dtype=F32)

    off = jnp.arange(PAIRS)[:, None, None]
    e_in = jnp.arange(PAIR)[None, :, None]
    e_out = jnp.arange(PAIR)[None, None, :]
    lag = PAIR * off + e_out - e_in
    tz = kern[jnp.clip(lag, 0, CHUNK - 1)] * (lag >= 0)[..., None, None, None].astype(F32)
    tz = tz.reshape(PAIRS, PAIR, PAIR, nblk, gl, SSM_GROUP, SSM_GROUP)
    tz = jnp.einsum("defcgoi,gk->cdegifko", tz, eye)
    tz = tz.reshape(nblk, PAIRS, MXU_DIM, MXU_DIM).astype(BF16)

    p_re = jnp.stack([w_re[1], w_re[0]])
    p_im = jnp.stack([w_im[1], w_im[0]])
    pm = jnp.stack([p_re, p_im]).reshape(2, PAIR, nblk, gl, SSM_STATE, SSM_GROUP)
    pm = jnp.einsum("recgpi,gk->cegirkp", pm, eye)
    pm = pm.reshape(nblk, MXU_DIM, 2 * STATE_LANES).astype(BF16)

    v_re = c_re[None] * pw_re[1:PAIR + 1, :, None, :] - c_im[None] * pw_im[1:PAIR + 1, :, None, :]
    v_im = c_re[None] * pw_im[1:PAIR + 1, :, None, :] + c_im[None] * pw_re[1:PAIR + 1, :, None, :]
    qm = jnp.stack([v_re, -v_im]).reshape(2, PAIR, nblk, gl, SSM_GROUP, SSM_STATE)
    qm = jnp.einsum("recgop,gk->crgpeko", qm, eye)
    qm = qm.reshape(nblk, 2 * STATE_LANES, MXU_DIM).astype(BF16)

    pw = jnp.stack([pw_re[PAIR], pw_im[PAIR], pw_re[CHUNK], pw_im[CHUNK]])
    pw = pw.reshape(4, nblk, STATE_LANES).transpose(1, 0, 2)
    dsk = d_skip.reshape(nblk, 1, LANES)
    return tz, pm, qm, pw, dsk


def _mix_body(x_ref, g_ref, ya_ref, zb_ref, wglu_ref, wma_ref, wmb_ref, wssm_ref,
              wconv_ref, wo_ref, h_ref, xn_scr, ya2_scr):
    j = pl.program_id(1)

    @pl.when(j == 0)
    def _():
        x = x_ref[...]
        xn_scr[...] = _rmsnorm(x, g_ref[...]).astype(BF16)
        ya = ya_ref[...]
        ya2_scr[...] = (ya * jax.nn.sigmoid(_dot(ya.astype(BF16), wglu_ref[...]))).astype(BF16)
        h_ref[...] = x

    xn = xn_scr[...]
    pa = _dot(ya2_scr[...], wssm_ref[...])
    pb = _dot(zb_ref[...], wconv_ref[...])
    merged = (jax.nn.sigmoid(_dot(xn, wma_ref[...])) * pa
              + jax.nn.sigmoid(_dot(xn, wmb_ref[...])) * pb)
    h_ref[...] += _dot(merged.astype(BF16), wo_ref[...])


def _mix(x2, norm_g, ya, zb, w_glu_bf, w_in_bf, w_ssm_bf, w_conv_bf, w_o_bf, col_tile=512):
    n, d = x2.shape
    sw = ya.shape[1]
    cw = zb.shape[1]
    ncol = d // col_tile
    merge_a0 = (w_in_bf.shape[1] - 2 * d) // col_tile
    merge_b0 = merge_a0 + ncol
    const = lambda i, j: (0, 0)
    return pl.pallas_call(
        _mix_body,
        grid=(n // TOKEN_TILE, ncol),
        in_specs=[
            pl.BlockSpec((TOKEN_TILE, d), lambda i, j: (i, 0)),
            pl.BlockSpec((1, d), const),
            pl.BlockSpec((TOKEN_TILE, sw), lambda i, j: (i, 0)),
            pl.BlockSpec((TOKEN_TILE, cw), lambda i, j: (i, 0)),
            pl.BlockSpec((sw, sw), const),
            pl.BlockSpec((d, col_tile), lambda i, j: (0, merge_a0 + j)),
            pl.BlockSpec((d, col_tile), lambda i, j: (0, merge_b0 + j)),
            pl.BlockSpec((sw, col_tile), lambda i, j: (0, j)),
            pl.BlockSpec((cw, col_tile), lambda i, j: (0, j)),
            pl.BlockSpec((col_tile, d), lambda i, j: (j, 0)),
        ],
        out_specs=pl.BlockSpec((TOKEN_TILE, d), lambda i, j: (i, 0)),
        out_shape=jax.ShapeDtypeStruct((n, d), F32),
        scratch_shapes=[pltpu.VMEM((TOKEN_TILE, d), BF16),
                        pltpu.VMEM((TOKEN_TILE, sw), BF16)],
        compiler_params=pltpu.CompilerParams(
            dimension_semantics=("arbitrary", "arbitrary"),
            vmem_limit_bytes=VMEM_LIMIT_BYTES),
        name="mix",
    )(x2, norm_g, ya, zb, w_glu_bf, w_in_bf, w_in_bf, w_ssm_bf, w_conv_bf, w_o_bf)


def _ffn_body(tiles_per_seq, final_norm, h_ref, g_ref, wa_ref, wg_ref, cw_ref, cb_ref,
              wd_ref, gfin_ref, o_ref, hn_scr, prev_scr):
    i = pl.program_id(0)
    f = pl.program_id(1)

    @pl.when(f == 0)
    def _():
        h = h_ref[...]
        hn_scr[...] = _rmsnorm(h, g_ref[...]).astype(BF16)
        o_ref[...] = h

    @pl.when(i % tiles_per_seq == 0)
    def _():
        prev_scr[f] = jnp.zeros(prev_scr.shape[1:], F32)

    hn = hn_scr[...]
    a = _dot(hn, wa_ref[...])
    conv = _causal_conv3(a, prev_scr[f], cw_ref, cb_ref)
    prev_scr[f] = a[a.shape[0] - SUBLANES:]
    act = (jax.nn.gelu(conv) * _dot(hn, wg_ref[...])).astype(BF16)
    o_ref[...] += _dot(act, wd_ref[...])

    if final_norm:
        @pl.when(f == pl.num_programs(1) - 1)
        def _():
            o_ref[...] = _rmsnorm(o_ref[...], gfin_ref[...])


def _ffn(h1, norm_g, w_up_bf, ffn_conv_w, ffn_conv_b, w_down_bf, norm_final, final_norm,
         seq_len, ff_tile=512):
    n, d = h1.shape
    d_ff = w_down_bf.shape[0]
    nff = d_ff // ff_tile
    tiles_per_seq = seq_len // TOKEN_TILE
    const = lambda i, f: (0, 0)
    return pl.pallas_call(
        functools.partial(_ffn_body, tiles_per_seq, final_norm),
        grid=(n // TOKEN_TILE, nff),
        in_specs=[
            pl.BlockSpec((TOKEN_TILE, d), lambda i, f: (i, 0)),
            pl.BlockSpec((1, d), const),
            pl.BlockSpec((d, ff_tile), lambda i, f: (0, f)),
            pl.BlockSpec((d, ff_tile), lambda i, f: (0, nff + f)),
            pl.BlockSpec((CONV_K, ff_tile), lambda i, f: (0, f)),
            pl.BlockSpec((1, ff_tile), lambda i, f: (0, f)),
            pl.BlockSpec((ff_tile, d), lambda i, f: (f, 0)),
            pl.BlockSpec((1, d), const),
        ],
        out_specs=pl.BlockSpec((TOKEN_TILE, d), lambda i, f: (i, 0)),
        out_shape=jax.ShapeDtypeStruct((n, d), F32),
        scratch_shapes=[pltpu.VMEM((TOKEN_TILE, d), BF16),
                        pltpu.VMEM((nff, SUBLANES, ff_tile), F32)],
        compiler_params=pltpu.CompilerParams(
            dimension_semantics=("arbitrary", "arbitrary"),
            vmem_limit_bytes=VMEM_LIMIT_BYTES),
        name="ffn",
    )(h1, norm_g, w_up_bf, w_up_bf, ffn_conv_w, ffn_conv_b, w_down_bf, norm_final)


def kernel(x, norm_tok, w_in, a_re, a_im, log_dt, b_re, b_im, c_re, c_im, d_skip, w_glu, w_ssm_out, conv_w, conv_b, w_conv_out, w_o, norm_ffn, w_up, ffn_conv_w, ffn_conv_b, w_down, norm_final):
    batch, seq_len, d = x.shape
    depth = w_in.shape[0]
    h = x.reshape(batch * seq_len, d)
    for l in range(depth):
        w_in_bf = w_in[l].astype(BF16)
        u, zb = _inproj(h, norm_tok[l][None], w_in_bf, conv_w[l], conv_b[l][None], seq_len)
        ops = _ssm_operators(a_re[l], a_im[l], log_dt[l], b_re[l], b_im[l],
                             c_re[l], c_im[l], d_skip[l])
        ya = _ssm(u, *ops, batch, seq_len)
        h1 = _mix(h, norm_tok[l][None], ya, zb, w_glu[l].astype(BF16), w_in_bf,
                  w_ssm_out[l].astype(BF16), w_conv_out[l].astype(BF16), w_o[l].astype(BF16))
        h = _ffn(h1, norm_ffn[l][None], w_up[l].astype(BF16), ffn_conv_w[l],
                 ffn_conv_b[l][None], w_down[l].astype(BF16), norm_final[None],
                 l == depth - 1, seq_len)
    return h.reshape(batch, seq_len, d)
```

```python
import functools

import jax
import jax.numpy as jnp
from jax import lax
from jax.experimental import pallas as pl
from jax.experimental.pallas import tpu as pltpu

SSM_GROUP = 16
SSM_STATE = 64
CONV_K = 3
EPS = 1e-6

LANES = 128
SUBLANES = 8
MXU_DIM = 256
VMEM_LIMIT_BYTES = 56 * 1024 * 1024

TOKEN_TILE = 512
CHUNK = 16
GROUPS_PER_LANE_BLOCK = LANES // SSM_GROUP
STATE_LANES = GROUPS_PER_LANE_BLOCK * SSM_STATE

F32 = jnp.float32
BF16 = jnp.bfloat16


def _rmsnorm(x, gain):
    return x * lax.rsqrt(jnp.mean(x * x, axis=-1, keepdims=True) + EPS) * gain


def _dot(a, b):
    return jnp.dot(a, b, preferred_element_type=F32)


def _causal_conv3(a, prev, w_ref, b_ref):
    w0 = w_ref[0:1, :]
    w1 = w_ref[1:2, :]
    w2 = w_ref[2:3, :]
    bias = b_ref[...]
    full = w2 * a + w1 * pltpu.roll(a, 1, 0) + w0 * pltpu.roll(a, 2, 0) + bias
    top = a[0:SUBLANES]
    rows = lax.broadcasted_iota(jnp.int32, top.shape, 0)
    s1 = jnp.where(rows < 1, pltpu.roll(prev, 1, 0), pltpu.roll(top, 1, 0))
    s2 = jnp.where(rows < 2, pltpu.roll(prev, 2, 0), pltpu.roll(top, 2, 0))
    fix = w2 * top + w1 * s1 + w0 * s2 + bias
    return jnp.concatenate([fix, full[SUBLANES:]], axis=0)


def _inproj_body(tiles_per_seq, x_ref, g_ref, wu_ref, wv_ref, wgb_ref, wgc_ref,
                 cw_ref, cb_ref, u_ref, zb_ref, prev_ref):
    i = pl.program_id(0)

    @pl.when(i % tiles_per_seq == 0)
    def _():
        prev_ref[...] = jnp.zeros_like(prev_ref)

    xn = _rmsnorm(x_ref[...], g_ref[...]).astype(BF16)
    u_ref[...] = _dot(xn, wu_ref[...])
    cin = _dot(xn, wgc_ref[...]) * _dot(xn, wv_ref[...])
    conv = _causal_conv3(cin, prev_ref[...], cw_ref, cb_ref)
    prev_ref[...] = cin[cin.shape[0] - SUBLANES:]
    zb_ref[...] = (_dot(xn, wgb_ref[...]) * conv).astype(BF16)


def _inproj(x2, norm_g, w_in_bf, conv_w, conv_b, seq_len):
    n, d = x2.shape
    cw = conv_w.shape[1]
    tiles_per_seq = seq_len // TOKEN_TILE
    const = lambda i: (0, 0)
    wspec = lambda col: pl.BlockSpec((d, cw), lambda i, col=col: (0, col),
                                     pipeline_mode=pl.Buffered(1))
    return pl.pallas_call(
        functools.partial(_inproj_body, tiles_per_seq),
        grid=(n // TOKEN_TILE,),
        in_specs=[
            pl.BlockSpec((TOKEN_TILE, d), lambda i: (i, 0)),
            pl.BlockSpec((1, d), const),
            wspec(0), wspec(1), wspec(2), wspec(3),
            pl.BlockSpec((CONV_K, cw), const),
            pl.BlockSpec((1, cw), const),
        ],
        out_specs=[
            pl.BlockSpec((TOKEN_TILE, cw), lambda i: (i, 0)),
            pl.BlockSpec((TOKEN_TILE, cw), lambda i: (i, 0)),
        ],
        out_shape=[
            jax.ShapeDtypeStruct((n, cw), F32),
            jax.ShapeDtypeStruct((n, cw), BF16),
        ],
        scratch_shapes=[pltpu.VMEM((SUBLANES, cw), F32)],
        compiler_params=pltpu.CompilerParams(
            dimension_semantics=("arbitrary",), vmem_limit_bytes=VMEM_LIMIT_BYTES),
        name="inproj",
    )(x2, norm_g, w_in_bf, w_in_bf, w_in_bf, w_in_bf, conv_w, conv_b)


def _ssm_body(u_ref, tz_ref, p_ref, q_ref, pw_ref, d_ref, y_ref, s_scr, xin_scr):
    m = u_ref.shape[0] // CHUNK
    sl = STATE_LANES
    ngl = GROUPS_PER_LANE_BLOCK
    pieces = [u_ref[pl.ds(t, m, stride=CHUNK), :] for t in range(CHUNK)]
    pts = [p.T.astype(BF16) for p in pieces]

    yts, s_re, s_im = [], [], []
    for g in range(ngl):
        xg = jnp.concatenate([pt[g * SSM_GROUP:(g + 1) * SSM_GROUP] for pt in pts], axis=0)
        yts.append(_dot(tz_ref[g], xg))
        sg = _dot(p_ref[g], xg)
        s_re.append(sg[:SSM_STATE])
        s_im.append(sg[SSM_STATE:])
    s_scr[...] = jnp.concatenate(s_re + s_im, axis=0).T

    a16r = pw_ref[0:1, :]
    a16i = pw_ref[1:2, :]

    def step(k, carry):
        xr, xi = carry
        xin_scr[pl.ds(k, 1), :sl] = xr
        xin_scr[pl.ds(k, 1), sl:] = xi
        row = s_scr[pl.ds(k, 1), :]
        return (a16r * xr - a16i * xi + row[:, :sl], a16r * xi + a16i * xr + row[:, sl:])

    zero = jnp.zeros((1, sl), F32)
    lax.fori_loop(0, m, step, (zero, zero), unroll=8)

    xint = xin_scr[...].T
    for g in range(ngl):
        xg = jnp.concatenate([xint[g * SSM_STATE:(g + 1) * SSM_STATE],
                              xint[sl + g * SSM_STATE:sl + (g + 1) * SSM_STATE]], axis=0)
        yts[g] = yts[g] + _dot(q_ref[g], xg.astype(BF16))

    d = d_ref[...]
    for t in range(CHUNK):
        yt = jnp.concatenate([y[t * SSM_GROUP:(t + 1) * SSM_GROUP] for y in yts], axis=0)
        y_ref[pl.ds(t, m, stride=CHUNK), :] = jax.nn.gelu(yt.T + d * pieces[t])


def _ssm(u, tz, pm, qm, pw, dsk, batch, seq_len):
    n, width = u.shape
    nblk = width // LANES
    m = seq_len // CHUNK
    ngl = GROUPS_PER_LANE_BLOCK
    return pl.pallas_call(
        _ssm_body,
        grid=(batch, nblk),
        in_specs=[
            pl.BlockSpec((seq_len, LANES), lambda b, c: (b, c)),
            pl.BlockSpec((ngl, MXU_DIM, MXU_DIM), lambda b, c: (c, 0, 0)),
            pl.BlockSpec((ngl, 2 * SSM_STATE, MXU_DIM), lambda b, c: (c, 0, 0)),
            pl.BlockSpec((ngl, MXU_DIM, 2 * SSM_STATE), lambda b, c: (c, 0, 0)),
            pl.BlockSpec((None, 2, STATE_LANES), lambda b, c: (c, 0, 0)),
            pl.BlockSpec((None, 1, LANES), lambda b, c: (c, 0, 0)),
        ],
        out_specs=pl.BlockSpec((seq_len, LANES), lambda b, c: (b, c)),
        out_shape=jax.ShapeDtypeStruct((n, width), F32),
        scratch_shapes=[pltpu.VMEM((m, 2 * STATE_LANES), F32),
                        pltpu.VMEM((m, 2 * STATE_LANES), F32)],
        compiler_params=pltpu.CompilerParams(
            dimension_semantics=("arbitrary", "arbitrary"),
            vmem_limit_bytes=VMEM_LIMIT_BYTES),
        name="ssm",
    )(u, tz, pm, qm, pw, dsk)


def _ssm_operators(a_re, a_im, log_dt, b_re, b_im, c_re, c_im, d_skip):
    hp = lax.Precision.HIGHEST
    g = a_re.shape[0]
    nblk = g // GROUPS_PER_LANE_BLOCK
    dt = jnp.exp(log_dt)[:, None]
    mag = jnp.exp(dt * a_re)
    abar_re = mag * jnp.cos(dt * a_im)
    abar_im = mag * jnp.sin(dt * a_im)
    nr = abar_re - 1.0
    ni = abar_im
    den = a_re * a_re + a_im * a_im
    fr = (nr * a_re + ni * a_im) / den
    fi = (ni * a_re - nr * a_im) / den
    bb_re = fr[..., None] * b_re - fi[..., None] * b_im
    bb_im = fr[..., None] * b_im + fi[..., None] * b_re
    nn = jnp.arange(CHUNK + 1, dtype=F32)[:, None, None]
    pmag = jnp.exp(nn * (dt * a_re)[None])
    pw_re = pmag * jnp.cos(nn * (dt * a_im)[None])
    pw_im = pmag * jnp.sin(nn * (dt * a_im)[None])

    w_re = pw_re[:CHUNK, :, :, None] * bb_re[None] - pw_im[:CHUNK, :, :, None] * bb_im[None]
    w_im = pw_re[:CHUNK, :, :, None] * bb_im[None] + pw_im[:CHUNK, :, :, None] * bb_re[None]
    kern = (jnp.einsum("gop,ngpi->ngoi", c_re, w_re, precision=hp)
            - jnp.einsum("gop,ngpi->ngoi", c_im, w_im, precision=hp))

    t_out = jnp.arange(CHUNK)[:, None]
    t_in = jnp.arange(CHUNK)[None, :]
    lag = t_out - t_in
    tz = kern[jnp.clip(lag, 0, CHUNK - 1)] * (lag >= 0)[..., None, None, None].astype(F32)
    tz = tz.transpose(2, 0, 3, 1, 4).reshape(g, MXU_DIM, MXU_DIM).astype(BF16)

    pm = jnp.stack([w_re[::-1], w_im[::-1]])
    pm = pm.transpose(2, 0, 3, 1, 4).reshape(g, 2 * SSM_STATE, MXU_DIM).astype(BF16)

    v_re = c_re[None] * pw_re[1:, :, None, :] - c_im[None] * pw_im[1:, :, None, :]
    v_im = c_re[None] * pw_im[1:, :, None, :] + c_im[None] * pw_re[1:, :, None, :]
    qm = jnp.stack([v_re, -v_im])
    qm = qm.transpose(2, 1, 3, 0, 4).reshape(g, MXU_DIM, 2 * SSM_STATE).astype(BF16)

    pw = jnp.stack([pw_re[CHUNK], pw_im[CHUNK]])
    pw = pw.reshape(2, nblk, STATE_LANES).transpose(1, 0, 2)
    dsk = d_skip.reshape(nblk, 1, LANES)
    return tz, pm, qm, pw, dsk


def _mix_body(x_ref, g_ref, ya_ref, zb_ref, wglu_ref, wma_ref, wmb_ref, wssm_ref,
              wconv_ref, wo_ref, h_ref, xn_scr, ya2_scr):
    j = pl.program_id(1)

    @pl.when(j == 0)
    def _():
        x = x_ref[...]
        xn_scr[...] = _rmsnorm(x, g_ref[...]).astype(BF16)
        ya = ya_ref[...]
        ya2_scr[...] = (ya * jax.nn.sigmoid(_dot(ya.astype(BF16), wglu_ref[...]))).astype(BF16)
        h_ref[...] = x

    xn = xn_scr[...]
    pa = _dot(ya2_scr[...], wssm_ref[...])
    pb = _dot(zb_ref[...], wconv_ref[...])
    merged = (jax.nn.sigmoid(_dot(xn, wma_ref[...])) * pa
              + jax.nn.sigmoid(_dot(xn, wmb_ref[...])) * pb)
    h_ref[...] += _dot(merged.astype(BF16), wo_ref[...])


def _mix(x2, norm_g, ya, zb, w_glu_bf, w_in_bf, w_ssm_bf, w_conv_bf, w_o_bf, col_tile=512):
    n, d = x2.shape
    sw = ya.shape[1]
    cw = zb.shape[1]
    ncol = d // col_tile
    merge_a0 = (w_in_bf.shape[1] - 2 * d) // col_tile
    merge_b0 = merge_a0 + ncol
    const = lambda i, j: (0, 0)
    return pl.pallas_call(
        _mix_body,
        grid=(n // TOKEN_TILE, ncol),
        in_specs=[
            pl.BlockSpec((TOKEN_TILE, d), lambda i, j: (i, 0)),
            pl.BlockSpec((1, d), const),
            pl.BlockSpec((TOKEN_TILE, sw), lambda i, j: (i, 0)),
            pl.BlockSpec((TOKEN_TILE, cw), lambda i, j: (i, 0)),
            pl.BlockSpec((sw, sw), const),
            pl.BlockSpec((d, col_tile), lambda i, j: (0, merge_a0 + j)),
            pl.BlockSpec((d, col_tile), lambda i, j: (0, merge_b0 + j)),
            pl.BlockSpec((sw, col_tile), lambda i, j: (0, j)),
            pl.BlockSpec((cw, col_tile), lambda i, j: (0, j)),
            pl.BlockSpec((col_tile, d), lambda i, j: (j, 0)),
        ],
        out_specs=pl.BlockSpec((TOKEN_TILE, d), lambda i, j: (i, 0)),
        out_shape=jax.ShapeDtypeStruct((n, d), F32),
        scratch_shapes=[pltpu.VMEM((TOKEN_TILE, d), BF16),
                        pltpu.VMEM((TOKEN_TILE, sw), BF16)],
        compiler_params=pltpu.CompilerParams(
            dimension_semantics=("arbitrary", "arbitrary"),
            vmem_limit_bytes=VMEM_LIMIT_BYTES),
        name="mix",
    )(x2, norm_g, ya, zb, w_glu_bf, w_in_bf, w_in_bf, w_ssm_bf, w_conv_bf, w_o_bf)


def _ffn_body(tiles_per_seq, final_norm, h_ref, g_ref, wa_ref, wg_ref, cw_ref, cb_ref,
              wd_ref, gfin_ref, o_ref, hn_scr, prev_scr):
    i = pl.program_id(0)
    f = pl.program_id(1)

    @pl.when(f == 0)
    def _():
        h = h_ref[...]
        hn_scr[...] = _rmsnorm(h, g_ref[...]).astype(BF16)
        o_ref[...] = h

    @pl.when(i % tiles_per_seq == 0)
    def _():
        prev_scr[f] = jnp.zeros(prev_scr.shape[1:], F32)

    hn = hn_scr[...]
    a = _dot(hn, wa_ref[...])
    conv = _causal_conv3(a, prev_scr[f], cw_ref, cb_ref)
    prev_scr[f] = a[a.shape[0] - SUBLANES:]
    act = (jax.nn.gelu(conv) * _dot(hn, wg_ref[...])).astype(BF16)
    o_ref[...] += _dot(act, wd_ref[...])

    if final_norm:
        @pl.when(f == pl.num_programs(1) - 1)
        def _():
            o_ref[...] = _rmsnorm(o_ref[...], gfin_ref[...])


def _ffn(h1, norm_g, w_up_bf, ffn_conv_w, ffn_conv_b, w_down_bf, norm_final, final_norm,
         seq_len, ff_tile=512):
    n, d = h1.shape
    d_ff = w_down_bf.shape[0]
    nff = d_ff // ff_tile
    tiles_per_seq = seq_len // TOKEN_TILE
    const = lambda i, f: (0, 0)
    return pl.pallas_call(
        functools.partial(_ffn_body, tiles_per_seq, final_norm),
        grid=(n // TOKEN_TILE, nff),
        in_specs=[
            pl.BlockSpec((TOKEN_TILE, d), lambda i, f: (i, 0)),
            pl.BlockSpec((1, d), const),
            pl.BlockSpec((d, ff_tile), lambda i, f: (0, f)),
            pl.BlockSpec((d, ff_tile), lambda i, f: (0, nff + f)),
            pl.BlockSpec((CONV_K, ff_tile), lambda i, f: (0, f)),
            pl.BlockSpec((1, ff_tile), lambda i, f: (0, f)),
            pl.BlockSpec((ff_tile, d), lambda i, f: (f, 0)),
            pl.BlockSpec((1, d), const),
        ],
        out_specs=pl.BlockSpec((TOKEN_TILE, d), lambda i, f: (i, 0)),
        out_shape=jax.ShapeDtypeStruct((n, d), F32),
        scratch_shapes=[pltpu.VMEM((TOKEN_TILE, d), BF16),
                        pltpu.VMEM((nff, SUBLANES, ff_tile), F32)],
        compiler_params=pltpu.CompilerParams(
            dimension_semantics=("arbitrary", "arbitrary"),
            vmem_limit_bytes=VMEM_LIMIT_BYTES),
        name="ffn",
    )(h1, norm_g, w_up_bf, w_up_bf, ffn_conv_w, ffn_conv_b, w_down_bf, norm_final)


def kernel(x, norm_tok, w_in, a_re, a_im, log_dt, b_re, b_im, c_re, c_im, d_skip, w_glu, w_ssm_out, conv_w, conv_b, w_conv_out, w_o, norm_ffn, w_up, ffn_conv_w, ffn_conv_b, w_down, norm_final):
    batch, seq_len, d = x.shape
    depth = w_in.shape[0]
    h = x.reshape(batch * seq_len, d)
    for l in range(depth):
        w_in_bf = w_in[l].astype(BF16)
        u, zb = _inproj(h, norm_tok[l][None], w_in_bf, conv_w[l], conv_b[l][None], seq_len)
        ops = _ssm_operators(a_re[l], a_im[l], log_dt[l], b_re[l], b_im[l],
                             c_re[l], c_im[l], d_skip[l])
        ya = _ssm(u, *ops, batch, seq_len)
        h1 = _mix(h, norm_tok[l][None], ya, zb, w_glu[l].astype(BF16), w_in_bf,
                  w_ssm_out[l].astype(BF16), w_conv_out[l].astype(BF16), w_o[l].astype(BF16))
        h = _ffn(h1, norm_ffn[l][None], w_up[l].astype(BF16), ffn_conv_w[l],
                 ffn_conv_b[l][None], w_down[l].astype(BF16), norm_final[None],
                 l == depth - 1, seq_len)
    return h.reshape(batch, seq_len, d)
```

```python
import functools

import jax
import jax.numpy as jnp
from jax import lax
from jax.experimental import pallas as pl
from jax.experimental.pallas import tpu as pltpu

SSM_GROUP = 16
SSM_STATE = 64
CONV_K = 3
EPS = 1e-6

LANES = 128
SUBLANES = 8
MXU_DIM = 256
VMEM_LIMIT_BYTES = 56 * 1024 * 1024

TOKEN_TILE = 512
CHUNK = 16
GROUPS_PER_LANE_BLOCK = LANES // SSM_GROUP
STATE_LANES = GROUPS_PER_LANE_BLOCK * SSM_STATE

F32 = jnp.float32
BF16 = jnp.bfloat16


def _rmsnorm(x, gain):
    return x * lax.rsqrt(jnp.mean(x * x, axis=-1, keepdims=True) + EPS) * gain


def _dot(a, b):
    return jnp.dot(a, b, preferred_element_type=F32)


def _causal_conv3(a, prev, w_ref, b_ref):
    w0 = w_ref[0:1, :]
    w1 = w_ref[1:2, :]
    w2 = w_ref[2:3, :]
    bias = b_ref[...]
    full = w2 * a + w1 * pltpu.roll(a, 1, 0) + w0 * pltpu.roll(a, 2, 0) + bias
    top = a[0:SUBLANES]
    rows = lax.broadcasted_iota(jnp.int32, top.shape, 0)
    s1 = jnp.where(rows < 1, pltpu.roll(prev, 1, 0), pltpu.roll(top, 1, 0))
    s2 = jnp.where(rows < 2, pltpu.roll(prev, 2, 0), pltpu.roll(top, 2, 0))
    fix = w2 * top + w1 * s1 + w0 * s2 + bias
    return jnp.concatenate([fix, full[SUBLANES:]], axis=0)


def _inproj_body(tiles_per_seq, x_ref, g_ref, wu_ref, wv_ref, wgb_ref, wgc_ref,
                 cw_ref, cb_ref, u_ref, zb_ref, prev_ref):
    i = pl.program_id(0)

    @pl.when(i % tiles_per_seq == 0)
    def _():
        prev_ref[...] = jnp.zeros_like(prev_ref)

    xn = _rmsnorm(x_ref[...], g_ref[...]).astype(BF16)
    u_ref[...] = _dot(xn, wu_ref[...])
    cin = _dot(xn, wgc_ref[...]) * _dot(xn, wv_ref[...])
    conv = _causal_conv3(cin, prev_ref[...], cw_ref, cb_ref)
    prev_ref[...] = cin[cin.shape[0] - SUBLANES:]
    zb_ref[...] = (_dot(xn, wgb_ref[...]) * conv).astype(BF16)


def _inproj(x2, norm_g, w_in_bf, conv_w, conv_b, seq_len):
    n, d = x2.shape
    cw = conv_w.shape[1]
    tiles_per_seq = seq_len // TOKEN_TILE
    const = lambda i: (0, 0)
    wspec = lambda col: pl.BlockSpec((d, cw), lambda i, col=col: (0, col),
                                     pipeline_mode=pl.Buffered(1))
    return pl.pallas_call(
        functools.partial(_inproj_body, tiles_per_seq),
        grid=(n // TOKEN_TILE,),
        in_specs=[
            pl.BlockSpec((TOKEN_TILE, d), lambda i: (i, 0)),
            pl.BlockSpec((1, d), const),
            wspec(0), wspec(1), wspec(2), wspec(3),
            pl.BlockSpec((CONV_K, cw), const),
            pl.BlockSpec((1, cw), const),
        ],
        out_specs=[
            pl.BlockSpec((TOKEN_TILE, cw), lambda i: (i, 0)),
            pl.BlockSpec((TOKEN_TILE, cw), lambda i: (i, 0)),
        ],
        out_shape=[
            jax.ShapeDtypeStruct((n, cw), F32),
            jax.ShapeDtypeStruct((n, cw), BF16),
        ],
        scratch_shapes=[pltpu.VMEM((SUBLANES, cw), F32)],
        compiler_params=pltpu.CompilerParams(
            dimension_semantics=("arbitrary",), vmem_limit_bytes=VMEM_LIMIT_BYTES),
        name="inproj",
    )(x2, norm_g, w_in_bf, w_in_bf, w_in_bf, w_in_bf, conv_w, conv_b)


def _ssm_body(u_ref, tz_ref, p_ref, q_ref, pw_ref, d_ref, y_ref, s_scr, xin_scr):
    m = u_ref.shape[0] // CHUNK
    sl = STATE_LANES
    ngl = GROUPS_PER_LANE_BLOCK
    pieces = [u_ref[pl.ds(t, m, stride=CHUNK), :] for t in range(CHUNK)]
    pts = [p.T.astype(BF16) for p in pieces]

    yts, s_re, s_im = [], [], []
    for g in range(ngl):
        xg = jnp.concatenate([pt[g * SSM_GROUP:(g + 1) * SSM_GROUP] for pt in pts], axis=0)
        yts.append(_dot(tz_ref[g], xg))
        sg = _dot(p_ref[g], xg)
        s_re.append(sg[:SSM_STATE])
        s_im.append(sg[SSM_STATE:])
    s_scr[...] = jnp.concatenate(s_re + s_im, axis=0).T

    a16r = pw_ref[0:1, :]
    a16i = pw_ref[1:2, :]

    def step(k, carry):
        xr, xi = carry
        xin_scr[pl.ds(k, 1), :sl] = xr
        xin_scr[pl.ds(k, 1), sl:] = xi
        row = s_scr[pl.ds(k, 1), :]
        return (a16r * xr - a16i * xi + row[:, :sl], a16r * xi + a16i * xr + row[:, sl:])

    zero = jnp.zeros((1, sl), F32)
    lax.fori_loop(0, m, step, (zero, zero), unroll=8)

    xint = xin_scr[...].T
    for g in range(ngl):
        xg = jnp.concatenate([xint[g * SSM_STATE:(g + 1) * SSM_STATE],
                              xint[sl + g * SSM_STATE:sl + (g + 1) * SSM_STATE]], axis=0)
        yts[g] = yts[g] + _dot(q_ref[g], xg.astype(BF16))

    d = d_ref[...]
    for t in range(CHUNK):
        yt = jnp.concatenate([y[t * SSM_GROUP:(t + 1) * SSM_GROUP] for y in yts], axis=0)
        y_ref[pl.ds(t, m, stride=CHUNK), :] = jax.nn.gelu(yt.T + d * pieces[t])


def _ssm(u, tz, pm, qm, pw, dsk, batch, seq_len):
    n, width = u.shape
    nblk = width // LANES
    m = seq_len // CHUNK
    ngl = GROUPS_PER_LANE_BLOCK
    return pl.pallas_call(
        _ssm_body,
        grid=(batch, nblk),
        in_specs=[
            pl.BlockSpec((seq_len, LANES), lambda b, c: (b, c)),
            pl.BlockSpec((ngl, MXU_DIM, MXU_DIM), lambda b, c: (c, 0, 0)),
            pl.BlockSpec((ngl, 2 * SSM_STATE, MXU_DIM), lambda b, c: (c, 0, 0)),
            pl.BlockSpec((ngl, MXU_DIM, 2 * SSM_STATE), lambda b, c: (c, 0, 0)),
            pl.BlockSpec((None, 2, STATE_LANES), lambda b, c: (c, 0, 0)),
            pl.BlockSpec((None, 1, LANES), lambda b, c: (c, 0, 0)),
        ],
        out_specs=pl.BlockSpec((seq_len, LANES), lambda b, c: (b, c)),
        out_shape=jax.ShapeDtypeStruct((n, width), F32),
        scratch_shapes=[pltpu.VMEM((m, 2 * STATE_LANES), F32),
                        pltpu.VMEM((m, 2 * STATE_LANES), F32)],
        compiler_params=pltpu.CompilerParams(
            dimension_semantics=("arbitrary", "arbitrary"),
            vmem_limit_bytes=VMEM_LIMIT_BYTES),
        name="ssm",
    )(u, tz, pm, qm, pw, dsk)


def _ssm_operators(a_re, a_im, log_dt, b_re, b_im, c_re, c_im, d_skip):
    hp = lax.Precision.HIGHEST
    g = a_re.shape[0]
    nblk = g // GROUPS_PER_LANE_BLOCK
    dt = jnp.exp(log_dt)[:, None]
    mag = jnp.exp(dt * a_re)
    abar_re = mag * jnp.cos(dt * a_im)
    abar_im = mag * jnp.sin(dt * a_im)
    nr = abar_re - 1.0
    ni = abar_im
    den = a_re * a_re + a_im * a_im
    fr = (nr * a_re + ni * a_im) / den
    fi = (ni * a_re - nr * a_im) / den
    bb_re = fr[..., None] * b_re - fi[..., None] * b_im
    bb_im = fr[..., None] * b_im + fi[..., None] * b_re
    lam_re = dt * a_re
    lam_im = dt * a_im

    col = jnp.arange(MXU_DIM)
    n_col = (CHUNK - 1 - col // SSM_GROUP).astype(F32)
    emag = jnp.exp(n_col * lam_re[..., None])
    e_re = emag * jnp.cos(n_col * lam_im[..., None])
    e_im = emag * jnp.sin(n_col * lam_im[..., None])
    spread = (col[None, :] % SSM_GROUP == jnp.arange(SSM_GROUP)[:, None]).astype(F32)
    bx_re = jnp.einsum("gpi,ic->gpc", bb_re, spread, precision=hp)
    bx_im = jnp.einsum("gpi,ic->gpc", bb_im, spread, precision=hp)
    p_re = e_re * bx_re - e_im * bx_im
    p_im = e_re * bx_im + e_im * bx_re
    pm = jnp.concatenate([p_re, p_im], axis=1).astype(BF16)

    krev = (jnp.einsum("gop,gpc->goc", c_re, p_re, precision=hp)
            - jnp.einsum("gop,gpc->goc", c_im, p_im, precision=hp))
    padded = jnp.concatenate([krev, jnp.zeros_like(krev)], axis=-1)
    tz = jnp.stack([padded[..., (CHUNK - 1 - t) * SSM_GROUP:(CHUNK - 1 - t) * SSM_GROUP + MXU_DIM]
                    for t in range(CHUNK)], axis=1)
    tz = tz.reshape(g, MXU_DIM, MXU_DIM).astype(BF16)

    n_out = jnp.arange(1, CHUNK + 1, dtype=F32)[None, :, None]
    omag = jnp.exp(n_out * lam_re[:, None, :])
    o_re = (omag * jnp.cos(n_out * lam_im[:, None, :]))[:, :, None, :]
    o_im = (omag * jnp.sin(n_out * lam_im[:, None, :]))[:, :, None, :]
    cr = c_re[:, None]
    ci = c_im[:, None]
    qm = jnp.concatenate([cr * o_re - ci * o_im, -(cr * o_im + ci * o_re)], axis=-1)
    qm = qm.reshape(g, MXU_DIM, 2 * SSM_STATE).astype(BF16)

    cmag = jnp.exp(CHUNK * lam_re)
    pw = jnp.stack([cmag * jnp.cos(CHUNK * lam_im), cmag * jnp.sin(CHUNK * lam_im)])
    pw = pw.reshape(2, nblk, STATE_LANES).transpose(1, 0, 2)
    dsk = d_skip.reshape(nblk, 1, LANES)
    return tz, pm, qm, pw, dsk


def _mix_body(x_ref, g_ref, ya_ref, zb_ref, wglu_ref, wma_ref, wmb_ref, wssm_ref,
              wconv_ref, wo_ref, h_ref, xn_scr, ya2_scr):
    j = pl.program_id(1)

    @pl.when(j == 0)
    def _():
        x = x_ref[...]
        xn_scr[...] = _rmsnorm(x, g_ref[...]).astype(BF16)
        ya = ya_ref[...]
        ya2_scr[...] = (ya * jax.nn.sigmoid(_dot(ya.astype(BF16), wglu_ref[...]))).astype(BF16)
        h_ref[...] = x

    xn = xn_scr[...]
    pa = _dot(ya2_scr[...], wssm_ref[...])
    pb = _dot(zb_ref[...], wconv_ref[...])
    merged = (jax.nn.sigmoid(_dot(xn, wma_ref[...])) * pa
              + jax.nn.sigmoid(_dot(xn, wmb_ref[...])) * pb)
    h_ref[...] += _dot(merged.astype(BF16), wo_ref[...])


def _mix(x2, norm_g, ya, zb, w_glu_bf, w_in_bf, w_ssm_bf, w_conv_bf, w_o_bf, col_tile=512):
    n, d = x2.shape
    sw = ya.shape[1]
    cw = zb.shape[1]
    ncol = d // col_tile
    merge_a0 = (w_in_bf.shape[1] - 2 * d) // col_tile
    merge_b0 = merge_a0 + ncol
    const = lambda i, j: (0, 0)
    return pl.pallas_call(
        _mix_body,
        grid=(n // TOKEN_TILE, ncol),
        in_specs=[
            pl.BlockSpec((TOKEN_TILE, d), lambda i, j: (i, 0)),
            pl.BlockSpec((1, d), const),
            pl.BlockSpec((TOKEN_TILE, sw), lambda i, j: (i, 0)),
            pl.BlockSpec((TOKEN_TILE, cw), lambda i, j: (i, 0)),
            pl.BlockSpec((sw, sw), const),
            pl.BlockSpec((d, col_tile), lambda i, j: (0, merge_a0 + j)),
            pl.BlockSpec((d, col_tile), lambda i, j: (0, merge_b0 + j)),
            pl.BlockSpec((sw, col_tile), lambda i, j: (0, j)),
            pl.BlockSpec((cw, col_tile), lambda i, j: (0, j)),
            pl.BlockSpec((col_tile, d), lambda i, j: (j, 0)),
        ],
        out_specs=pl.BlockSpec((TOKEN_TILE, d), lambda i, j: (i, 0)),
        out_shape=jax.ShapeDtypeStruct((n, d), F32),
        scratch_shapes=[pltpu.VMEM((TOKEN_TILE, d), BF16),
                        pltpu.VMEM((TOKEN_TILE, sw), BF16)],
        compiler_params=pltpu.CompilerParams(
            dimension_semantics=("arbitrary", "arbitrary"),
            vmem_limit_bytes=VMEM_LIMIT_BYTES),
        name="mix",
    )(x2, norm_g, ya, zb, w_glu_bf, w_in_bf, w_in_bf, w_ssm_bf, w_conv_bf, w_o_bf)


def _ffn_body(tiles_per_seq, final_norm, h_ref, g_ref, wa_ref, wg_ref, cw_ref, cb_ref,
              wd_ref, gfin_ref, o_ref, hn_scr, prev_scr):
    i = pl.program_id(0)
    f = pl.program_id(1)

    @pl.when(f == 0)
    def _():
        h = h_ref[...]
        hn_scr[...] = _rmsnorm(h, g_ref[...]).astype(BF16)
        o_ref[...] = h

    @pl.when(i % tiles_per_seq == 0)
    def _():
        prev_scr[f] = jnp.zeros(prev_scr.shape[1:], F32)

    hn = hn_scr[...]
    a = _dot(hn, wa_ref[...])
    conv = _causal_conv3(a, prev_scr[f], cw_ref, cb_ref)
    prev_scr[f] = a[a.shape[0] - SUBLANES:]
    act = (jax.nn.gelu(conv) * _dot(hn, wg_ref[...])).astype(BF16)
    o_ref[...] += _dot(act, wd_ref[...])

    if final_norm:
        @pl.when(f == pl.num_programs(1) - 1)
        def _():
            o_ref[...] = _rmsnorm(o_ref[...], gfin_ref[...])


def _ffn(h1, norm_g, w_up_bf, ffn_conv_w, ffn_conv_b, w_down_bf, norm_final, final_norm,
         seq_len, ff_tile=512):
    n, d = h1.shape
    d_ff = w_down_bf.shape[0]
    nff = d_ff // ff_tile
    tiles_per_seq = seq_len // TOKEN_TILE
    const = lambda i, f: (0, 0)
    return pl.pallas_call(
        functools.partial(_ffn_body, tiles_per_seq, final_norm),
        grid=(n // TOKEN_TILE, nff),
        in_specs=[
            pl.BlockSpec((TOKEN_TILE, d), lambda i, f: (i, 0)),
            pl.BlockSpec((1, d), const),
            pl.BlockSpec((d, ff_tile), lambda i, f: (0, f)),
            pl.BlockSpec((d, ff_tile), lambda i, f: (0, nff + f)),
            pl.BlockSpec((CONV_K, ff_tile), lambda i, f: (0, f)),
            pl.BlockSpec((1, ff_tile), lambda i, f: (0, f)),
            pl.BlockSpec((ff_tile, d), lambda i, f: (f, 0)),
            pl.BlockSpec((1, d), const),
        ],
        out_specs=pl.BlockSpec((TOKEN_TILE, d), lambda i, f: (i, 0)),
        out_shape=jax.ShapeDtypeStruct((n, d), F32),
        scratch_shapes=[pltpu.VMEM((TOKEN_TILE, d), BF16),
                        pltpu.VMEM((nff, SUBLANES, ff_tile), F32)],
        compiler_params=pltpu.CompilerParams(
            dimension_semantics=("arbitrary", "arbitrary"),
            vmem_limit_bytes=VMEM_LIMIT_BYTES),
        name="ffn",
    )(h1, norm_g, w_up_bf, w_up_bf, ffn_conv_w, ffn_conv_b, w_down_bf, norm_final)


def kernel(x, norm_tok, w_in, a_re, a_im, log_dt, b_re, b_im, c_re, c_im, d_skip, w_glu, w_ssm_out, conv_w, conv_b, w_conv_out, w_o, norm_ffn, w_up, ffn_conv_w, ffn_conv_b, w_down, norm_final):
    batch, seq_len, d = x.shape
    depth = w_in.shape[0]
    h = x.reshape(batch * seq_len, d)
    for l in range(depth):
        w_in_bf = w_in[l].astype(BF16)
        u, zb = _inproj(h, norm_tok[l][None], w_in_bf, conv_w[l], conv_b[l][None], seq_len)
        ops = _ssm_operators(a_re[l], a_im[l], log_dt[l], b_re[l], b_im[l],
                             c_re[l], c_im[l], d_skip[l])
        ya = _ssm(u, *ops, batch, seq_len)
        h1 = _mix(h, norm_tok[l][None], ya, zb, w_glu[l].astype(BF16), w_in_bf,
                  w_ssm_out[l].astype(BF16), w_conv_out[l].astype(BF16), w_o[l].astype(BF16))
        h = _ffn(h1, norm_ffn[l][None], w_up[l].astype(BF16), ffn_conv_w[l],
                 ffn_conv_b[l][None], w_down[l].astype(BF16), norm_final[None],
                 l == depth - 1, seq_len)
    return h.reshape(batch, seq_len, d)
```

```python
import functools

import jax
import jax.numpy as jnp
from jax import lax
from jax.experimental import pallas as pl
from jax.experimental.pallas import tpu as pltpu

SSM_GROUP = 16
SSM_STATE = 64
CONV_K = 3
EPS = 1e-6

LANES = 128
SUBLANES = 8
MXU_DIM = 256
VMEM_LIMIT_BYTES = 56 * 1024 * 1024

TOKEN_TILE = 512
CHUNK = 16
GROUPS_PER_LANE_BLOCK = LANES // SSM_GROUP
STATE_LANES = GROUPS_PER_LANE_BLOCK * SSM_STATE

F32 = jnp.float32
BF16 = jnp.bfloat16


def _rmsnorm(x, gain):
    return x * lax.rsqrt(jnp.mean(x * x, axis=-1, keepdims=True) + EPS) * gain


def _dot(a, b):
    return jnp.dot(a, b, preferred_element_type=F32)


def _causal_conv3(a, prev, w_ref, b_ref):
    w0 = w_ref[0:1, :]
    w1 = w_ref[1:2, :]
    w2 = w_ref[2:3, :]
    bias = b_ref[...]
    full = w2 * a + w1 * pltpu.roll(a, 1, 0) + w0 * pltpu.roll(a, 2, 0) + bias
    top = a[0:SUBLANES]
    rows = lax.broadcasted_iota(jnp.int32, top.shape, 0)
    s1 = jnp.where(rows < 1, pltpu.roll(prev, 1, 0), pltpu.roll(top, 1, 0))
    s2 = jnp.where(rows < 2, pltpu.roll(prev, 2, 0), pltpu.roll(top, 2, 0))
    fix = w2 * top + w1 * s1 + w0 * s2 + bias
    return jnp.concatenate([fix, full[SUBLANES:]], axis=0)


def _inproj_body(tiles_per_seq, x_ref, g_ref, wu_ref, wv_ref, wgb_ref, wgc_ref,
                 cw_ref, cb_ref, u_ref, zb_ref, prev_ref):
    i = pl.program_id(0)

    @pl.when(i % tiles_per_seq == 0)
    def _():
        prev_ref[...] = jnp.zeros_like(prev_ref)

    xn = _rmsnorm(x_ref[...], g_ref[...]).astype(BF16)
    u_ref[...] = _dot(xn, wu_ref[...])
    cin = _dot(xn, wgc_ref[...]) * _dot(xn, wv_ref[...])
    conv = _causal_conv3(cin, prev_ref[...], cw_ref, cb_ref)
    prev_ref[...] = cin[cin.shape[0] - SUBLANES:]
    zb_ref[...] = (_dot(xn, wgb_ref[...]) * conv).astype(BF16)


def _inproj(x2, norm_g, w_in_bf, conv_w, conv_b, seq_len):
    n, d = x2.shape
    cw = conv_w.shape[1]
    tiles_per_seq = seq_len // TOKEN_TILE
    const = lambda i: (0, 0)
    wspec = lambda col: pl.BlockSpec((d, cw), lambda i, col=col: (0, col),
                                     pipeline_mode=pl.Buffered(1))
    return pl.pallas_call(
        functools.partial(_inproj_body, tiles_per_seq),
        grid=(n // TOKEN_TILE,),
        in_specs=[
            pl.BlockSpec((TOKEN_TILE, d), lambda i: (i, 0)),
            pl.BlockSpec((1, d), const),
            wspec(0), wspec(1), wspec(2), wspec(3),
            pl.BlockSpec((CONV_K, cw), const),
            pl.BlockSpec((1, cw), const),
        ],
        out_specs=[
            pl.BlockSpec((TOKEN_TILE, cw), lambda i: (i, 0)),
            pl.BlockSpec((TOKEN_TILE, cw), lambda i: (i, 0)),
        ],
        out_shape=[
            jax.ShapeDtypeStruct((n, cw), F32),
            jax.ShapeDtypeStruct((n, cw), BF16),
        ],
        scratch_shapes=[pltpu.VMEM((SUBLANES, cw), F32)],
        compiler_params=pltpu.CompilerParams(
            dimension_semantics=("arbitrary",), vmem_limit_bytes=VMEM_LIMIT_BYTES),
        name="inproj",
    )(x2, norm_g, w_in_bf, w_in_bf, w_in_bf, w_in_bf, conv_w, conv_b)


def _ssm_body(u_ref, tz_ref, p_ref, q_ref, pw_ref, d_ref, y_ref, s_scr, xin_scr):
    nseq = u_ref.shape[0]
    m = u_ref.shape[1] // CHUNK
    sl = STATE_LANES
    ngl = GROUPS_PER_LANE_BLOCK

    yts, pieces = [], []
    for s in range(nseq):
        pieces.append([u_ref[s, pl.ds(t, m, stride=CHUNK), :] for t in range(CHUNK)])
        pts = [p.T.astype(BF16) for p in pieces[s]]
        ys, s_re, s_im = [], [], []
        for g in range(ngl):
            xg = jnp.concatenate([pt[g * SSM_GROUP:(g + 1) * SSM_GROUP] for pt in pts], axis=0)
            ys.append(_dot(tz_ref[g], xg))
            sg = _dot(p_ref[g], xg)
            s_re.append(sg[:SSM_STATE])
            s_im.append(sg[SSM_STATE:])
        yts.append(ys)
        s_scr[s] = jnp.concatenate(s_re + s_im, axis=0).T

    a16r = pw_ref[0:1, :]
    a16i = pw_ref[1:2, :]

    def step(k, carry):
        out = []
        for s in range(nseq):
            xr, xi = carry[2 * s], carry[2 * s + 1]
            xin_scr[s, pl.ds(k, 1), :sl] = xr
            xin_scr[s, pl.ds(k, 1), sl:] = xi
            row = s_scr[s, pl.ds(k, 1), :]
            out += [a16r * xr - a16i * xi + row[:, :sl], a16r * xi + a16i * xr + row[:, sl:]]
        return tuple(out)

    zero = jnp.zeros((1, sl), F32)
    lax.fori_loop(0, m, step, (zero,) * (2 * nseq), unroll=8)

    d = d_ref[...]
    for s in range(nseq):
        xint = xin_scr[s].T
        ys = yts[s]
        for g in range(ngl):
            xg = jnp.concatenate([xint[g * SSM_STATE:(g + 1) * SSM_STATE],
                                  xint[sl + g * SSM_STATE:sl + (g + 1) * SSM_STATE]], axis=0)
            ys[g] = ys[g] + _dot(q_ref[g], xg.astype(BF16))
        for t in range(CHUNK):
            yt = jnp.concatenate([y[t * SSM_GROUP:(t + 1) * SSM_GROUP] for y in ys], axis=0)
            y_ref[s, pl.ds(t, m, stride=CHUNK), :] = jax.nn.gelu(yt.T + d * pieces[s][t])


def _ssm(u, tz, pm, qm, pw, dsk, batch, seq_len, seqs_per_step=2):
    n, width = u.shape
    nblk = width // LANES
    m = seq_len // CHUNK
    ngl = GROUPS_PER_LANE_BLOCK
    u3 = u.reshape(batch, seq_len, width)
    y3 = pl.pallas_call(
        _ssm_body,
        grid=(nblk, batch // seqs_per_step),
        in_specs=[
            pl.BlockSpec((seqs_per_step, seq_len, LANES), lambda c, b: (b, 0, c)),
            pl.BlockSpec((ngl, MXU_DIM, MXU_DIM), lambda c, b: (c, 0, 0)),
            pl.BlockSpec((ngl, 2 * SSM_STATE, MXU_DIM), lambda c, b: (c, 0, 0)),
            pl.BlockSpec((ngl, MXU_DIM, 2 * SSM_STATE), lambda c, b: (c, 0, 0)),
            pl.BlockSpec((None, 2, STATE_LANES), lambda c, b: (c, 0, 0)),
            pl.BlockSpec((None, 1, LANES), lambda c, b: (c, 0, 0)),
        ],
        out_specs=pl.BlockSpec((seqs_per_step, seq_len, LANES), lambda c, b: (b, 0, c)),
        out_shape=jax.ShapeDtypeStruct((batch, seq_len, width), F32),
        scratch_shapes=[pltpu.VMEM((seqs_per_step, m, 2 * STATE_LANES), F32),
                        pltpu.VMEM((seqs_per_step, m, 2 * STATE_LANES), F32)],
        compiler_params=pltpu.CompilerParams(
            dimension_semantics=("arbitrary", "arbitrary"),
            vmem_limit_bytes=VMEM_LIMIT_BYTES),
        name="ssm",
    )(u3, tz, pm, qm, pw, dsk)
    return y3.reshape(n, width)


def _ssm_operators(a_re, a_im, log_dt, b_re, b_im, c_re, c_im, d_skip):
    hp = lax.Precision.HIGHEST
    g = a_re.shape[0]
    nblk = g // GROUPS_PER_LANE_BLOCK
    dt = jnp.exp(log_dt)[:, None]
    mag = jnp.exp(dt * a_re)
    abar_re = mag * jnp.cos(dt * a_im)
    abar_im = mag * jnp.sin(dt * a_im)
    nr = abar_re - 1.0
    ni = abar_im
    den = a_re * a_re + a_im * a_im
    fr = (nr * a_re + ni * a_im) / den
    fi = (ni * a_re - nr * a_im) / den
    bb_re = fr[..., None] * b_re - fi[..., None] * b_im
    bb_im = fr[..., None] * b_im + fi[..., None] * b_re
    pr, pi = [jnp.ones_like(abar_re)], [jnp.zeros_like(abar_re)]
    for _ in range(CHUNK):
        pr, pi = (pr + [pr[-1] * abar_re - pi[-1] * abar_im],
                  pi + [pr[-1] * abar_im + pi[-1] * abar_re])
    pw_re = jnp.stack(pr, axis=-1)
    pw_im = jnp.stack(pi, axis=-1)

    col = jnp.arange(MXU_DIM)
    pick = (CHUNK - 1 - col[None, :] // SSM_GROUP == jnp.arange(CHUNK)[:, None]).astype(F32)
    e_re = jnp.einsum("gpn,nc->gpc", pw_re[..., :CHUNK], pick, precision=hp)
    e_im = jnp.einsum("gpn,nc->gpc", pw_im[..., :CHUNK], pick, precision=hp)
    spread = (col[None, :] % SSM_GROUP == jnp.arange(SSM_GROUP)[:, None]).astype(F32)
    bx_re = jnp.einsum("gpi,ic->gpc", bb_re, spread, precision=hp)
    bx_im = jnp.einsum("gpi,ic->gpc", bb_im, spread, precision=hp)
    p_re = e_re * bx_re - e_im * bx_im
    p_im = e_re * bx_im + e_im * bx_re
    pm = jnp.concatenate([p_re, p_im], axis=1).astype(BF16)

    krev = (jnp.einsum("gop,gpc->goc", c_re, p_re, precision=hp)
            - jnp.einsum("gop,gpc->goc", c_im, p_im, precision=hp))
    padded = jnp.concatenate([krev, jnp.zeros_like(krev)], axis=-1)
    tz = jnp.stack([padded[..., (CHUNK - 1 - t) * SSM_GROUP:(CHUNK - 1 - t) * SSM_GROUP + MXU_DIM]
                    for t in range(CHUNK)], axis=1)
    tz = tz.reshape(g, MXU_DIM, MXU_DIM).astype(BF16)

    o_re = pw_re[..., 1:].transpose(0, 2, 1)[:, :, None, :]
    o_im = pw_im[..., 1:].transpose(0, 2, 1)[:, :, None, :]
    cr = c_re[:, None]
    ci = c_im[:, None]
    qm = jnp.concatenate([cr * o_re - ci * o_im, -(cr * o_im + ci * o_re)], axis=-1)
    qm = qm.reshape(g, MXU_DIM, 2 * SSM_STATE).astype(BF16)

    pw = jnp.stack([pw_re[..., CHUNK], pw_im[..., CHUNK]])
    pw = pw.reshape(2, nblk, STATE_LANES).transpose(1, 0, 2)
    dsk = d_skip.reshape(nblk, 1, LANES)
    return tz, pm, qm, pw, dsk


def _mix_body(x_ref, g_ref, ya_ref, zb_ref, wglu_ref, wma_ref, wmb_ref, wssm_ref,
              wconv_ref, wo_ref, h_ref, xn_scr, ya2_scr):
    j = pl.program_id(1)

    @pl.when(j == 0)
    def _():
        x = x_ref[...]
        xn_scr[...] = _rmsnorm(x, g_ref[...]).astype(BF16)
        ya = ya_ref[...]
        ya2_scr[...] = (ya * jax.nn.sigmoid(_dot(ya.astype(BF16), wglu_ref[...]))).astype(BF16)
        h_ref[...] = x

    xn = xn_scr[...]
    pa = _dot(ya2_scr[...], wssm_ref[...])
    pb = _dot(zb_ref[...], wconv_ref[...])
    merged = (jax.nn.sigmoid(_dot(xn, wma_ref[...])) * pa
              + jax.nn.sigmoid(_dot(xn, wmb_ref[...])) * pb)
    h_ref[...] += _dot(merged.astype(BF16), wo_ref[...])


def _mix(x2, norm_g, ya, zb, w_glu_bf, w_in_bf, w_ssm_bf, w_conv_bf, w_o_bf, col_tile=512):
    n, d = x2.shape
    sw = ya.shape[1]
    cw = zb.shape[1]
    ncol = d // col_tile
    merge_a0 = (w_in_bf.shape[1] - 2 * d) // col_tile
    merge_b0 = merge_a0 + ncol
    const = lambda i, j: (0, 0)
    return pl.pallas_call(
        _mix_body,
        grid=(n // TOKEN_TILE, ncol),
        in_specs=[
            pl.BlockSpec((TOKEN_TILE, d), lambda i, j: (i, 0)),
            pl.BlockSpec((1, d), const),
            pl.BlockSpec((TOKEN_TILE, sw), lambda i, j: (i, 0)),
            pl.BlockSpec((TOKEN_TILE, cw), lambda i, j: (i, 0)),
            pl.BlockSpec((sw, sw), const),
            pl.BlockSpec((d, col_tile), lambda i, j: (0, merge_a0 + j)),
            pl.BlockSpec((d, col_tile), lambda i, j: (0, merge_b0 + j)),
            pl.BlockSpec((sw, col_tile), lambda i, j: (0, j)),
            pl.BlockSpec((cw, col_tile), lambda i, j: (0, j)),
            pl.BlockSpec((col_tile, d), lambda i, j: (j, 0)),
        ],
        out_specs=pl.BlockSpec((TOKEN_TILE, d), lambda i, j: (i, 0)),
        out_shape=jax.ShapeDtypeStruct((n, d), F32),
        scratch_shapes=[pltpu.VMEM((TOKEN_TILE, d), BF16),
                        pltpu.VMEM((TOKEN_TILE, sw), BF16)],
        compiler_params=pltpu.CompilerParams(
            dimension_semantics=("arbitrary", "arbitrary"),
            vmem_limit_bytes=VMEM_LIMIT_BYTES),
        name="mix",
    )(x2, norm_g, ya, zb, w_glu_bf, w_in_bf, w_in_bf, w_ssm_bf, w_conv_bf, w_o_bf)


def _ffn_body(tiles_per_seq, final_norm, h_ref, g_ref, wa_ref, wg_ref, cw_ref, cb_ref,
              wd_ref, gfin_ref, o_ref, hn_scr, prev_scr):
    i = pl.program_id(0)
    f = pl.program_id(1)

    @pl.when(f == 0)
    def _():
        h = h_ref[...]
        hn_scr[...] = _rmsnorm(h, g_ref[...]).astype(BF16)
        o_ref[...] = h

    @pl.when(i % tiles_per_seq == 0)
    def _():
        prev_scr[f] = jnp.zeros(prev_scr.shape[1:], F32)

    hn = hn_scr[...]
    a = _dot(hn, wa_ref[...])
    conv = _causal_conv3(a, prev_scr[f], cw_ref, cb_ref)
    prev_scr[f] = a[a.shape[0] - SUBLANES:]
    act = (jax.nn.gelu(conv) * _dot(hn, wg_ref[...])).astype(BF16)
    o_ref[...] += _dot(act, wd_ref[...])

    if final_norm:
        @pl.when(f == pl.num_programs(1) - 1)
        def _():
            o_ref[...] = _rmsnorm(o_ref[...], gfin_ref[...])


def _ffn(h1, norm_g, w_up_bf, ffn_conv_w, ffn_conv_b, w_down_bf, norm_final, final_norm,
         seq_len, ff_tile=512):
    n, d = h1.shape
    d_ff = w_down_bf.shape[0]
    nff = d_ff // ff_tile
    tiles_per_seq = seq_len // TOKEN_TILE
    const = lambda i, f: (0, 0)
    return pl.pallas_call(
        functools.partial(_ffn_body, tiles_per_seq, final_norm),
        grid=(n // TOKEN_TILE, nff),
        in_specs=[
            pl.BlockSpec((TOKEN_TILE, d), lambda i, f: (i, 0)),
            pl.BlockSpec((1, d), const),
            pl.BlockSpec((d, ff_tile), lambda i, f: (0, f)),
            pl.BlockSpec((d, ff_tile), lambda i, f: (0, nff + f)),
            pl.BlockSpec((CONV_K, ff_tile), lambda i, f: (0, f)),
            pl.BlockSpec((1, ff_tile), lambda i, f: (0, f)),
            pl.BlockSpec((ff_tile, d), lambda i, f: (f, 0)),
            pl.BlockSpec((1, d), const),
        ],
        out_specs=pl.BlockSpec((TOKEN_TILE, d), lambda i, f: (i, 0)),
        out_shape=jax.ShapeDtypeStruct((n, d), F32),
        scratch_shapes=[pltpu.VMEM((TOKEN_TILE, d), BF16),
                        pltpu.VMEM((nff, SUBLANES, ff_tile), F32)],
        compiler_params=pltpu.CompilerParams(
            dimension_semantics=("arbitrary", "arbitrary"),
            vmem_limit_bytes=VMEM_LIMIT_BYTES),
        name="ffn",
    )(h1, norm_g, w_up_bf, w_up_bf, ffn_conv_w, ffn_conv_b, w_down_bf, norm_final)


def kernel(x, norm_tok, w_in, a_re, a_im, log_dt, b_re, b_im, c_re, c_im, d_skip, w_glu, w_ssm_out, conv_w, conv_b, w_conv_out, w_o, norm_ffn, w_up, ffn_conv_w, ffn_conv_b, w_down, norm_final):
    batch, seq_len, d = x.shape
    depth = w_in.shape[0]
    h = x.reshape(batch * seq_len, d)
    for l in range(depth):
        w_in_bf = w_in[l].astype(BF16)
        u, zb = _inproj(h, norm_tok[l][None], w_in_bf, conv_w[l], conv_b[l][None], seq_len)
        ops = _ssm_operators(a_re[l], a_im[l], log_dt[l], b_re[l], b_im[l],
                             c_re[l], c_im[l], d_skip[l])
        ya = _ssm(u, *ops, batch, seq_len)
        h1 = _mix(h, norm_tok[l][None], ya, zb, w_glu[l].astype(BF16), w_in_bf,
                  w_ssm_out[l].astype(BF16), w_conv_out[l].astype(BF16), w_o[l].astype(BF16))
        h = _ffn(h1, norm_ffn[l][None], w_up[l].astype(BF16), ffn_conv_w[l],
                 ffn_conv_b[l][None], w_down[l].astype(BF16), norm_final[None],
                 l == depth - 1, seq_len)
    return h.reshape(batch, seq_len, d)
```

```python
import functools

import jax
import jax.numpy as jnp
from jax import lax
from jax.experimental import pallas as pl
from jax.experimental.pallas import tpu as pltpu

SSM_GROUP = 16
SSM_STATE = 64
CONV_K = 3
EPS = 1e-6

LANES = 128
SUBLANES = 8
MXU_DIM = 256
VMEM_LIMIT_BYTES = 56 * 1024 * 1024

TOKEN_TILE = 512
FFN_TOKEN_TILE = 1024
CHUNK = 16
GROUPS_PER_LANE_BLOCK = LANES // SSM_GROUP
STATE_LANES = GROUPS_PER_LANE_BLOCK * SSM_STATE

F32 = jnp.float32
BF16 = jnp.bfloat16


def _rmsnorm(x, gain):
    return x * lax.rsqrt(jnp.mean(x * x, axis=-1, keepdims=True) + EPS) * gain


def _dot(a, b):
    return jnp.dot(a, b, preferred_element_type=F32)


def _causal_conv3(a, prev, w_ref, b_ref):
    w0 = w_ref[0:1, :]
    w1 = w_ref[1:2, :]
    w2 = w_ref[2:3, :]
    bias = b_ref[...]
    full = w2 * a + w1 * pltpu.roll(a, 1, 0) + w0 * pltpu.roll(a, 2, 0) + bias
    top = a[0:SUBLANES]
    rows = lax.broadcasted_iota(jnp.int32, top.shape, 0)
    s1 = jnp.where(rows < 1, pltpu.roll(prev, 1, 0), pltpu.roll(top, 1, 0))
    s2 = jnp.where(rows < 2, pltpu.roll(prev, 2, 0), pltpu.roll(top, 2, 0))
    fix = w2 * top + w1 * s1 + w0 * s2 + bias
    return jnp.concatenate([fix, full[SUBLANES:]], axis=0)


def _inproj_body(tiles_per_seq, x_ref, g_ref, wu_ref, wv_ref, wgb_ref, wgc_ref,
                 cw_ref, cb_ref, u_ref, zb_ref, prev_ref):
    i = pl.program_id(0)

    @pl.when(i % tiles_per_seq == 0)
    def _():
        prev_ref[...] = jnp.zeros_like(prev_ref)

    xn = _rmsnorm(x_ref[...], g_ref[...]).astype(BF16)
    u_ref[...] = _dot(xn, wu_ref[...])
    cin = _dot(xn, wgc_ref[...]) * _dot(xn, wv_ref[...])
    conv = _causal_conv3(cin, prev_ref[...], cw_ref, cb_ref)
    prev_ref[...] = cin[cin.shape[0] - SUBLANES:]
    zb_ref[...] = (_dot(xn, wgb_ref[...]) * conv).astype(BF16)


def _inproj(x2, norm_g, w_in_bf, conv_w, conv_b, seq_len):
    n, d = x2.shape
    cw = conv_w.shape[1]
    tiles_per_seq = seq_len // TOKEN_TILE
    const = lambda i: (0, 0)
    wspec = lambda col: pl.BlockSpec((d, cw), lambda i, col=col: (0, col),
                                     pipeline_mode=pl.Buffered(1))
    return pl.pallas_call(
        functools.partial(_inproj_body, tiles_per_seq),
        grid=(n // TOKEN_TILE,),
        in_specs=[
            pl.BlockSpec((TOKEN_TILE, d), lambda i: (i, 0)),
            pl.BlockSpec((1, d), const),
            wspec(0), wspec(1), wspec(2), wspec(3),
            pl.BlockSpec((CONV_K, cw), const),
            pl.BlockSpec((1, cw), const),
        ],
        out_specs=[
            pl.BlockSpec((TOKEN_TILE, cw), lambda i: (i, 0)),
            pl.BlockSpec((TOKEN_TILE, cw), lambda i: (i, 0)),
        ],
        out_shape=[
            jax.ShapeDtypeStruct((n, cw), F32),
            jax.ShapeDtypeStruct((n, cw), BF16),
        ],
        scratch_shapes=[pltpu.VMEM((SUBLANES, cw), F32)],
        compiler_params=pltpu.CompilerParams(
            dimension_semantics=("arbitrary",), vmem_limit_bytes=VMEM_LIMIT_BYTES),
        name="inproj",
    )(x2, norm_g, w_in_bf, w_in_bf, w_in_bf, w_in_bf, conv_w, conv_b)


def _ssm_body(u_ref, tz_ref, p_ref, q_ref, pw_ref, d_ref, y_ref, s_scr, xin_scr):
    nseq = u_ref.shape[0]
    m = u_ref.shape[1] // CHUNK
    sl = STATE_LANES
    ngl = GROUPS_PER_LANE_BLOCK

    yts, pieces = [], []
    for s in range(nseq):
        pieces.append([u_ref[s, pl.ds(t, m, stride=CHUNK), :] for t in range(CHUNK)])
        pts = [p.T.astype(BF16) for p in pieces[s]]
        ys, s_re, s_im = [], [], []
        for g in range(ngl):
            xg = jnp.concatenate([pt[g * SSM_GROUP:(g + 1) * SSM_GROUP] for pt in pts], axis=0)
            ys.append(_dot(tz_ref[g], xg))
            sg = _dot(p_ref[g], xg)
            s_re.append(sg[:SSM_STATE])
            s_im.append(sg[SSM_STATE:])
        yts.append(ys)
        s_scr[s] = jnp.concatenate(s_re + s_im, axis=0).T

    a16r = pw_ref[0:1, :]
    a16i = pw_ref[1:2, :]

    def step(k, carry):
        out = []
        for s in range(nseq):
            xr, xi = carry[2 * s], carry[2 * s + 1]
            xin_scr[s, pl.ds(k, 1), :sl] = xr
            xin_scr[s, pl.ds(k, 1), sl:] = xi
            row = s_scr[s, pl.ds(k, 1), :]
            out += [a16r * xr - a16i * xi + row[:, :sl], a16r * xi + a16i * xr + row[:, sl:]]
        return tuple(out)

    zero = jnp.zeros((1, sl), F32)
    lax.fori_loop(0, m, step, (zero,) * (2 * nseq), unroll=8)

    d = d_ref[...]
    for s in range(nseq):
        xint = xin_scr[s].T
        ys = yts[s]
        for g in range(ngl):
            xg = jnp.concatenate([xint[g * SSM_STATE:(g + 1) * SSM_STATE],
                                  xint[sl + g * SSM_STATE:sl + (g + 1) * SSM_STATE]], axis=0)
            ys[g] = ys[g] + _dot(q_ref[g], xg.astype(BF16))
        for t in range(CHUNK):
            yt = jnp.concatenate([y[t * SSM_GROUP:(t + 1) * SSM_GROUP] for y in ys], axis=0)
            y_ref[s, pl.ds(t, m, stride=CHUNK), :] = jax.nn.gelu(yt.T + d * pieces[s][t])


def _ssm(u, tz, pm, qm, pw, dsk, batch, seq_len, seqs_per_step=2):
    n, width = u.shape
    nblk = width // LANES
    m = seq_len // CHUNK
    ngl = GROUPS_PER_LANE_BLOCK
    u3 = u.reshape(batch, seq_len, width)
    y3 = pl.pallas_call(
        _ssm_body,
        grid=(nblk, batch // seqs_per_step),
        in_specs=[
            pl.BlockSpec((seqs_per_step, seq_len, LANES), lambda c, b: (b, 0, c)),
            pl.BlockSpec((ngl, MXU_DIM, MXU_DIM), lambda c, b: (c, 0, 0)),
            pl.BlockSpec((ngl, 2 * SSM_STATE, MXU_DIM), lambda c, b: (c, 0, 0)),
            pl.BlockSpec((ngl, MXU_DIM, 2 * SSM_STATE), lambda c, b: (c, 0, 0)),
            pl.BlockSpec((None, 2, STATE_LANES), lambda c, b: (c, 0, 0)),
            pl.BlockSpec((None, 1, LANES), lambda c, b: (c, 0, 0)),
        ],
        out_specs=pl.BlockSpec((seqs_per_step, seq_len, LANES), lambda c, b: (b, 0, c)),
        out_shape=jax.ShapeDtypeStruct((batch, seq_len, width), F32),
        scratch_shapes=[pltpu.VMEM((seqs_per_step, m, 2 * STATE_LANES), F32),
                        pltpu.VMEM((seqs_per_step, m, 2 * STATE_LANES), F32)],
        compiler_params=pltpu.CompilerParams(
            dimension_semantics=("arbitrary", "arbitrary"),
            vmem_limit_bytes=VMEM_LIMIT_BYTES),
        name="ssm",
    )(u3, tz, pm, qm, pw, dsk)
    return y3.reshape(n, width)


def _ssm_operators(a_re, a_im, log_dt, b_re, b_im, c_re, c_im, d_skip):
    hp = lax.Precision.HIGHEST
    g = a_re.shape[0]
    nblk = g // GROUPS_PER_LANE_BLOCK
    dt = jnp.exp(log_dt)[:, None]
    mag = jnp.exp(dt * a_re)
    abar_re = mag * jnp.cos(dt * a_im)
    abar_im = mag * jnp.sin(dt * a_im)
    nr = abar_re - 1.0
    ni = abar_im
    den = a_re * a_re + a_im * a_im
    fr = (nr * a_re + ni * a_im) / den
    fi = (ni * a_re - nr * a_im) / den
    bb_re = fr[..., None] * b_re - fi[..., None] * b_im
    bb_im = fr[..., None] * b_im + fi[..., None] * b_re
    pr, pi = [jnp.ones_like(abar_re)], [jnp.zeros_like(abar_re)]
    for _ in range(CHUNK):
        pr, pi = (pr + [pr[-1] * abar_re - pi[-1] * abar_im],
                  pi + [pr[-1] * abar_im + pi[-1] * abar_re])
    pw_re = jnp.stack(pr, axis=-1)
    pw_im = jnp.stack(pi, axis=-1)

    col = jnp.arange(MXU_DIM)
    pick = (CHUNK - 1 - col[None, :] // SSM_GROUP == jnp.arange(CHUNK)[:, None]).astype(F32)
    e_re = jnp.einsum("gpn,nc->gpc", pw_re[..., :CHUNK], pick, precision=hp)
    e_im = jnp.einsum("gpn,nc->gpc", pw_im[..., :CHUNK], pick, precision=hp)
    spread = (col[None, :] % SSM_GROUP == jnp.arange(SSM_GROUP)[:, None]).astype(F32)
    bx_re = jnp.einsum("gpi,ic->gpc", bb_re, spread, precision=hp)
    bx_im = jnp.einsum("gpi,ic->gpc", bb_im, spread, precision=hp)
    p_re = e_re * bx_re - e_im * bx_im
    p_im = e_re * bx_im + e_im * bx_re
    pm = jnp.concatenate([p_re, p_im], axis=1).astype(BF16)

    krev = (jnp.einsum("gop,gpc->goc", c_re, p_re, precision=hp)
            - jnp.einsum("gop,gpc->goc", c_im, p_im, precision=hp))
    padded = jnp.concatenate([krev, jnp.zeros_like(krev)], axis=-1)
    tz = jnp.stack([padded[..., (CHUNK - 1 - t) * SSM_GROUP:(CHUNK - 1 - t) * SSM_GROUP + MXU_DIM]
                    for t in range(CHUNK)], axis=1)
    tz = tz.reshape(g, MXU_DIM, MXU_DIM).astype(BF16)

    o_re = pw_re[..., 1:].transpose(0, 2, 1)[:, :, None, :]
    o_im = pw_im[..., 1:].transpose(0, 2, 1)[:, :, None, :]
    cr = c_re[:, None]
    ci = c_im[:, None]
    qm = jnp.concatenate([cr * o_re - ci * o_im, -(cr * o_im + ci * o_re)], axis=-1)
    qm = qm.reshape(g, MXU_DIM, 2 * SSM_STATE).astype(BF16)

    pw = jnp.stack([pw_re[..., CHUNK], pw_im[..., CHUNK]])
    pw = pw.reshape(2, nblk, STATE_LANES).transpose(1, 0, 2)
    dsk = d_skip.reshape(nblk, 1, LANES)
    return tz, pm, qm, pw, dsk


def _mix_body(x_ref, g_ref, ya_ref, zb_ref, wglu_ref, wma_ref, wmb_ref, wssm_ref,
              wconv_ref, wo_ref, h_ref, xn_scr, ya2_scr):
    j = pl.program_id(1)

    @pl.when(j == 0)
    def _():
        x = x_ref[...]
        xn_scr[...] = _rmsnorm(x, g_ref[...]).astype(BF16)
        ya = ya_ref[...]
        ya2_scr[...] = (ya * jax.nn.sigmoid(_dot(ya.astype(BF16), wglu_ref[...]))).astype(BF16)
        h_ref[...] = x

    xn = xn_scr[...]
    pa = _dot(ya2_scr[...], wssm_ref[...])
    pb = _dot(zb_ref[...], wconv_ref[...])
    merged = (jax.nn.sigmoid(_dot(xn, wma_ref[...])) * pa
              + jax.nn.sigmoid(_dot(xn, wmb_ref[...])) * pb)
    h_ref[...] += _dot(merged.astype(BF16), wo_ref[...])


def _mix(x2, norm_g, ya, zb, w_glu_bf, w_in_bf, w_ssm_bf, w_conv_bf, w_o_bf, col_tile=512):
    n, d = x2.shape
    sw = ya.shape[1]
    cw = zb.shape[1]
    ncol = d // col_tile
    merge_a0 = (w_in_bf.shape[1] - 2 * d) // col_tile
    merge_b0 = merge_a0 + ncol
    const = lambda i, j: (0, 0)
    return pl.pallas_call(
        _mix_body,
        grid=(n // TOKEN_TILE, ncol),
        in_specs=[
            pl.BlockSpec((TOKEN_TILE, d), lambda i, j: (i, 0)),
            pl.BlockSpec((1, d), const),
            pl.BlockSpec((TOKEN_TILE, sw), lambda i, j: (i, 0)),
            pl.BlockSpec((TOKEN_TILE, cw), lambda i, j: (i, 0)),
            pl.BlockSpec((sw, sw), const),
            pl.BlockSpec((d, col_tile), lambda i, j: (0, merge_a0 + j)),
            pl.BlockSpec((d, col_tile), lambda i, j: (0, merge_b0 + j)),
            pl.BlockSpec((sw, col_tile), lambda i, j: (0, j)),
            pl.BlockSpec((cw, col_tile), lambda i, j: (0, j)),
            pl.BlockSpec((col_tile, d), lambda i, j: (j, 0)),
        ],
        out_specs=pl.BlockSpec((TOKEN_TILE, d), lambda i, j: (i, 0)),
        out_shape=jax.ShapeDtypeStruct((n, d), F32),
        scratch_shapes=[pltpu.VMEM((TOKEN_TILE, d), BF16),
                        pltpu.VMEM((TOKEN_TILE, sw), BF16)],
        compiler_params=pltpu.CompilerParams(
            dimension_semantics=("arbitrary", "arbitrary"),
            vmem_limit_bytes=VMEM_LIMIT_BYTES),
        name="mix",
    )(x2, norm_g, ya, zb, w_glu_bf, w_in_bf, w_in_bf, w_ssm_bf, w_conv_bf, w_o_bf)


def _ffn_body(tiles_per_seq, final_norm, h_hbm, g_ref, wa_ref, wg_ref, cw_ref, cb_ref,
              wd_ref, gfin_ref, o_ref, hbuf, hn_scr, prev_scr, sem):
    i = pl.program_id(0)
    f = pl.program_id(1)
    tm = o_ref.shape[0]

    def h_copy(tile):
        rows = pl.ds(pl.multiple_of(tile * tm, tm), tm)
        return pltpu.make_async_copy(h_hbm.at[rows], hbuf, sem)

    @pl.when(f == 0)
    def _():
        @pl.when(i == 0)
        def _():
            h_copy(0).start()

        h_copy(i).wait()
        h = hbuf[...]
        hn_scr[...] = _rmsnorm(h, g_ref[...]).astype(BF16)
        o_ref[...] = h

        @pl.when(i + 1 < pl.num_programs(0))
        def _():
            h_copy(i + 1).start()

    @pl.when(i % tiles_per_seq == 0)
    def _():
        prev_scr[f] = jnp.zeros(prev_scr.shape[1:], F32)

    hn = hn_scr[...]
    a = _dot(hn, wa_ref[...])
    conv = _causal_conv3(a, prev_scr[f], cw_ref, cb_ref)
    prev_scr[f] = a[a.shape[0] - SUBLANES:]
    act = (jax.nn.gelu(conv) * _dot(hn, wg_ref[...])).astype(BF16)
    o_ref[...] += _dot(act, wd_ref[...])

    if final_norm:
        @pl.when(f == pl.num_programs(1) - 1)
        def _():
            o_ref[...] = _rmsnorm(o_ref[...], gfin_ref[...])


def _ffn(h1, norm_g, w_up_bf, ffn_conv_w, ffn_conv_b, w_down_bf, norm_final, final_norm,
         seq_len, ff_tile=512):
    n, d = h1.shape
    d_ff = w_down_bf.shape[0]
    nff = d_ff // ff_tile
    tm = FFN_TOKEN_TILE
    tiles_per_seq = seq_len // tm
    const = lambda i, f: (0, 0)
    return pl.pallas_call(
        functools.partial(_ffn_body, tiles_per_seq, final_norm),
        grid=(n // tm, nff),
        in_specs=[
            pl.BlockSpec(memory_space=pl.ANY),
            pl.BlockSpec((1, d), const),
            pl.BlockSpec((d, ff_tile), lambda i, f: (0, f)),
            pl.BlockSpec((d, ff_tile), lambda i, f: (0, nff + f)),
            pl.BlockSpec((CONV_K, ff_tile), lambda i, f: (0, f)),
            pl.BlockSpec((1, ff_tile), lambda i, f: (0, f)),
            pl.BlockSpec((ff_tile, d), lambda i, f: (f, 0)),
            pl.BlockSpec((1, d), const),
        ],
        out_specs=pl.BlockSpec((tm, d), lambda i, f: (i, 0)),
        out_shape=jax.ShapeDtypeStruct((n, d), F32),
        scratch_shapes=[pltpu.VMEM((tm, d), F32),
                        pltpu.VMEM((tm, d), BF16),
                        pltpu.VMEM((nff, SUBLANES, ff_tile), F32),
                        pltpu.SemaphoreType.DMA(())],
        compiler_params=pltpu.CompilerParams(
            dimension_semantics=("arbitrary", "arbitrary"),
            vmem_limit_bytes=VMEM_LIMIT_BYTES),
        name="ffn",
    )(h1, norm_g, w_up_bf, w_up_bf, ffn_conv_w, ffn_conv_b, w_down_bf, norm_final)


def kernel(x, norm_tok, w_in, a_re, a_im, log_dt, b_re, b_im, c_re, c_im, d_skip, w_glu, w_ssm_out, conv_w, conv_b, w_conv_out, w_o, norm_ffn, w_up, ffn_conv_w, ffn_conv_b, w_down, norm_final):
    batch, seq_len, d = x.shape
    depth = w_in.shape[0]
    h = x.reshape(batch * seq_len, d)
    for l in range(depth):
        w_in_bf = w_in[l].astype(BF16)
        u, zb = _inproj(h, norm_tok[l][None], w_in_bf, conv_w[l], conv_b[l][None], seq_len)
        ops = _ssm_operators(a_re[l], a_im[l], log_dt[l], b_re[l], b_im[l],
                             c_re[l], c_im[l], d_skip[l])
        ya = _ssm(u, *ops, batch, seq_len)
        h1 = _mix(h, norm_tok[l][None], ya, zb, w_glu[l].astype(BF16), w_in_bf,
                  w_ssm_out[l].astype(BF16), w_conv_out[l].astype(BF16), w_o[l].astype(BF16))
        h = _ffn(h1, norm_ffn[l][None], w_up[l].astype(BF16), ffn_conv_w[l],
                 ffn_conv_b[l][None], w_down[l].astype(BF16), norm_final[None],
                 l == depth - 1, seq_len)
    return h.reshape(batch, seq_len, d)
```

```python
import functools

import jax
import jax.numpy as jnp
from jax import lax
from jax.experimental import pallas as pl
from jax.experimental.pallas import tpu as pltpu

SSM_GROUP = 16
SSM_STATE = 64
CONV_K = 3
EPS = 1e-6

LANES = 128
SUBLANES = 8
MXU_DIM = 256
VMEM_LIMIT_BYTES = 56 * 1024 * 1024

TOKEN_TILE = 512
FFN_TOKEN_TILE = 1024
CHUNK = 16
GROUPS_PER_LANE_BLOCK = LANES // SSM_GROUP
STATE_LANES = GROUPS_PER_LANE_BLOCK * SSM_STATE

F32 = jnp.float32
BF16 = jnp.bfloat16


def _rmsnorm(x, gain):
    return x * lax.rsqrt(jnp.mean(x * x, axis=-1, keepdims=True) + EPS) * gain


def _dot(a, b):
    return jnp.dot(a, b, preferred_element_type=F32)


def _causal_conv3(a, prev, w_ref, b_ref):
    w0 = w_ref[0:1, :]
    w1 = w_ref[1:2, :]
    w2 = w_ref[2:3, :]
    bias = b_ref[...]
    full = w2 * a + w1 * pltpu.roll(a, 1, 0) + w0 * pltpu.roll(a, 2, 0) + bias
    top = a[0:SUBLANES]
    rows = lax.broadcasted_iota(jnp.int32, top.shape, 0)
    s1 = jnp.where(rows < 1, pltpu.roll(prev, 1, 0), pltpu.roll(top, 1, 0))
    s2 = jnp.where(rows < 2, pltpu.roll(prev, 2, 0), pltpu.roll(top, 2, 0))
    fix = w2 * top + w1 * s1 + w0 * s2 + bias
    return jnp.concatenate([fix, full[SUBLANES:]], axis=0)


def _cast_riders(src_refs, dst_refs):
    for src, dst in zip(src_refs, dst_refs):
        dst[...] = src[...].astype(BF16)


def _inproj_body(tiles_per_seq, n_cast, x_ref, g_ref, wu_ref, wv_ref, wgb_ref, wgc_ref,
                 cw_ref, cb_ref, *rest):
    cast_src = rest[:n_cast]
    u_ref, zb_ref = rest[n_cast:n_cast + 2]
    cast_dst = rest[n_cast + 2:2 * n_cast + 2]
    prev_ref = rest[2 * n_cast + 2]
    i = pl.program_id(0)

    @pl.when(i % tiles_per_seq == 0)
    def _():
        prev_ref[...] = jnp.zeros_like(prev_ref)

    xn = _rmsnorm(x_ref[...], g_ref[...]).astype(BF16)
    u_ref[...] = _dot(xn, wu_ref[...])
    cin = _dot(xn, wgc_ref[...]) * _dot(xn, wv_ref[...])
    conv = _causal_conv3(cin, prev_ref[...], cw_ref, cb_ref)
    prev_ref[...] = cin[cin.shape[0] - SUBLANES:]
    zb_ref[...] = (_dot(xn, wgb_ref[...]) * conv).astype(BF16)
    _cast_riders(cast_src, cast_dst)


def _inproj(x2, norm_g, w_in_bf, conv_w, conv_b, seq_len, riders):
    n, d = x2.shape
    cw = conv_w.shape[1]
    steps = n // TOKEN_TILE
    tiles_per_seq = seq_len // TOKEN_TILE
    const = lambda i: (0, 0)
    wspec = lambda col: pl.BlockSpec((d, cw), lambda i, col=col: (0, col),
                                     pipeline_mode=pl.Buffered(1))
    cast_in, cast_out, cast_shape = [], [], []
    for w, cols in riders:
        rows = w.shape[0] // steps
        width = w.shape[1] if cols is None else cols[1]
        col = 0 if cols is None else cols[0]
        cast_in.append(pl.BlockSpec((rows, width), lambda i, col=col: (i, col)))
        cast_out.append(pl.BlockSpec((rows, width), lambda i: (i, 0)))
        cast_shape.append(jax.ShapeDtypeStruct((w.shape[0], width), BF16))
    outs = pl.pallas_call(
        functools.partial(_inproj_body, tiles_per_seq, len(riders)),
        grid=(steps,),
        in_specs=[
            pl.BlockSpec((TOKEN_TILE, d), lambda i: (i, 0)),
            pl.BlockSpec((1, d), const),
            wspec(0), wspec(1), wspec(2), wspec(3),
            pl.BlockSpec((CONV_K, cw), const),
            pl.BlockSpec((1, cw), const),
        ] + cast_in,
        out_specs=[
            pl.BlockSpec((TOKEN_TILE, cw), lambda i: (i, 0)),
            pl.BlockSpec((TOKEN_TILE, cw), lambda i: (i, 0)),
        ] + cast_out,
        out_shape=[
            jax.ShapeDtypeStruct((n, cw), F32),
            jax.ShapeDtypeStruct((n, cw), BF16),
        ] + cast_shape,
        scratch_shapes=[pltpu.VMEM((SUBLANES, cw), F32)],
        compiler_params=pltpu.CompilerParams(
            dimension_semantics=("arbitrary",), vmem_limit_bytes=VMEM_LIMIT_BYTES),
        name="inproj",
    )(x2, norm_g, w_in_bf, w_in_bf, w_in_bf, w_in_bf, conv_w, conv_b, *[w for w, _ in riders])
    return outs[0], outs[1], outs[2:]


def _ssm_body(u_ref, tz_ref, p_ref, q_ref, pw_ref, d_ref, y_ref, s_scr, xin_scr):
    nseq = u_ref.shape[0]
    m = u_ref.shape[1] // CHUNK
    sl = STATE_LANES
    ngl = GROUPS_PER_LANE_BLOCK

    yts, pieces = [], []
    for s in range(nseq):
        pieces.append([u_ref[s, pl.ds(t, m, stride=CHUNK), :] for t in range(CHUNK)])
        pts = [p.T.astype(BF16) for p in pieces[s]]
        ys, s_re, s_im = [], [], []
        for g in range(ngl):
            xg = jnp.concatenate([pt[g * SSM_GROUP:(g + 1) * SSM_GROUP] for pt in pts], axis=0)
            ys.append(_dot(tz_ref[g], xg))
            sg = _dot(p_ref[g], xg)
            s_re.append(sg[:SSM_STATE])
            s_im.append(sg[SSM_STATE:])
        yts.append(ys)
        s_scr[s] = jnp.concatenate(s_re + s_im, axis=0).T

    a16r = pw_ref[0:1, :]
    a16i = pw_ref[1:2, :]

    def step(k, carry):
        out = []
        for s in range(nseq):
            xr, xi = carry[2 * s], carry[2 * s + 1]
            xin_scr[s, pl.ds(k, 1), :sl] = xr
            xin_scr[s, pl.ds(k, 1), sl:] = xi
            row = s_scr[s, pl.ds(k, 1), :]
            out += [a16r * xr - a16i * xi + row[:, :sl], a16r * xi + a16i * xr + row[:, sl:]]
        return tuple(out)

    zero = jnp.zeros((1, sl), F32)
    lax.fori_loop(0, m, step, (zero,) * (2 * nseq), unroll=8)

    d = d_ref[...]
    for s in range(nseq):
        xint = xin_scr[s].T
        ys = yts[s]
        for g in range(ngl):
            xg = jnp.concatenate([xint[g * SSM_STATE:(g + 1) * SSM_STATE],
                                  xint[sl + g * SSM_STATE:sl + (g + 1) * SSM_STATE]], axis=0)
            ys[g] = ys[g] + _dot(q_ref[g], xg.astype(BF16))
        for t in range(CHUNK):
            yt = jnp.concatenate([y[t * SSM_GROUP:(t + 1) * SSM_GROUP] for y in ys], axis=0)
            y_ref[s, pl.ds(t, m, stride=CHUNK), :] = jax.nn.gelu(yt.T + d * pieces[s][t])


def _ssm(u, tz, pm, qm, pw, dsk, batch, seq_len, seqs_per_step=2):
    n, width = u.shape
    nblk = width // LANES
    m = seq_len // CHUNK
    ngl = GROUPS_PER_LANE_BLOCK
    u3 = u.reshape(batch, seq_len, width)
    y3 = pl.pallas_call(
        _ssm_body,
        grid=(nblk, batch // seqs_per_step),
        in_specs=[
            pl.BlockSpec((seqs_per_step, seq_len, LANES), lambda c, b: (b, 0, c)),
            pl.BlockSpec((ngl, MXU_DIM, MXU_DIM), lambda c, b: (c, 0, 0)),
            pl.BlockSpec((ngl, 2 * SSM_STATE, MXU_DIM), lambda c, b: (c, 0, 0)),
            pl.BlockSpec((ngl, MXU_DIM, 2 * SSM_STATE), lambda c, b: (c, 0, 0)),
            pl.BlockSpec((None, 2, STATE_LANES), lambda c, b: (c, 0, 0)),
            pl.BlockSpec((None, 1, LANES), lambda c, b: (c, 0, 0)),
        ],
        out_specs=pl.BlockSpec((seqs_per_step, seq_len, LANES), lambda c, b: (b, 0, c)),
        out_shape=jax.ShapeDtypeStruct((batch, seq_len, width), F32),
        scratch_shapes=[pltpu.VMEM((seqs_per_step, m, 2 * STATE_LANES), F32),
                        pltpu.VMEM((seqs_per_step, m, 2 * STATE_LANES), F32)],
        compiler_params=pltpu.CompilerParams(
            dimension_semantics=("arbitrary", "arbitrary"),
            vmem_limit_bytes=VMEM_LIMIT_BYTES),
        name="ssm",
    )(u3, tz, pm, qm, pw, dsk)
    return y3.reshape(n, width)


def _ssm_operators(a_re, a_im, log_dt, b_re, b_im, c_re, c_im, d_skip):
    hp = lax.Precision.HIGHEST
    g = a_re.shape[0]
    nblk = g // GROUPS_PER_LANE_BLOCK
    dt = jnp.exp(log_dt)[:, None]
    mag = jnp.exp(dt * a_re)
    abar_re = mag * jnp.cos(dt * a_im)
    abar_im = mag * jnp.sin(dt * a_im)
    nr = abar_re - 1.0
    ni = abar_im
    den = a_re * a_re + a_im * a_im
    fr = (nr * a_re + ni * a_im) / den
    fi = (ni * a_re - nr * a_im) / den
    bb_re = fr[..., None] * b_re - fi[..., None] * b_im
    bb_im = fr[..., None] * b_im + fi[..., None] * b_re
    pr, pi = [jnp.ones_like(abar_re)], [jnp.zeros_like(abar_re)]
    for _ in range(CHUNK):
        pr, pi = (pr + [pr[-1] * abar_re - pi[-1] * abar_im],
                  pi + [pr[-1] * abar_im + pi[-1] * abar_re])
    pw_re = jnp.stack(pr, axis=-1)
    pw_im = jnp.stack(pi, axis=-1)

    col = jnp.arange(MXU_DIM)
    pick = (CHUNK - 1 - col[None, :] // SSM_GROUP == jnp.arange(CHUNK)[:, None]).astype(F32)
    e_re = jnp.einsum("gpn,nc->gpc", pw_re[..., :CHUNK], pick, precision=hp)
    e_im = jnp.einsum("gpn,nc->gpc", pw_im[..., :CHUNK], pick, precision=hp)
    spread = (col[None, :] % SSM_GROUP == jnp.arange(SSM_GROUP)[:, None]).astype(F32)
    bx_re = jnp.einsum("gpi,ic->gpc", bb_re, spread, precision=hp)
    bx_im = jnp.einsum("gpi,ic->gpc", bb_im, spread, precision=hp)
    p_re = e_re * bx_re - e_im * bx_im
    p_im = e_re * bx_im + e_im * bx_re
    pm = jnp.concatenate([p_re, p_im], axis=1).astype(BF16)

    krev = (jnp.einsum("gop,gpc->goc", c_re, p_re, precision=hp)
            - jnp.einsum("gop,gpc->goc", c_im, p_im, precision=hp))
    padded = jnp.concatenate([krev, jnp.zeros_like(krev)], axis=-1)
    tz = jnp.stack([padded[..., (CHUNK - 1 - t) * SSM_GROUP:(CHUNK - 1 - t) * SSM_GROUP + MXU_DIM]
                    for t in range(CHUNK)], axis=1)
    tz = tz.reshape(g, MXU_DIM, MXU_DIM).astype(BF16)

    o_re = pw_re[..., 1:].transpose(0, 2, 1)[:, :, None, :]
    o_im = pw_im[..., 1:].transpose(0, 2, 1)[:, :, None, :]
    cr = c_re[:, None]
    ci = c_im[:, None]
    qm = jnp.concatenate([cr * o_re - ci * o_im, -(cr * o_im + ci * o_re)], axis=-1)
    qm = qm.reshape(g, MXU_DIM, 2 * SSM_STATE).astype(BF16)

    pw = jnp.stack([pw_re[..., CHUNK], pw_im[..., CHUNK]])
    pw = pw.reshape(2, nblk, STATE_LANES).transpose(1, 0, 2)
    dsk = d_skip.reshape(nblk, 1, LANES)
    return tz, pm, qm, pw, dsk


def _mix_body(n_cast, x_ref, g_ref, ya_ref, zb_ref, wglu_ref, wma_ref, wmb_ref, wssm_ref,
              wconv_ref, wo_ref, *rest):
    cast_src = rest[:n_cast]
    h_ref = rest[n_cast]
    cast_dst = rest[n_cast + 1:2 * n_cast + 1]
    xn_scr, ya2_scr = rest[2 * n_cast + 1:]
    j = pl.program_id(1)

    @pl.when(j == 0)
    def _():
        x = x_ref[...]
        xn_scr[...] = _rmsnorm(x, g_ref[...]).astype(BF16)
        ya = ya_ref[...]
        ya2_scr[...] = (ya * jax.nn.sigmoid(_dot(ya.astype(BF16), wglu_ref[...]))).astype(BF16)
        h_ref[...] = x

    xn = xn_scr[...]
    pa = _dot(ya2_scr[...], wssm_ref[...])
    pb = _dot(zb_ref[...], wconv_ref[...])
    merged = (jax.nn.sigmoid(_dot(xn, wma_ref[...])) * pa
              + jax.nn.sigmoid(_dot(xn, wmb_ref[...])) * pb)
    h_ref[...] += _dot(merged.astype(BF16), wo_ref[...])
    _cast_riders(cast_src, cast_dst)


def _mix(x2, norm_g, ya, zb, w_glu_bf, w_merge_bf, w_ssm_bf, w_conv_bf, w_o_bf, riders,
         col_tile=512):
    n, d = x2.shape
    sw = ya.shape[1]
    cw = zb.shape[1]
    ncol = d // col_tile
    steps = (n // TOKEN_TILE) * ncol
    const = lambda i, j: (0, 0)
    cast_in, cast_out, cast_shape = [], [], []
    for w in riders:
        rows = w.shape[0] // steps
        spec = pl.BlockSpec((rows, w.shape[1]), lambda i, j: (i * ncol + j, 0))
        cast_in.append(spec)
        cast_out.append(spec)
        cast_shape.append(jax.ShapeDtypeStruct(w.shape, BF16))
    outs = pl.pallas_call(
        functools.partial(_mix_body, len(riders)),
        grid=(n // TOKEN_TILE, ncol),
        in_specs=[
            pl.BlockSpec((TOKEN_TILE, d), lambda i, j: (i, 0)),
            pl.BlockSpec((1, d), const),
            pl.BlockSpec((TOKEN_TILE, sw), lambda i, j: (i, 0)),
            pl.BlockSpec((TOKEN_TILE, cw), lambda i, j: (i, 0)),
            pl.BlockSpec((sw, sw), const),
            pl.BlockSpec((d, col_tile), lambda i, j: (0, j)),
            pl.BlockSpec((d, col_tile), lambda i, j: (0, ncol + j)),
            pl.BlockSpec((sw, col_tile), lambda i, j: (0, j)),
            pl.BlockSpec((cw, col_tile), lambda i, j: (0, j)),
            pl.BlockSpec((col_tile, d), lambda i, j: (j, 0)),
        ] + cast_in,
        out_specs=[pl.BlockSpec((TOKEN_TILE, d), lambda i, j: (i, 0))] + cast_out,
        out_shape=[jax.ShapeDtypeStruct((n, d), F32)] + cast_shape,
        scratch_shapes=[pltpu.VMEM((TOKEN_TILE, d), BF16),
                        pltpu.VMEM((TOKEN_TILE, sw), BF16)],
        compiler_params=pltpu.CompilerParams(
            dimension_semantics=("arbitrary", "arbitrary"),
            vmem_limit_bytes=VMEM_LIMIT_BYTES),
        name="mix",
    )(x2, norm_g, ya, zb, w_glu_bf, w_merge_bf, w_merge_bf, w_ssm_bf, w_conv_bf, w_o_bf,
      *riders)
    return outs[0], outs[1:]


def _ffn_body(tiles_per_seq, final_norm, h_hbm, g_ref, wa_ref, wg_ref, cw_ref, cb_ref,
              wd_ref, gfin_ref, o_ref, hbuf, hn_scr, prev_scr, sem):
    i = pl.program_id(0)
    f = pl.program_id(1)
    tm = o_ref.shape[0]

    def h_copy(tile):
        rows = pl.ds(pl.multiple_of(tile * tm, tm), tm)
        return pltpu.make_async_copy(h_hbm.at[rows], hbuf, sem)

    @pl.when(f == 0)
    def _():
        @pl.when(i == 0)
        def _():
            h_copy(0).start()

        h_copy(i).wait()
        h = hbuf[...]
        hn_scr[...] = _rmsnorm(h, g_ref[...]).astype(BF16)
        o_ref[...] = h

        @pl.when(i + 1 < pl.num_programs(0))
        def _():
            h_copy(i + 1).start()

    @pl.when(i % tiles_per_seq == 0)
    def _():
        prev_scr[f] = jnp.zeros(prev_scr.shape[1:], F32)

    hn = hn_scr[...]
    a = _dot(hn, wa_ref[...])
    conv = _causal_conv3(a, prev_scr[f], cw_ref, cb_ref)
    prev_scr[f] = a[a.shape[0] - SUBLANES:]
    act = (jax.nn.gelu(conv) * _dot(hn, wg_ref[...])).astype(BF16)
    o_ref[...] += _dot(act, wd_ref[...])

    if final_norm:
        @pl.when(f == pl.num_programs(1) - 1)
        def _():
            o_ref[...] = _rmsnorm(o_ref[...], gfin_ref[...])


def _ffn(h1, norm_g, w_up_bf, ffn_conv_w, ffn_conv_b, w_down_bf, norm_final, final_norm,
         seq_len, ff_tile=512):
    n, d = h1.shape
    d_ff = w_down_bf.shape[0]
    nff = d_ff // ff_tile
    tm = FFN_TOKEN_TILE
    tiles_per_seq = seq_len // tm
    const = lambda i, f: (0, 0)
    return pl.pallas_call(
        functools.partial(_ffn_body, tiles_per_seq, final_norm),
        grid=(n // tm, nff),
        in_specs=[
            pl.BlockSpec(memory_space=pl.ANY),
            pl.BlockSpec((1, d), const),
            pl.BlockSpec((d, ff_tile), lambda i, f: (0, f)),
            pl.BlockSpec((d, ff_tile), lambda i, f: (0, nff + f)),
            pl.BlockSpec((CONV_K, ff_tile), lambda i, f: (0, f)),
            pl.BlockSpec((1, ff_tile), lambda i, f: (0, f)),
            pl.BlockSpec((ff_tile, d), lambda i, f: (f, 0)),
            pl.BlockSpec((1, d), const),
        ],
        out_specs=pl.BlockSpec((tm, d), lambda i, f: (i, 0)),
        out_shape=jax.ShapeDtypeStruct((n, d), F32),
        scratch_shapes=[pltpu.VMEM((tm, d), F32),
                        pltpu.VMEM((tm, d), BF16),
                        pltpu.VMEM((nff, SUBLANES, ff_tile), F32),
                        pltpu.SemaphoreType.DMA(())],
        compiler_params=pltpu.CompilerParams(
            dimension_semantics=("arbitrary", "arbitrary"),
            vmem_limit_bytes=VMEM_LIMIT_BYTES),
        name="ffn",
    )(h1, norm_g, w_up_bf, w_up_bf, ffn_conv_w, ffn_conv_b, w_down_bf, norm_final)


def kernel(x, norm_tok, w_in, a_re, a_im, log_dt, b_re, b_im, c_re, c_im, d_skip, w_glu, w_ssm_out, conv_w, conv_b, w_conv_out, w_o, norm_ffn, w_up, ffn_conv_w, ffn_conv_b, w_down, norm_final):
    batch, seq_len, d = x.shape
    depth = w_in.shape[0]
    h = x.reshape(batch * seq_len, d)
    for l in range(depth):
        n_first = w_in.shape[2] - 2 * d
        w_first_bf = w_in[l, :, :n_first].astype(BF16)
        u, zb, (w_merge_bf, w_glu_bf, w_ssm_bf, w_conv_bf, w_o_bf, w_down_bf) = _inproj(
            h, norm_tok[l][None], w_first_bf, conv_w[l], conv_b[l][None], seq_len,
            [(w_in[l], (n_first // (2 * d), 2 * d)), (w_glu[l], None), (w_ssm_out[l], None),
             (w_conv_out[l], None), (w_o[l], None), (w_down[l], None)])
        ops = _ssm_operators(a_re[l], a_im[l], log_dt[l], b_re[l], b_im[l],
                             c_re[l], c_im[l], d_skip[l])
        ya = _ssm(u, *ops, batch, seq_len)
        h1, (w_up_bf,) = _mix(
            h, norm_tok[l][None], ya, zb, w_glu_bf, w_merge_bf, w_ssm_bf, w_conv_bf, w_o_bf,
            [w_up[l]])
        h = _ffn(h1, norm_ffn[l][None], w_up_bf, ffn_conv_w[l], ffn_conv_b[l][None],
                 w_down_bf, norm_final[None], l == depth - 1, seq_len)
    return h.reshape(batch, seq_len, d)
```

```python
import functools

import jax
import jax.numpy as jnp
from jax import lax
from jax.experimental import pallas as pl
from jax.experimental.pallas import tpu as pltpu

SSM_GROUP = 16
SSM_STATE = 64
CONV_K = 3
EPS = 1e-6

LANES = 128
SUBLANES = 8
MXU_DIM = 256
VMEM_LIMIT_BYTES = 56 * 1024 * 1024

TOKEN_TILE = 512
FFN_TOKEN_TILE = 1024
CHUNK = 16
GROUPS_PER_LANE_BLOCK = LANES // SSM_GROUP
STATE_LANES = GROUPS_PER_LANE_BLOCK * SSM_STATE

F32 = jnp.float32
BF16 = jnp.bfloat16


def _rmsnorm(x, gain):
    return x * lax.rsqrt(jnp.mean(x * x, axis=-1, keepdims=True) + EPS) * gain


def _dot(a, b):
    return jnp.dot(a, b, preferred_element_type=F32)


def _causal_conv3(a, prev, w_ref, b_ref):
    w0 = w_ref[0:1, :]
    w1 = w_ref[1:2, :]
    w2 = w_ref[2:3, :]
    bias = b_ref[...]
    full = w2 * a + w1 * pltpu.roll(a, 1, 0) + w0 * pltpu.roll(a, 2, 0) + bias
    top = a[0:SUBLANES]
    rows = lax.broadcasted_iota(jnp.int32, top.shape, 0)
    s1 = jnp.where(rows < 1, pltpu.roll(prev, 1, 0), pltpu.roll(top, 1, 0))
    s2 = jnp.where(rows < 2, pltpu.roll(prev, 2, 0), pltpu.roll(top, 2, 0))
    fix = w2 * top + w1 * s1 + w0 * s2 + bias
    return jnp.concatenate([fix, full[SUBLANES:]], axis=0)


def _cast_riders(src_refs, dst_refs):
    for src, dst in zip(src_refs, dst_refs):
        dst[...] = src[...].astype(BF16)


def _inproj_body(tiles_per_seq, n_cast, x_ref, g_ref, wu_ref, wv_ref, wgb_ref, wgc_ref,
                 cw_ref, cb_ref, *rest):
    cast_src = rest[:n_cast]
    u_ref, zb_ref, xn_ref = rest[n_cast:n_cast + 3]
    cast_dst = rest[n_cast + 3:2 * n_cast + 3]
    prev_ref = rest[2 * n_cast + 3]
    i = pl.program_id(0)

    @pl.when(i % tiles_per_seq == 0)
    def _():
        prev_ref[...] = jnp.zeros_like(prev_ref)

    xn = _rmsnorm(x_ref[...], g_ref[...]).astype(BF16)
    xn_ref[...] = xn
    u_ref[...] = _dot(xn, wu_ref[...])
    cin = _dot(xn, wgc_ref[...]) * _dot(xn, wv_ref[...])
    conv = _causal_conv3(cin, prev_ref[...], cw_ref, cb_ref)
    prev_ref[...] = cin[cin.shape[0] - SUBLANES:]
    zb_ref[...] = (_dot(xn, wgb_ref[...]) * conv).astype(BF16)
    _cast_riders(cast_src, cast_dst)


def _inproj(x2, norm_g, w_in_bf, conv_w, conv_b, seq_len, riders):
    n, d = x2.shape
    cw = conv_w.shape[1]
    steps = n // TOKEN_TILE
    tiles_per_seq = seq_len // TOKEN_TILE
    const = lambda i: (0, 0)
    wspec = lambda col: pl.BlockSpec((d, cw), lambda i, col=col: (0, col),
                                     pipeline_mode=pl.Buffered(1))
    cast_in, cast_out, cast_shape = [], [], []
    for w, cols in riders:
        rows = w.shape[0] // steps
        width = w.shape[1] if cols is None else cols[1]
        col = 0 if cols is None else cols[0]
        cast_in.append(pl.BlockSpec((rows, width), lambda i, col=col: (i, col)))
        cast_out.append(pl.BlockSpec((rows, width), lambda i: (i, 0)))
        cast_shape.append(jax.ShapeDtypeStruct((w.shape[0], width), BF16))
    outs = pl.pallas_call(
        functools.partial(_inproj_body, tiles_per_seq, len(riders)),
        grid=(steps,),
        in_specs=[
            pl.BlockSpec((TOKEN_TILE, d), lambda i: (i, 0)),
            pl.BlockSpec((1, d), const),
            wspec(0), wspec(1), wspec(2), wspec(3),
            pl.BlockSpec((CONV_K, cw), const),
            pl.BlockSpec((1, cw), const),
        ] + cast_in,
        out_specs=[
            pl.BlockSpec((TOKEN_TILE, cw), lambda i: (i, 0)),
            pl.BlockSpec((TOKEN_TILE, cw), lambda i: (i, 0)),
            pl.BlockSpec((TOKEN_TILE, d), lambda i: (i, 0)),
        ] + cast_out,
        out_shape=[
            jax.ShapeDtypeStruct((n, cw), F32),
            jax.ShapeDtypeStruct((n, cw), BF16),
            jax.ShapeDtypeStruct((n, d), BF16),
        ] + cast_shape,
        scratch_shapes=[pltpu.VMEM((SUBLANES, cw), F32)],
        compiler_params=pltpu.CompilerParams(
            dimension_semantics=("arbitrary",), vmem_limit_bytes=VMEM_LIMIT_BYTES),
        name="inproj",
    )(x2, norm_g, w_in_bf, w_in_bf, w_in_bf, w_in_bf, conv_w, conv_b, *[w for w, _ in riders])
    return outs[0], outs[1], outs[2], outs[3:]


def _ssm_body(n_cast, u_ref, tz_ref, p_ref, q_ref, pw_ref, d_ref, *rest):
    cast_src = rest[:n_cast]
    y_ref = rest[n_cast]
    cast_dst = rest[n_cast + 1:2 * n_cast + 1]
    s_scr, xin_scr = rest[2 * n_cast + 1:]
    _cast_riders(cast_src, cast_dst)
    nseq = u_ref.shape[0]
    m = u_ref.shape[1] // CHUNK
    sl = STATE_LANES
    ngl = GROUPS_PER_LANE_BLOCK

    yts, pieces = [], []
    for s in range(nseq):
        pieces.append([u_ref[s, pl.ds(t, m, stride=CHUNK), :] for t in range(CHUNK)])
        pts = [p.T.astype(BF16) for p in pieces[s]]
        ys, s_re, s_im = [], [], []
        for g in range(ngl):
            xg = jnp.concatenate([pt[g * SSM_GROUP:(g + 1) * SSM_GROUP] for pt in pts], axis=0)
            ys.append(_dot(tz_ref[g], xg))
            sg = _dot(p_ref[g], xg)
            s_re.append(sg[:SSM_STATE])
            s_im.append(sg[SSM_STATE:])
        yts.append(ys)
        s_scr[s] = jnp.concatenate(s_re + s_im, axis=0).T

    a16r = pw_ref[0:1, :]
    a16i = pw_ref[1:2, :]

    def step(k, carry):
        out = []
        for s in range(nseq):
            xr, xi = carry[2 * s], carry[2 * s + 1]
            xin_scr[s, pl.ds(k, 1), :sl] = xr
            xin_scr[s, pl.ds(k, 1), sl:] = xi
            row = s_scr[s, pl.ds(k, 1), :]
            out += [a16r * xr - a16i * xi + row[:, :sl], a16r * xi + a16i * xr + row[:, sl:]]
        return tuple(out)

    zero = jnp.zeros((1, sl), F32)
    lax.fori_loop(0, m, step, (zero,) * (2 * nseq), unroll=8)

    d = d_ref[...]
    for s in range(nseq):
        xint = xin_scr[s].T
        ys = yts[s]
        for g in range(ngl):
            xg = jnp.concatenate([xint[g * SSM_STATE:(g + 1) * SSM_STATE],
                                  xint[sl + g * SSM_STATE:sl + (g + 1) * SSM_STATE]], axis=0)
            ys[g] = ys[g] + _dot(q_ref[g], xg.astype(BF16))
        for t in range(CHUNK):
            yt = jnp.concatenate([y[t * SSM_GROUP:(t + 1) * SSM_GROUP] for y in ys], axis=0)
            y_ref[s, pl.ds(t, m, stride=CHUNK), :] = jax.nn.gelu(yt.T + d * pieces[s][t])


def _ssm(u, tz, pm, qm, pw, dsk, batch, seq_len, riders, seqs_per_step=2):
    n, width = u.shape
    nblk = width // LANES
    m = seq_len // CHUNK
    ngl = GROUPS_PER_LANE_BLOCK
    u3 = u.reshape(batch, seq_len, width)
    nb = batch // seqs_per_step
    cast_specs = [pl.BlockSpec((w.shape[0] // (nblk * nb), w.shape[1]),
                               lambda c, b: (c * nb + b, 0)) for w in riders]
    outs = pl.pallas_call(
        functools.partial(_ssm_body, len(riders)),
        grid=(nblk, nb),
        in_specs=[
            pl.BlockSpec((seqs_per_step, seq_len, LANES), lambda c, b: (b, 0, c)),
            pl.BlockSpec((ngl, MXU_DIM, MXU_DIM), lambda c, b: (c, 0, 0)),
            pl.BlockSpec((ngl, 2 * SSM_STATE, MXU_DIM), lambda c, b: (c, 0, 0)),
            pl.BlockSpec((ngl, MXU_DIM, 2 * SSM_STATE), lambda c, b: (c, 0, 0)),
            pl.BlockSpec((None, 2, STATE_LANES), lambda c, b: (c, 0, 0)),
            pl.BlockSpec((None, 1, LANES), lambda c, b: (c, 0, 0)),
        ] + cast_specs,
        out_specs=[pl.BlockSpec((seqs_per_step, seq_len, LANES), lambda c, b: (b, 0, c))]
        + cast_specs,
        out_shape=[jax.ShapeDtypeStruct((batch, seq_len, width), F32)]
        + [jax.ShapeDtypeStruct(w.shape, BF16) for w in riders],
        scratch_shapes=[pltpu.VMEM((seqs_per_step, m, 2 * STATE_LANES), F32),
                        pltpu.VMEM((seqs_per_step, m, 2 * STATE_LANES), F32)],
        compiler_params=pltpu.CompilerParams(
            dimension_semantics=("arbitrary", "arbitrary"),
            vmem_limit_bytes=VMEM_LIMIT_BYTES),
        name="ssm",
    )(u3, tz, pm, qm, pw, dsk, *riders)
    return outs[0].reshape(n, width), outs[1:]


def _ssm_operators(a_re, a_im, log_dt, b_re, b_im, c_re, c_im, d_skip):
    hp = lax.Precision.HIGHEST
    g = a_re.shape[0]
    nblk = g // GROUPS_PER_LANE_BLOCK
    dt = jnp.exp(log_dt)[:, None]
    mag = jnp.exp(dt * a_re)
    abar_re = mag * jnp.cos(dt * a_im)
    abar_im = mag * jnp.sin(dt * a_im)
    nr = abar_re - 1.0
    ni = abar_im
    den = a_re * a_re + a_im * a_im
    fr = (nr * a_re + ni * a_im) / den
    fi = (ni * a_re - nr * a_im) / den
    bb_re = fr[..., None] * b_re - fi[..., None] * b_im
    bb_im = fr[..., None] * b_im + fi[..., None] * b_re
    pr, pi = [jnp.ones_like(abar_re)], [jnp.zeros_like(abar_re)]
    for _ in range(CHUNK):
        pr, pi = (pr + [pr[-1] * abar_re - pi[-1] * abar_im],
                  pi + [pr[-1] * abar_im + pi[-1] * abar_re])
    pw_re = jnp.stack(pr, axis=-1)
    pw_im = jnp.stack(pi, axis=-1)

    col = jnp.arange(MXU_DIM)
    pick = (CHUNK - 1 - col[None, :] // SSM_GROUP == jnp.arange(CHUNK)[:, None]).astype(F32)
    e_re = jnp.einsum("gpn,nc->gpc", pw_re[..., :CHUNK], pick, precision=hp)
    e_im = jnp.einsum("gpn,nc->gpc", pw_im[..., :CHUNK], pick, precision=hp)
    spread = (col[None, :] % SSM_GROUP == jnp.arange(SSM_GROUP)[:, None]).astype(F32)
    bx_re = jnp.einsum("gpi,ic->gpc", bb_re, spread, precision=hp)
    bx_im = jnp.einsum("gpi,ic->gpc", bb_im, spread, precision=hp)
    p_re = e_re * bx_re - e_im * bx_im
    p_im = e_re * bx_im + e_im * bx_re
    pm = jnp.concatenate([p_re, p_im], axis=1).astype(BF16)

    krev = (jnp.einsum("gop,gpc->goc", c_re, p_re, precision=hp)
            - jnp.einsum("gop,gpc->goc", c_im, p_im, precision=hp))
    padded = jnp.concatenate([krev, jnp.zeros_like(krev)], axis=-1)
    tz = jnp.stack([padded[..., (CHUNK - 1 - t) * SSM_GROUP:(CHUNK - 1 - t) * SSM_GROUP + MXU_DIM]
                    for t in range(CHUNK)], axis=1)
    tz = tz.reshape(g, MXU_DIM, MXU_DIM).astype(BF16)

    o_re = pw_re[..., 1:].transpose(0, 2, 1)[:, :, None, :]
    o_im = pw_im[..., 1:].transpose(0, 2, 1)[:, :, None, :]
    cr = c_re[:, None]
    ci = c_im[:, None]
    qm = jnp.concatenate([cr * o_re - ci * o_im, -(cr * o_im + ci * o_re)], axis=-1)
    qm = qm.reshape(g, MXU_DIM, 2 * SSM_STATE).astype(BF16)

    pw = jnp.stack([pw_re[..., CHUNK], pw_im[..., CHUNK]])
    pw = pw.reshape(2, nblk, STATE_LANES).transpose(1, 0, 2)
    dsk = d_skip.reshape(nblk, 1, LANES)
    return tz, pm, qm, pw, dsk


def _merge_body(col_tile, xn_ref, ya_ref, zb_ref, wglu_ref, wm_ref, wssm_ref, wconv_ref, m_ref):
    d = m_ref.shape[1]
    ya = ya_ref[...]
    ya2 = (ya * jax.nn.sigmoid(_dot(ya.astype(BF16), wglu_ref[...]))).astype(BF16)
    xn = xn_ref[...]
    zb = zb_ref[...]
    for c in range(d // col_tile):
        cols = slice(c * col_tile, (c + 1) * col_tile)
        gate_a = jax.nn.sigmoid(_dot(xn, wm_ref[:, cols]))
        gate_b = jax.nn.sigmoid(_dot(xn, wm_ref[:, d + c * col_tile:d + (c + 1) * col_tile]))
        m_ref[:, cols] = (gate_a * _dot(ya2, wssm_ref[:, cols])
                          + gate_b * _dot(zb, wconv_ref[:, cols])).astype(BF16)


def _merge(xn, ya, zb, w_glu_bf, w_merge_bf, w_ssm_bf, w_conv_bf, col_tile=512):
    n, d = xn.shape
    sw = ya.shape[1]
    cw = zb.shape[1]
    resident = lambda shape: pl.BlockSpec(shape, lambda i: (0, 0), pipeline_mode=pl.Buffered(1))
    return pl.pallas_call(
        functools.partial(_merge_body, col_tile),
        grid=(n // TOKEN_TILE,),
        in_specs=[
            pl.BlockSpec((TOKEN_TILE, d), lambda i: (i, 0)),
            pl.BlockSpec((TOKEN_TILE, sw), lambda i: (i, 0)),
            pl.BlockSpec((TOKEN_TILE, cw), lambda i: (i, 0)),
            resident((sw, sw)),
            resident((d, 2 * d)),
            resident((sw, d)),
            resident((cw, d)),
        ],
        out_specs=pl.BlockSpec((TOKEN_TILE, d), lambda i: (i, 0)),
        out_shape=jax.ShapeDtypeStruct((n, d), BF16),
        compiler_params=pltpu.CompilerParams(
            dimension_semantics=("arbitrary",), vmem_limit_bytes=VMEM_LIMIT_BYTES),
        name="merge",
    )(xn, ya, zb, w_glu_bf, w_merge_bf, w_ssm_bf, w_conv_bf)


def _outproj_body(col_tile, x_ref, m_ref, wo_ref, h_ref):
    m = m_ref[...]
    for c in range(h_ref.shape[1] // col_tile):
        cols = slice(c * col_tile, (c + 1) * col_tile)
        h_ref[:, cols] = x_ref[:, cols] + _dot(m, wo_ref[:, cols])


def _outproj(x2, merged, w_o_bf, col_tile=512):
    n, d = x2.shape
    tm = FFN_TOKEN_TILE
    return pl.pallas_call(
        functools.partial(_outproj_body, col_tile),
        grid=(n // tm,),
        in_specs=[
            pl.BlockSpec((tm, d), lambda i: (i, 0)),
            pl.BlockSpec((tm, d), lambda i: (i, 0)),
            pl.BlockSpec((d, d), lambda i: (0, 0), pipeline_mode=pl.Buffered(1)),
        ],
        out_specs=pl.BlockSpec((tm, d), lambda i: (i, 0)),
        out_shape=jax.ShapeDtypeStruct((n, d), F32),
        compiler_params=pltpu.CompilerParams(
            dimension_semantics=("arbitrary",), vmem_limit_bytes=VMEM_LIMIT_BYTES),
        name="outproj",
    )(x2, merged, w_o_bf)


def _ffn_body(tiles_per_seq, final_norm, h_hbm, g_ref, wa_ref, wg_ref, cw_ref, cb_ref,
              wd_ref, gfin_ref, o_ref, hbuf, hn_scr, prev_scr, sem):
    i = pl.program_id(0)
    f = pl.program_id(1)
    tm = o_ref.shape[0]

    def h_copy(tile):
        rows = pl.ds(pl.multiple_of(tile * tm, tm), tm)
        return pltpu.make_async_copy(h_hbm.at[rows], hbuf, sem)

    @pl.when(f == 0)
    def _():
        @pl.when(i == 0)
        def _():
            h_copy(0).start()

        h_copy(i).wait()
        h = hbuf[...]
        hn_scr[...] = _rmsnorm(h, g_ref[...]).astype(BF16)
        o_ref[...] = h

        @pl.when(i + 1 < pl.num_programs(0))
        def _():
            h_copy(i + 1).start()

    @pl.when(i % tiles_per_seq == 0)
    def _():
        prev_scr[f] = jnp.zeros(prev_scr.shape[1:], F32)

    hn = hn_scr[...]
    a = _dot(hn, wa_ref[...])
    conv = _causal_conv3(a, prev_scr[f], cw_ref, cb_ref)
    prev_scr[f] = a[a.shape[0] - SUBLANES:]
    act = (jax.nn.gelu(conv) * _dot(hn, wg_ref[...])).astype(BF16)
    o_ref[...] += _dot(act, wd_ref[...])

    if final_norm:
        @pl.when(f == pl.num_programs(1) - 1)
        def _():
            o_ref[...] = _rmsnorm(o_ref[...], gfin_ref[...])


def _ffn(h1, norm_g, w_up_bf, ffn_conv_w, ffn_conv_b, w_down_bf, norm_final, final_norm,
         seq_len, ff_tile=512):
    n, d = h1.shape
    d_ff = w_down_bf.shape[0]
    nff = d_ff // ff_tile
    tm = FFN_TOKEN_TILE
    tiles_per_seq = seq_len // tm
    const = lambda i, f: (0, 0)
    return pl.pallas_call(
        functools.partial(_ffn_body, tiles_per_seq, final_norm),
        grid=(n // tm, nff),
        in_specs=[
            pl.BlockSpec(memory_space=pl.ANY),
            pl.BlockSpec((1, d), const),
            pl.BlockSpec((d, ff_tile), lambda i, f: (0, f)),
            pl.BlockSpec((d, ff_tile), lambda i, f: (0, nff + f)),
            pl.BlockSpec((CONV_K, ff_tile), lambda i, f: (0, f)),
            pl.BlockSpec((1, ff_tile), lambda i, f: (0, f)),
            pl.BlockSpec((ff_tile, d), lambda i, f: (f, 0)),
            pl.BlockSpec((1, d), const),
        ],
        out_specs=pl.BlockSpec((tm, d), lambda i, f: (i, 0)),
        out_shape=jax.ShapeDtypeStruct((n, d), F32),
        scratch_shapes=[pltpu.VMEM((tm, d), F32),
                        pltpu.VMEM((tm, d), BF16),
                        pltpu.VMEM((nff, SUBLANES, ff_tile), F32),
                        pltpu.SemaphoreType.DMA(())],
        compiler_params=pltpu.CompilerParams(
            dimension_semantics=("arbitrary", "arbitrary"),
            vmem_limit_bytes=VMEM_LIMIT_BYTES),
        name="ffn",
    )(h1, norm_g, w_up_bf, w_up_bf, ffn_conv_w, ffn_conv_b, w_down_bf, norm_final)


def kernel(x, norm_tok, w_in, a_re, a_im, log_dt, b_re, b_im, c_re, c_im, d_skip, w_glu, w_ssm_out, conv_w, conv_b, w_conv_out, w_o, norm_ffn, w_up, ffn_conv_w, ffn_conv_b, w_down, norm_final):
    batch, seq_len, d = x.shape
    depth = w_in.shape[0]
    h = x.reshape(batch * seq_len, d)
    for l in range(depth):
        n_first = w_in.shape[2] - 2 * d
        w_first_bf = w_in[l, :, :n_first].astype(BF16)
        u, zb, xn, (w_merge_bf, w_glu_bf, w_ssm_bf, w_conv_bf, w_o_bf, w_down_bf) = _inproj(
            h, norm_tok[l][None], w_first_bf, conv_w[l], conv_b[l][None], seq_len,
            [(w_in[l], (n_first // (2 * d), 2 * d)), (w_glu[l], None), (w_ssm_out[l], None),
             (w_conv_out[l], None), (w_o[l], None), (w_down[l], None)])
        ops = _ssm_operators(a_re[l], a_im[l], log_dt[l], b_re[l], b_im[l],
                             c_re[l], c_im[l], d_skip[l])
        ya, (w_up_bf,) = _ssm(u, *ops, batch, seq_len, [w_up[l]])
        merged = _merge(xn, ya, zb, w_glu_bf, w_merge_bf, w_ssm_bf, w_conv_bf)
        h1 = _outproj(h, merged, w_o_bf)
        h = _ffn(h1, norm_ffn[l][None], w_up_bf, ffn_conv_w[l], ffn_conv_b[l][None],
                 w_down_bf, norm_final[None], l == depth - 1, seq_len)
    return h.reshape(batch, seq_len, d)
```

```python
import functools

import jax
import jax.numpy as jnp
from jax import lax
from jax.experimental import pallas as pl
from jax.experimental.pallas import tpu as pltpu

SSM_GROUP = 16
SSM_STATE = 64
CONV_K = 3
EPS = 1e-6

LANES = 128
SUBLANES = 8
MXU_DIM = 256
VMEM_LIMIT_BYTES = 56 * 1024 * 1024

TOKEN_TILE = 512
FFN_TOKEN_TILE = 1024
CHUNK = 16
GROUPS_PER_LANE_BLOCK = LANES // SSM_GROUP
STATE_LANES = GROUPS_PER_LANE_BLOCK * SSM_STATE

F32 = jnp.float32
BF16 = jnp.bfloat16


def _rmsnorm(x, gain):
    return x * lax.rsqrt(jnp.mean(x * x, axis=-1, keepdims=True) + EPS) * gain


def _dot(a, b):
    return jnp.dot(a, b, preferred_element_type=F32)


def _causal_conv3(a, prev, w_ref, b_ref):
    w0 = w_ref[0:1, :]
    w1 = w_ref[1:2, :]
    w2 = w_ref[2:3, :]
    bias = b_ref[...]
    full = w2 * a + w1 * pltpu.roll(a, 1, 0) + w0 * pltpu.roll(a, 2, 0) + bias
    top = a[0:SUBLANES]
    rows = lax.broadcasted_iota(jnp.int32, top.shape, 0)
    s1 = jnp.where(rows < 1, pltpu.roll(prev, 1, 0), pltpu.roll(top, 1, 0))
    s2 = jnp.where(rows < 2, pltpu.roll(prev, 2, 0), pltpu.roll(top, 2, 0))
    fix = w2 * top + w1 * s1 + w0 * s2 + bias
    return jnp.concatenate([fix, full[SUBLANES:]], axis=0)


def _cast_riders(src_refs, dst_refs):
    for src, dst in zip(src_refs, dst_refs):
        dst[...] = src[...].astype(BF16)


def _inproj_body(tiles_per_seq, n_cast, x_ref, g_ref, wu_ref, wv_ref, wgb_ref, wgc_ref,
                 cw_ref, cb_ref, *rest):
    cast_src = rest[:n_cast]
    u_ref, zb_ref, xn_ref = rest[n_cast:n_cast + 3]
    cast_dst = rest[n_cast + 3:2 * n_cast + 3]
    prev_ref = rest[2 * n_cast + 3]
    i = pl.program_id(0)

    @pl.when(i % tiles_per_seq == 0)
    def _():
        prev_ref[...] = jnp.zeros_like(prev_ref)

    xn = _rmsnorm(x_ref[...], g_ref[...]).astype(BF16)
    xn_ref[...] = xn
    u_ref[...] = _dot(xn, wu_ref[...])
    cin = _dot(xn, wgc_ref[...]) * _dot(xn, wv_ref[...])
    conv = _causal_conv3(cin, prev_ref[...], cw_ref, cb_ref)
    prev_ref[...] = cin[cin.shape[0] - SUBLANES:]
    zb_ref[...] = (_dot(xn, wgb_ref[...]) * conv).astype(BF16)
    _cast_riders(cast_src, cast_dst)


def _inproj(x2, norm_g, w_in_bf, conv_w, conv_b, seq_len, riders):
    n, d = x2.shape
    cw = conv_w.shape[1]
    steps = n // TOKEN_TILE
    tiles_per_seq = seq_len // TOKEN_TILE
    const = lambda i: (0, 0)
    wspec = lambda col: pl.BlockSpec((d, cw), lambda i, col=col: (0, col),
                                     pipeline_mode=pl.Buffered(1))
    cast_in, cast_out, cast_shape = [], [], []
    for w, cols in riders:
        rows = w.shape[0] // steps
        width = w.shape[1] if cols is None else cols[1]
        col = 0 if cols is None else cols[0]
        cast_in.append(pl.BlockSpec((rows, width), lambda i, col=col: (i, col)))
        cast_out.append(pl.BlockSpec((rows, width), lambda i: (i, 0)))
        cast_shape.append(jax.ShapeDtypeStruct((w.shape[0], width), BF16))
    outs = pl.pallas_call(
        functools.partial(_inproj_body, tiles_per_seq, len(riders)),
        grid=(steps,),
        in_specs=[
            pl.BlockSpec((TOKEN_TILE, d), lambda i: (i, 0)),
            pl.BlockSpec((1, d), const),
            wspec(0), wspec(1), wspec(2), wspec(3),
            pl.BlockSpec((CONV_K, cw), const),
            pl.BlockSpec((1, cw), const),
        ] + cast_in,
        out_specs=[
            pl.BlockSpec((TOKEN_TILE, cw), lambda i: (i, 0)),
            pl.BlockSpec((TOKEN_TILE, cw), lambda i: (i, 0)),
            pl.BlockSpec((TOKEN_TILE, d), lambda i: (i, 0)),
        ] + cast_out,
        out_shape=[
            jax.ShapeDtypeStruct((n, cw), F32),
            jax.ShapeDtypeStruct((n, cw), BF16),
            jax.ShapeDtypeStruct((n, d), BF16),
        ] + cast_shape,
        scratch_shapes=[pltpu.VMEM((SUBLANES, cw), F32)],
        compiler_params=pltpu.CompilerParams(
            dimension_semantics=("arbitrary",), vmem_limit_bytes=VMEM_LIMIT_BYTES),
        name="inproj",
    )(x2, norm_g, w_in_bf, w_in_bf, w_in_bf, w_in_bf, conv_w, conv_b, *[w for w, _ in riders])
    return outs[0], outs[1], outs[2], outs[3:]


def _ssm_body(n_cast, u_ref, tz_ref, p_ref, q_ref, pw_ref, d_ref, *rest):
    cast_src = rest[:n_cast]
    y_ref = rest[n_cast]
    cast_dst = rest[n_cast + 1:2 * n_cast + 1]
    s_scr, xin_scr = rest[2 * n_cast + 1:]
    _cast_riders(cast_src, cast_dst)
    nseq = u_ref.shape[0]
    m = u_ref.shape[1] // CHUNK
    sl = STATE_LANES
    ngl = GROUPS_PER_LANE_BLOCK

    yts, pieces = [], []
    for s in range(nseq):
        pieces.append([u_ref[s, pl.ds(t, m, stride=CHUNK), :] for t in range(CHUNK)])
        pts = [p.T.astype(BF16) for p in pieces[s]]
        ys, s_re, s_im = [], [], []
        for g in range(ngl):
            xg = jnp.concatenate([pt[g * SSM_GROUP:(g + 1) * SSM_GROUP] for pt in pts], axis=0)
            ys.append(_dot(tz_ref[g], xg))
            sg = _dot(p_ref[g], xg)
            s_re.append(sg[:SSM_STATE])
            s_im.append(sg[SSM_STATE:])
        yts.append(ys)
        s_scr[s] = jnp.concatenate(s_re + s_im, axis=0).T

    a16r = pw_ref[0:1, :]
    a16i = pw_ref[1:2, :]

    def step(k, carry):
        out = []
        for s in range(nseq):
            xr, xi = carry[2 * s], carry[2 * s + 1]
            xin_scr[s, pl.ds(k, 1), :sl] = xr
            xin_scr[s, pl.ds(k, 1), sl:] = xi
            row = s_scr[s, pl.ds(k, 1), :]
            out += [a16r * xr - a16i * xi + row[:, :sl], a16r * xi + a16i * xr + row[:, sl:]]
        return tuple(out)

    zero = jnp.zeros((1, sl), F32)
    lax.fori_loop(0, m, step, (zero,) * (2 * nseq), unroll=8)

    d = d_ref[...]
    for s in range(nseq):
        xint = xin_scr[s].T
        ys = yts[s]
        for g in range(ngl):
            xg = jnp.concatenate([xint[g * SSM_STATE:(g + 1) * SSM_STATE],
                                  xint[sl + g * SSM_STATE:sl + (g + 1) * SSM_STATE]], axis=0)
            ys[g] = ys[g] + _dot(q_ref[g], xg.astype(BF16))
        for t in range(CHUNK):
            yt = jnp.concatenate([y[t * SSM_GROUP:(t + 1) * SSM_GROUP] for y in ys], axis=0)
            y_ref[s, pl.ds(t, m, stride=CHUNK), :] = yt.T + d * pieces[s][t]


def _ssm(u, tz, pm, qm, pw, dsk, batch, seq_len, riders, seqs_per_step=2):
    n, width = u.shape
    nblk = width // LANES
    m = seq_len // CHUNK
    ngl = GROUPS_PER_LANE_BLOCK
    u3 = u.reshape(batch, seq_len, width)
    nb = batch // seqs_per_step
    cast_specs = [pl.BlockSpec((w.shape[0] // (nblk * nb), w.shape[1]),
                               lambda c, b: (c * nb + b, 0)) for w in riders]
    outs = pl.pallas_call(
        functools.partial(_ssm_body, len(riders)),
        grid=(nblk, nb),
        in_specs=[
            pl.BlockSpec((seqs_per_step, seq_len, LANES), lambda c, b: (b, 0, c)),
            pl.BlockSpec((ngl, MXU_DIM, MXU_DIM), lambda c, b: (c, 0, 0)),
            pl.BlockSpec((ngl, 2 * SSM_STATE, MXU_DIM), lambda c, b: (c, 0, 0)),
            pl.BlockSpec((ngl, MXU_DIM, 2 * SSM_STATE), lambda c, b: (c, 0, 0)),
            pl.BlockSpec((None, 2, STATE_LANES), lambda c, b: (c, 0, 0)),
            pl.BlockSpec((None, 1, LANES), lambda c, b: (c, 0, 0)),
        ] + cast_specs,
        out_specs=[pl.BlockSpec((seqs_per_step, seq_len, LANES), lambda c, b: (b, 0, c))]
        + cast_specs,
        out_shape=[jax.ShapeDtypeStruct((batch, seq_len, width), F32)]
        + [jax.ShapeDtypeStruct(w.shape, BF16) for w in riders],
        scratch_shapes=[pltpu.VMEM((seqs_per_step, m, 2 * STATE_LANES), F32),
                        pltpu.VMEM((seqs_per_step, m, 2 * STATE_LANES), F32)],
        compiler_params=pltpu.CompilerParams(
            dimension_semantics=("arbitrary", "arbitrary"),
            vmem_limit_bytes=VMEM_LIMIT_BYTES),
        name="ssm",
    )(u3, tz, pm, qm, pw, dsk, *riders)
    return outs[0].reshape(n, width), outs[1:]


def _ssm_operators(a_re, a_im, log_dt, b_re, b_im, c_re, c_im, d_skip):
    hp = lax.Precision.HIGHEST
    g = a_re.shape[0]
    nblk = g // GROUPS_PER_LANE_BLOCK
    dt = jnp.exp(log_dt)[:, None]
    mag = jnp.exp(dt * a_re)
    abar_re = mag * jnp.cos(dt * a_im)
    abar_im = mag * jnp.sin(dt * a_im)
    nr = abar_re - 1.0
    ni = abar_im
    den = a_re * a_re + a_im * a_im
    fr = (nr * a_re + ni * a_im) / den
    fi = (ni * a_re - nr * a_im) / den
    bb_re = fr[..., None] * b_re - fi[..., None] * b_im
    bb_im = fr[..., None] * b_im + fi[..., None] * b_re
    pr, pi = [jnp.ones_like(abar_re)], [jnp.zeros_like(abar_re)]
    for _ in range(CHUNK):
        pr, pi = (pr + [pr[-1] * abar_re - pi[-1] * abar_im],
                  pi + [pr[-1] * abar_im + pi[-1] * abar_re])
    pw_re = jnp.stack(pr, axis=-1)
    pw_im = jnp.stack(pi, axis=-1)

    col = jnp.arange(MXU_DIM)
    pick = (CHUNK - 1 - col[None, :] // SSM_GROUP == jnp.arange(CHUNK)[:, None]).astype(F32)
    e_re = jnp.einsum("gpn,nc->gpc", pw_re[..., :CHUNK], pick, precision=hp)
    e_im = jnp.einsum("gpn,nc->gpc", pw_im[..., :CHUNK], pick, precision=hp)
    spread = (col[None, :] % SSM_GROUP == jnp.arange(SSM_GROUP)[:, None]).astype(F32)
    bx_re = jnp.einsum("gpi,ic->gpc", bb_re, spread, precision=hp)
    bx_im = jnp.einsum("gpi,ic->gpc", bb_im, spread, precision=hp)
    p_re = e_re * bx_re - e_im * bx_im
    p_im = e_re * bx_im + e_im * bx_re
    pm = jnp.concatenate([p_re, p_im], axis=1).astype(BF16)

    krev = (jnp.einsum("gop,gpc->goc", c_re, p_re, precision=hp)
            - jnp.einsum("gop,gpc->goc", c_im, p_im, precision=hp))
    padded = jnp.concatenate([krev, jnp.zeros_like(krev)], axis=-1)
    tz = jnp.stack([padded[..., (CHUNK - 1 - t) * SSM_GROUP:(CHUNK - 1 - t) * SSM_GROUP + MXU_DIM]
                    for t in range(CHUNK)], axis=1)
    tz = tz.reshape(g, MXU_DIM, MXU_DIM).astype(BF16)

    o_re = pw_re[..., 1:].transpose(0, 2, 1)[:, :, None, :]
    o_im = pw_im[..., 1:].transpose(0, 2, 1)[:, :, None, :]
    cr = c_re[:, None]
    ci = c_im[:, None]
    qm = jnp.concatenate([cr * o_re - ci * o_im, -(cr * o_im + ci * o_re)], axis=-1)
    qm = qm.reshape(g, MXU_DIM, 2 * SSM_STATE).astype(BF16)

    pw = jnp.stack([pw_re[..., CHUNK], pw_im[..., CHUNK]])
    pw = pw.reshape(2, nblk, STATE_LANES).transpose(1, 0, 2)
    dsk = d_skip.reshape(nblk, 1, LANES)
    return tz, pm, qm, pw, dsk


def _merge_body(col_tile, xn_ref, y_ref, zb_ref, wglu_ref, wm_ref, wssm_ref, wconv_ref, m_ref):
    d = m_ref.shape[1]
    ya = jax.nn.gelu(y_ref[...])
    ya2 = (ya * jax.nn.sigmoid(_dot(ya.astype(BF16), wglu_ref[...]))).astype(BF16)
    xn = xn_ref[...]
    zb = zb_ref[...]
    for c in range(d // col_tile):
        cols = slice(c * col_tile, (c + 1) * col_tile)
        gate_a = jax.nn.sigmoid(_dot(xn, wm_ref[:, cols]))
        gate_b = jax.nn.sigmoid(_dot(xn, wm_ref[:, d + c * col_tile:d + (c + 1) * col_tile]))
        m_ref[:, cols] = (gate_a * _dot(ya2, wssm_ref[:, cols])
                          + gate_b * _dot(zb, wconv_ref[:, cols])).astype(BF16)


def _merge(xn, ya, zb, w_glu_bf, w_merge_bf, w_ssm_bf, w_conv_bf, col_tile=512):
    n, d = xn.shape
    sw = ya.shape[1]
    cw = zb.shape[1]
    resident = lambda shape: pl.BlockSpec(shape, lambda i: (0, 0), pipeline_mode=pl.Buffered(1))
    return pl.pallas_call(
        functools.partial(_merge_body, col_tile),
        grid=(n // TOKEN_TILE,),
        in_specs=[
            pl.BlockSpec((TOKEN_TILE, d), lambda i: (i, 0)),
            pl.BlockSpec((TOKEN_TILE, sw), lambda i: (i, 0)),
            pl.BlockSpec((TOKEN_TILE, cw), lambda i: (i, 0)),
            resident((sw, sw)),
            resident((d, 2 * d)),
            resident((sw, d)),
            resident((cw, d)),
        ],
        out_specs=pl.BlockSpec((TOKEN_TILE, d), lambda i: (i, 0)),
        out_shape=jax.ShapeDtypeStruct((n, d), BF16),
        compiler_params=pltpu.CompilerParams(
            dimension_semantics=("arbitrary",), vmem_limit_bytes=VMEM_LIMIT_BYTES),
        name="merge",
    )(xn, ya, zb, w_glu_bf, w_merge_bf, w_ssm_bf, w_conv_bf)


def _outproj_body(col_tile, x_ref, m_ref, wo_ref, h_ref):
    m = m_ref[...]
    for c in range(h_ref.shape[1] // col_tile):
        cols = slice(c * col_tile, (c + 1) * col_tile)
        h_ref[:, cols] = x_ref[:, cols] + _dot(m, wo_ref[:, cols])


def _outproj(x2, merged, w_o_bf, col_tile=512):
    n, d = x2.shape
    tm = FFN_TOKEN_TILE
    return pl.pallas_call(
        functools.partial(_outproj_body, col_tile),
        grid=(n // tm,),
        in_specs=[
            pl.BlockSpec((tm, d), lambda i: (i, 0)),
            pl.BlockSpec((tm, d), lambda i: (i, 0)),
            pl.BlockSpec((d, d), lambda i: (0, 0), pipeline_mode=pl.Buffered(1)),
        ],
        out_specs=pl.BlockSpec((tm, d), lambda i: (i, 0)),
        out_shape=jax.ShapeDtypeStruct((n, d), F32),
        compiler_params=pltpu.CompilerParams(
            dimension_semantics=("arbitrary",), vmem_limit_bytes=VMEM_LIMIT_BYTES),
        name="outproj",
    )(x2, merged, w_o_bf)


def _ffn_body(tiles_per_seq, final_norm, h_hbm, g_ref, wa_ref, wg_ref, cw_ref, cb_ref,
              wd_ref, gfin_ref, o_ref, hbuf, hn_scr, prev_scr, sem):
    i = pl.program_id(0)
    f = pl.program_id(1)
    tm = o_ref.shape[0]

    def h_copy(tile):
        rows = pl.ds(pl.multiple_of(tile * tm, tm), tm)
        return pltpu.make_async_copy(h_hbm.at[rows], hbuf, sem)

    @pl.when(f == 0)
    def _():
        @pl.when(i == 0)
        def _():
            h_copy(0).start()

        h_copy(i).wait()
        h = hbuf[...]
        hn_scr[...] = _rmsnorm(h, g_ref[...]).astype(BF16)
        o_ref[...] = h

        @pl.when(i + 1 < pl.num_programs(0))
        def _():
            h_copy(i + 1).start()

    @pl.when(i % tiles_per_seq == 0)
    def _():
        prev_scr[f] = jnp.zeros(prev_scr.shape[1:], F32)

    hn = hn_scr[...]
    a = _dot(hn, wa_ref[...])
    conv = _causal_conv3(a, prev_scr[f], cw_ref, cb_ref)
    prev_scr[f] = a[a.shape[0] - SUBLANES:]
    act = (jax.nn.gelu(conv) * _dot(hn, wg_ref[...])).astype(BF16)
    o_ref[...] += _dot(act, wd_ref[...])

    if final_norm:
        @pl.when(f == pl.num_programs(1) - 1)
        def _():
            o_ref[...] = _rmsnorm(o_ref[...], gfin_ref[...])


def _ffn(h1, norm_g, w_up_bf, ffn_conv_w, ffn_conv_b, w_down_bf, norm_final, final_norm,
         seq_len, ff_tile=512):
    n, d = h1.shape
    d_ff = w_down_bf.shape[0]
    nff = d_ff // ff_tile
    tm = FFN_TOKEN_TILE
    tiles_per_seq = seq_len // tm
    const = lambda i, f: (0, 0)
    return pl.pallas_call(
        functools.partial(_ffn_body, tiles_per_seq, final_norm),
        grid=(n // tm, nff),
        in_specs=[
            pl.BlockSpec(memory_space=pl.ANY),
            pl.BlockSpec((1, d), const),
            pl.BlockSpec((d, ff_tile), lambda i, f: (0, f)),
            pl.BlockSpec((d, ff_tile), lambda i, f: (0, nff + f)),
            pl.BlockSpec((CONV_K, ff_tile), lambda i, f: (0, f)),
            pl.BlockSpec((1, ff_tile), lambda i, f: (0, f)),
            pl.BlockSpec((ff_tile, d), lambda i, f: (f, 0)),
            pl.BlockSpec((1, d), const),
        ],
        out_specs=pl.BlockSpec((tm, d), lambda i, f: (i, 0)),
        out_shape=jax.ShapeDtypeStruct((n, d), F32),
        scratch_shapes=[pltpu.VMEM((tm, d), F32),
                        pltpu.VMEM((tm, d), BF16),
                        pltpu.VMEM((nff, SUBLANES, ff_tile), F32),
                        pltpu.SemaphoreType.DMA(())],
        compiler_params=pltpu.CompilerParams(
            dimension_semantics=("arbitrary", "arbitrary"),
            vmem_limit_bytes=VMEM_LIMIT_BYTES),
        name="ffn",
    )(h1, norm_g, w_up_bf, w_up_bf, ffn_conv_w, ffn_conv_b, w_down_bf, norm_final)


def kernel(x, norm_tok, w_in, a_re, a_im, log_dt, b_re, b_im, c_re, c_im, d_skip, w_glu, w_ssm_out, conv_w, conv_b, w_conv_out, w_o, norm_ffn, w_up, ffn_conv_w, ffn_conv_b, w_down, norm_final):
    batch, seq_len, d = x.shape
    depth = w_in.shape[0]
    h = x.reshape(batch * seq_len, d)
    for l in range(depth):
        n_first = w_in.shape[2] - 2 * d
        w_first_bf = w_in[l, :, :n_first].astype(BF16)
        u, zb, xn, (w_merge_bf, w_glu_bf, w_ssm_bf, w_conv_bf, w_o_bf, w_down_bf) = _inproj(
            h, norm_tok[l][None], w_first_bf, conv_w[l], conv_b[l][None], seq_len,
            [(w_in[l], (n_first // (2 * d), 2 * d)), (w_glu[l], None), (w_ssm_out[l], None),
             (w_conv_out[l], None), (w_o[l], None), (w_down[l], None)])
        ops = _ssm_operators(a_re[l], a_im[l], log_dt[l], b_re[l], b_im[l],
                             c_re[l], c_im[l], d_skip[l])
        ya, (w_up_bf,) = _ssm(u, *ops, batch, seq_len, [w_up[l]])
        merged = _merge(xn, ya, zb, w_glu_bf, w_merge_bf, w_ssm_bf, w_conv_bf)
        h1 = _outproj(h, merged, w_o_bf)
        h = _ffn(h1, norm_ffn[l][None], w_up_bf, ffn_conv_w[l], ffn_conv_b[l][None],
                 w_down_bf, norm_final[None], l == depth - 1, seq_len)
    return h.reshape(batch, seq_len, d)
```

```python
import functools

import jax
import jax.numpy as jnp
from jax import lax
from jax.experimental import pallas as pl
from jax.experimental.pallas import tpu as pltpu

SSM_GROUP = 16
SSM_STATE = 64
CONV_K = 3
EPS = 1e-6

LANES = 128
SUBLANES = 8
MXU_DIM = 256
VMEM_LIMIT_BYTES = 56 * 1024 * 1024

TOKEN_TILE = 512
FFN_TOKEN_TILE = 1024
CHUNK = 16
GROUPS_PER_LANE_BLOCK = LANES // SSM_GROUP
STATE_LANES = GROUPS_PER_LANE_BLOCK * SSM_STATE

F32 = jnp.float32
BF16 = jnp.bfloat16


def _rmsnorm(x, gain):
    return x * lax.rsqrt(jnp.mean(x * x, axis=-1, keepdims=True) + EPS) * gain


def _dot(a, b):
    return jnp.dot(a, b, preferred_element_type=F32)


def _causal_conv3(a, prev, w_ref, b_ref):
    w0 = w_ref[0:1, :]
    w1 = w_ref[1:2, :]
    w2 = w_ref[2:3, :]
    bias = b_ref[...]
    full = w2 * a + w1 * pltpu.roll(a, 1, 0) + w0 * pltpu.roll(a, 2, 0) + bias
    top = a[0:SUBLANES]
    rows = lax.broadcasted_iota(jnp.int32, top.shape, 0)
    s1 = jnp.where(rows < 1, pltpu.roll(prev, 1, 0), pltpu.roll(top, 1, 0))
    s2 = jnp.where(rows < 2, pltpu.roll(prev, 2, 0), pltpu.roll(top, 2, 0))
    fix = w2 * top + w1 * s1 + w0 * s2 + bias
    return jnp.concatenate([fix, full[SUBLANES:]], axis=0)


def _cast_riders(src_refs, dst_refs):
    for src, dst in zip(src_refs, dst_refs):
        dst[...] = src[...].astype(BF16)


def _inproj_body(tiles_per_seq, n_cast, x_ref, g_ref, wu_ref, wv_ref, wgb_ref, wgc_ref,
                 cw_ref, cb_ref, *rest):
    cast_src = rest[:n_cast]
    u_ref, zb_ref, xn_ref = rest[n_cast:n_cast + 3]
    cast_dst = rest[n_cast + 3:2 * n_cast + 3]
    prev_ref = rest[2 * n_cast + 3]
    i = pl.program_id(0)

    @pl.when(i % tiles_per_seq == 0)
    def _():
        prev_ref[...] = jnp.zeros_like(prev_ref)

    xn = _rmsnorm(x_ref[...], g_ref[...]).astype(BF16)
    xn_ref[...] = xn
    u_ref[...] = _dot(xn, wu_ref[...])
    cin = _dot(xn, wgc_ref[...]) * _dot(xn, wv_ref[...])
    conv = _causal_conv3(cin, prev_ref[...], cw_ref, cb_ref)
    prev_ref[...] = cin[cin.shape[0] - SUBLANES:]
    zb_ref[...] = (_dot(xn, wgb_ref[...]) * conv).astype(BF16)
    _cast_riders(cast_src, cast_dst)


def _inproj(x2, norm_g, w_in_bf, conv_w, conv_b, seq_len, riders):
    n, d = x2.shape
    cw = conv_w.shape[1]
    steps = n // TOKEN_TILE
    tiles_per_seq = seq_len // TOKEN_TILE
    const = lambda i: (0, 0)
    wspec = lambda col: pl.BlockSpec((d, cw), lambda i, col=col: (0, col),
                                     pipeline_mode=pl.Buffered(1))
    cast_in, cast_out, cast_shape = [], [], []
    for w, cols in riders:
        rows = w.shape[0] // steps
        width = w.shape[1] if cols is None else cols[1]
        col = 0 if cols is None else cols[0]
        cast_in.append(pl.BlockSpec((rows, width), lambda i, col=col: (i, col)))
        cast_out.append(pl.BlockSpec((rows, width), lambda i: (i, 0)))
        cast_shape.append(jax.ShapeDtypeStruct((w.shape[0], width), BF16))
    outs = pl.pallas_call(
        functools.partial(_inproj_body, tiles_per_seq, len(riders)),
        grid=(steps,),
        in_specs=[
            pl.BlockSpec((TOKEN_TILE, d), lambda i: (i, 0)),
            pl.BlockSpec((1, d), const),
            wspec(0), wspec(1), wspec(2), wspec(3),
            pl.BlockSpec((CONV_K, cw), const),
            pl.BlockSpec((1, cw), const),
        ] + cast_in,
        out_specs=[
            pl.BlockSpec((TOKEN_TILE, cw), lambda i: (i, 0)),
            pl.BlockSpec((TOKEN_TILE, cw), lambda i: (i, 0)),
            pl.BlockSpec((TOKEN_TILE, d), lambda i: (i, 0)),
        ] + cast_out,
        out_shape=[
            jax.ShapeDtypeStruct((n, cw), F32),
            jax.ShapeDtypeStruct((n, cw), BF16),
            jax.ShapeDtypeStruct((n, d), BF16),
        ] + cast_shape,
        scratch_shapes=[pltpu.VMEM((SUBLANES, cw), F32)],
        compiler_params=pltpu.CompilerParams(
            dimension_semantics=("arbitrary",), vmem_limit_bytes=VMEM_LIMIT_BYTES),
        name="inproj",
    )(x2, norm_g, w_in_bf, w_in_bf, w_in_bf, w_in_bf, conv_w, conv_b, *[w for w, _ in riders])
    return outs[0], outs[1], outs[2], outs[3:]


def _ssm_body(n_cast, u_ref, tz_ref, p_ref, q_ref, pw_ref, d_ref, *rest):
    cast_src = rest[:n_cast]
    y_ref = rest[n_cast]
    cast_dst = rest[n_cast + 1:2 * n_cast + 1]
    s_scr, xin_scr = rest[2 * n_cast + 1:]
    _cast_riders(cast_src, cast_dst)
    nseq = u_ref.shape[0]
    m = u_ref.shape[1] // CHUNK
    sl = STATE_LANES
    ngl = GROUPS_PER_LANE_BLOCK

    yts, pieces = [], []
    for s in range(nseq):
        pieces.append([u_ref[s, pl.ds(t, m, stride=CHUNK), :] for t in range(CHUNK)])
        pts = [p.T.astype(BF16) for p in pieces[s]]
        ys, s_re, s_im = [], [], []
        for g in range(ngl):
            xg = jnp.concatenate([pt[g * SSM_GROUP:(g + 1) * SSM_GROUP] for pt in pts], axis=0)
            ys.append(_dot(tz_ref[g], xg))
            sg = _dot(p_ref[g], xg)
            s_re.append(sg[:SSM_STATE])
            s_im.append(sg[SSM_STATE:])
        yts.append(ys)
        s_scr[s] = jnp.concatenate(s_re + s_im, axis=0).T

    a16r = pw_ref[0:1, :]
    a16i = pw_ref[1:2, :]

    def step(k, carry):
        out = []
        for s in range(nseq):
            xr, xi = carry[2 * s], carry[2 * s + 1]
            xin_scr[s, pl.ds(k, 1), :sl] = xr
            xin_scr[s, pl.ds(k, 1), sl:] = xi
            row = s_scr[s, pl.ds(k, 1), :]
            out += [a16r * xr - a16i * xi + row[:, :sl], a16r * xi + a16i * xr + row[:, sl:]]
        return tuple(out)

    zero = jnp.zeros((1, sl), F32)
    lax.fori_loop(0, m, step, (zero,) * (2 * nseq), unroll=8)

    d = d_ref[...]
    for s in range(nseq):
        xint = xin_scr[s].T
        ys = yts[s]
        for g in range(ngl):
            xg = jnp.concatenate([xint[g * SSM_STATE:(g + 1) * SSM_STATE],
                                  xint[sl + g * SSM_STATE:sl + (g + 1) * SSM_STATE]], axis=0)
            ys[g] = ys[g] + _dot(q_ref[g], xg.astype(BF16))
        for t in range(CHUNK):
            yt = jnp.concatenate([y[t * SSM_GROUP:(t + 1) * SSM_GROUP] for y in ys], axis=0)
            y_ref[s, pl.ds(t, m, stride=CHUNK), :] = jax.nn.gelu(yt.T + d * pieces[s][t])


def _ssm(u, tz, pm, qm, pw, dsk, batch, seq_len, riders, seqs_per_step=2):
    n, width = u.shape
    nblk = width // LANES
    m = seq_len // CHUNK
    ngl = GROUPS_PER_LANE_BLOCK
    u3 = u.reshape(batch, seq_len, width)
    nb = batch // seqs_per_step
    cast_specs = [pl.BlockSpec((w.shape[0] // (nblk * nb), w.shape[1]),
                               lambda c, b: (c * nb + b, 0)) for w in riders]
    outs = pl.pallas_call(
        functools.partial(_ssm_body, len(riders)),
        grid=(nblk, nb),
        in_specs=[
            pl.BlockSpec((seqs_per_step, seq_len, LANES), lambda c, b: (b, 0, c)),
            pl.BlockSpec((ngl, MXU_DIM, MXU_DIM), lambda c, b: (c, 0, 0)),
            pl.BlockSpec((ngl, 2 * SSM_STATE, MXU_DIM), lambda c, b: (c, 0, 0)),
            pl.BlockSpec((ngl, MXU_DIM, 2 * SSM_STATE), lambda c, b: (c, 0, 0)),
            pl.BlockSpec((None, 2, STATE_LANES), lambda c, b: (c, 0, 0)),
            pl.BlockSpec((None, 1, LANES), lambda c, b: (c, 0, 0)),
        ] + cast_specs,
        out_specs=[pl.BlockSpec((seqs_per_step, seq_len, LANES), lambda c, b: (b, 0, c))]
        + cast_specs,
        out_shape=[jax.ShapeDtypeStruct((batch, seq_len, width), F32)]
        + [jax.ShapeDtypeStruct(w.shape, BF16) for w in riders],
        scratch_shapes=[pltpu.VMEM((seqs_per_step, m, 2 * STATE_LANES), F32),
                        pltpu.VMEM((seqs_per_step, m, 2 * STATE_LANES), F32)],
        compiler_params=pltpu.CompilerParams(
            dimension_semantics=("arbitrary", "arbitrary"),
            vmem_limit_bytes=VMEM_LIMIT_BYTES),
        name="ssm",
    )(u3, tz, pm, qm, pw, dsk, *riders)
    return outs[0].reshape(n, width), outs[1:]


def _ssm_operators(a_re, a_im, log_dt, b_re, b_im, c_re, c_im, d_skip):
    hp = lax.Precision.HIGHEST
    g = a_re.shape[0]
    nblk = g // GROUPS_PER_LANE_BLOCK
    dt = jnp.exp(log_dt)[:, None]
    mag = jnp.exp(dt * a_re)
    abar_re = mag * jnp.cos(dt * a_im)
    abar_im = mag * jnp.sin(dt * a_im)
    nr = abar_re - 1.0
    ni = abar_im
    den = a_re * a_re + a_im * a_im
    fr = (nr * a_re + ni * a_im) / den
    fi = (ni * a_re - nr * a_im) / den
    bb_re = fr[..., None] * b_re - fi[..., None] * b_im
    bb_im = fr[..., None] * b_im + fi[..., None] * b_re
    nn = jnp.arange(CHUNK + 1, dtype=F32)
    pmag = jnp.exp(nn * (dt * a_re)[..., None])
    pw_re = pmag * jnp.cos(nn * (dt * a_im)[..., None])
    pw_im = pmag * jnp.sin(nn * (dt * a_im)[..., None])

    col = jnp.arange(MXU_DIM)
    pick = (CHUNK - 1 - col[None, :] // SSM_GROUP == jnp.arange(CHUNK)[:, None]).astype(F32)
    e_re = jnp.einsum("gpn,nc->gpc", pw_re[..., :CHUNK], pick, precision=hp)
    e_im = jnp.einsum("gpn,nc->gpc", pw_im[..., :CHUNK], pick, precision=hp)
    spread = (col[None, :] % SSM_GROUP == jnp.arange(SSM_GROUP)[:, None]).astype(F32)
    bx_re = jnp.einsum("gpi,ic->gpc", bb_re, spread, precision=hp)
    bx_im = jnp.einsum("gpi,ic->gpc", bb_im, spread, precision=hp)
    p_re = e_re * bx_re - e_im * bx_im
    p_im = e_re * bx_im + e_im * bx_re
    pm = jnp.concatenate([p_re, p_im], axis=1).astype(BF16)

    krev = (jnp.einsum("gop,gpc->goc", c_re, p_re, precision=hp)
            - jnp.einsum("gop,gpc->goc", c_im, p_im, precision=hp))
    padded = jnp.concatenate([krev, jnp.zeros_like(krev)], axis=-1)
    tz = jnp.stack([padded[..., (CHUNK - 1 - t) * SSM_GROUP:(CHUNK - 1 - t) * SSM_GROUP + MXU_DIM]
                    for t in range(CHUNK)], axis=1)
    tz = tz.reshape(g, MXU_DIM, MXU_DIM).astype(BF16)

    o_re = pw_re[..., 1:].transpose(0, 2, 1)[:, :, None, :]
    o_im = pw_im[..., 1:].transpose(0, 2, 1)[:, :, None, :]
    cr = c_re[:, None]
    ci = c_im[:, None]
    qm = jnp.concatenate([cr * o_re - ci * o_im, -(cr * o_im + ci * o_re)], axis=-1)
    qm = qm.reshape(g, MXU_DIM, 2 * SSM_STATE).astype(BF16)

    pw = jnp.stack([pw_re[..., CHUNK], pw_im[..., CHUNK]])
    pw = pw.reshape(2, nblk, STATE_LANES).transpose(1, 0, 2)
    dsk = d_skip.reshape(nblk, 1, LANES)
    return tz, pm, qm, pw, dsk


def _merge_body(col_tile, xn_ref, ya_ref, zb_ref, wglu_ref, wm_ref, wssm_ref, wconv_ref, m_ref):
    d = m_ref.shape[1]
    ya = ya_ref[...]
    ya2 = (ya * jax.nn.sigmoid(_dot(ya.astype(BF16), wglu_ref[...]))).astype(BF16)
    xn = xn_ref[...]
    zb = zb_ref[...]
    for c in range(d // col_tile):
        cols = slice(c * col_tile, (c + 1) * col_tile)
        gate_a = jax.nn.sigmoid(_dot(xn, wm_ref[:, cols]))
        gate_b = jax.nn.sigmoid(_dot(xn, wm_ref[:, d + c * col_tile:d + (c + 1) * col_tile]))
        m_ref[:, cols] = (gate_a * _dot(ya2, wssm_ref[:, cols])
                          + gate_b * _dot(zb, wconv_ref[:, cols])).astype(BF16)


def _merge(xn, ya, zb, w_glu_bf, w_merge_bf, w_ssm_bf, w_conv_bf, col_tile=512):
    n, d = xn.shape
    sw = ya.shape[1]
    cw = zb.shape[1]
    resident = lambda shape: pl.BlockSpec(shape, lambda i: (0, 0), pipeline_mode=pl.Buffered(1))
    return pl.pallas_call(
        functools.partial(_merge_body, col_tile),
        grid=(n // TOKEN_TILE,),
        in_specs=[
            pl.BlockSpec((TOKEN_TILE, d), lambda i: (i, 0)),
            pl.BlockSpec((TOKEN_TILE, sw), lambda i: (i, 0)),
            pl.BlockSpec((TOKEN_TILE, cw), lambda i: (i, 0)),
            resident((sw, sw)),
            resident((d, 2 * d)),
            resident((sw, d)),
            resident((cw, d)),
        ],
        out_specs=pl.BlockSpec((TOKEN_TILE, d), lambda i: (i, 0)),
        out_shape=jax.ShapeDtypeStruct((n, d), BF16),
        compiler_params=pltpu.CompilerParams(
            dimension_semantics=("arbitrary",), vmem_limit_bytes=VMEM_LIMIT_BYTES),
        name="merge",
    )(xn, ya, zb, w_glu_bf, w_merge_bf, w_ssm_bf, w_conv_bf)


def _outproj_body(col_tile, x_ref, m_ref, wo_ref, h_ref):
    m = m_ref[...]
    for c in range(h_ref.shape[1] // col_tile):
        cols = slice(c * col_tile, (c + 1) * col_tile)
        h_ref[:, cols] = x_ref[:, cols] + _dot(m, wo_ref[:, cols])


def _outproj(x2, merged, w_o_bf, col_tile=512):
    n, d = x2.shape
    tm = FFN_TOKEN_TILE
    return pl.pallas_call(
        functools.partial(_outproj_body, col_tile),
        grid=(n // tm,),
        in_specs=[
            pl.BlockSpec((tm, d), lambda i: (i, 0)),
            pl.BlockSpec((tm, d), lambda i: (i, 0)),
            pl.BlockSpec((d, d), lambda i: (0, 0), pipeline_mode=pl.Buffered(1)),
        ],
        out_specs=pl.BlockSpec((tm, d), lambda i: (i, 0)),
        out_shape=jax.ShapeDtypeStruct((n, d), F32),
        compiler_params=pltpu.CompilerParams(
            dimension_semantics=("arbitrary",), vmem_limit_bytes=VMEM_LIMIT_BYTES),
        name="outproj",
    )(x2, merged, w_o_bf)


def _ffn_body(tiles_per_seq, final_norm, h_hbm, g_ref, wa_ref, wg_ref, cw_ref, cb_ref,
              wd_ref, gfin_ref, o_ref, hbuf, hn_scr, prev_scr, sem):
    i = pl.program_id(0)
    f = pl.program_id(1)
    tm = o_ref.shape[0]

    def h_copy(tile):
        rows = pl.ds(pl.multiple_of(tile * tm, tm), tm)
        return pltpu.make_async_copy(h_hbm.at[rows], hbuf, sem)

    @pl.when(f == 0)
    def _():
        @pl.when(i == 0)
        def _():
            h_copy(0).start()

        h_copy(i).wait()
        h = hbuf[...]
        hn_scr[...] = _rmsnorm(h, g_ref[...]).astype(BF16)
        o_ref[...] = h

        @pl.when(i + 1 < pl.num_programs(0))
        def _():
            h_copy(i + 1).start()

    @pl.when(i % tiles_per_seq == 0)
    def _():
        prev_scr[f] = jnp.zeros(prev_scr.shape[1:], F32)

    hn = hn_scr[...]
    a = _dot(hn, wa_ref[...])
    conv = _causal_conv3(a, prev_scr[f], cw_ref, cb_ref)
    prev_scr[f] = a[a.shape[0] - SUBLANES:]
    act = (jax.nn.gelu(conv) * _dot(hn, wg_ref[...])).astype(BF16)
    o_ref[...] += _dot(act, wd_ref[...])

    if final_norm:
        @pl.when(f == pl.num_programs(1) - 1)
        def _():
            o_ref[...] = _rmsnorm(o_ref[...], gfin_ref[...])


def _ffn(h1, norm_g, w_up_bf, ffn_conv_w, ffn_conv_b, w_down_bf, norm_final, final_norm,
         seq_len, ff_tile=512):
    n, d = h1.shape
    d_ff = w_down_bf.shape[0]
    nff = d_ff // ff_tile
    tm = FFN_TOKEN_TILE
    tiles_per_seq = seq_len // tm
    const = lambda i, f: (0, 0)
    return pl.pallas_call(
        functools.partial(_ffn_body, tiles_per_seq, final_norm),
        grid=(n // tm, nff),
        in_specs=[
            pl.BlockSpec(memory_space=pl.ANY),
            pl.BlockSpec((1, d), const),
            pl.BlockSpec((d, ff_tile), lambda i, f: (0, f)),
            pl.BlockSpec((d, ff_tile), lambda i, f: (0, nff + f)),
            pl.BlockSpec((CONV_K, ff_tile), lambda i, f: (0, f)),
            pl.BlockSpec((1, ff_tile), lambda i, f: (0, f)),
            pl.BlockSpec((ff_tile, d), lambda i, f: (f, 0)),
            pl.BlockSpec((1, d), const),
        ],
        out_specs=pl.BlockSpec((tm, d), lambda i, f: (i, 0)),
        out_shape=jax.ShapeDtypeStruct((n, d), F32),
        scratch_shapes=[pltpu.VMEM((tm, d), F32),
                        pltpu.VMEM((tm, d), BF16),
                        pltpu.VMEM((nff, SUBLANES, ff_tile), F32),
                        pltpu.SemaphoreType.DMA(())],
        compiler_params=pltpu.CompilerParams(
            dimension_semantics=("arbitrary", "arbitrary"),
            vmem_limit_bytes=VMEM_LIMIT_BYTES),
        name="ffn",
    )(h1, norm_g, w_up_bf, w_up_bf, ffn_conv_w, ffn_conv_b, w_down_bf, norm_final)


def kernel(x, norm_tok, w_in, a_re, a_im, log_dt, b_re, b_im, c_re, c_im, d_skip, w_glu, w_ssm_out, conv_w, conv_b, w_conv_out, w_o, norm_ffn, w_up, ffn_conv_w, ffn_conv_b, w_down, norm_final):
    batch, seq_len, d = x.shape
    depth = w_in.shape[0]
    h = x.reshape(batch * seq_len, d)
    for l in range(depth):
        n_first = w_in.shape[2] - 2 * d
        w_first_bf = w_in[l, :, :n_first].astype(BF16)
        u, zb, xn, (w_merge_bf, w_glu_bf, w_ssm_bf, w_conv_bf, w_o_bf, w_down_bf) = _inproj(
            h, norm_tok[l][None], w_first_bf, conv_w[l], conv_b[l][None], seq_len,
            [(w_in[l], (n_first // (2 * d), 2 * d)), (w_glu[l], None), (w_ssm_out[l], None),
             (w_conv_out[l], None), (w_o[l], None), (w_down[l], None)])
        ops = _ssm_operators(a_re[l], a_im[l], log_dt[l], b_re[l], b_im[l],
                             c_re[l], c_im[l], d_skip[l])
        ya, (w_up_bf,) = _ssm(u, *ops, batch, seq_len, [w_up[l]])
        merged = _merge(xn, ya, zb, w_glu_bf, w_merge_bf, w_ssm_bf, w_conv_bf)
        h1 = _outproj(h, merged, w_o_bf)
        h = _ffn(h1, norm_ffn[l][None], w_up_bf, ffn_conv_w[l], ffn_conv_b[l][None],
                 w_down_bf, norm_final[None], l == depth - 1, seq_len)
    return h.reshape(batch, seq_len, d)
```

```python
import functools

import jax
import jax.numpy as jnp
from jax import lax
from jax.experimental import pallas as pl
from jax.experimental.pallas import tpu as pltpu

SSM_GROUP = 16
SSM_STATE = 64
CONV_K = 3
EPS = 1e-6

LANES = 128
SUBLANES = 8
MXU_DIM = 256
VMEM_LIMIT_BYTES = 56 * 1024 * 1024

TOKEN_TILE = 512
FFN_TOKEN_TILE = 1024
CHUNK = 16
GROUPS_PER_LANE_BLOCK = LANES // SSM_GROUP
STATE_LANES = GROUPS_PER_LANE_BLOCK * SSM_STATE

F32 = jnp.float32
BF16 = jnp.bfloat16


def _rmsnorm(x, gain):
    return x * lax.rsqrt(jnp.mean(x * x, axis=-1, keepdims=True) + EPS) * gain


def _dot(a, b):
    return jnp.dot(a, b, preferred_element_type=F32)


def _causal_conv3(a, prev, w_ref, b_ref):
    w0 = w_ref[0:1, :]
    w1 = w_ref[1:2, :]
    w2 = w_ref[2:3, :]
    bias = b_ref[...]
    full = w2 * a + w1 * pltpu.roll(a, 1, 0) + w0 * pltpu.roll(a, 2, 0) + bias
    top = a[0:SUBLANES]
    rows = lax.broadcasted_iota(jnp.int32, top.shape, 0)
    s1 = jnp.where(rows < 1, pltpu.roll(prev, 1, 0), pltpu.roll(top, 1, 0))
    s2 = jnp.where(rows < 2, pltpu.roll(prev, 2, 0), pltpu.roll(top, 2, 0))
    fix = w2 * top + w1 * s1 + w0 * s2 + bias
    return jnp.concatenate([fix, full[SUBLANES:]], axis=0)


def _cast_riders(src_refs, dst_refs):
    for src, dst in zip(src_refs, dst_refs):
        dst[...] = src[...].astype(BF16)


def _inproj_body(tiles_per_seq, n_cast, x_ref, g_ref, wu_ref, wv_ref, wgb_ref, wgc_ref,
                 cw_ref, cb_ref, *rest):
    cast_src = rest[:n_cast]
    u_ref, zb_ref, xn_ref = rest[n_cast:n_cast + 3]
    cast_dst = rest[n_cast + 3:2 * n_cast + 3]
    prev_ref = rest[2 * n_cast + 3]
    i = pl.program_id(0)

    @pl.when(i % tiles_per_seq == 0)
    def _():
        prev_ref[...] = jnp.zeros_like(prev_ref)

    xn = _rmsnorm(x_ref[...], g_ref[...]).astype(BF16)
    xn_ref[...] = xn
    u_ref[...] = _dot(xn, wu_ref[...])
    cin = _dot(xn, wgc_ref[...]) * _dot(xn, wv_ref[...])
    conv = _causal_conv3(cin, prev_ref[...], cw_ref, cb_ref)
    prev_ref[...] = cin[cin.shape[0] - SUBLANES:]
    zb_ref[...] = (_dot(xn, wgb_ref[...]) * conv).astype(BF16)
    _cast_riders(cast_src, cast_dst)


def _inproj(x2, norm_g, w_in_bf, conv_w, conv_b, seq_len, riders):
    n, d = x2.shape
    cw = conv_w.shape[1]
    steps = n // TOKEN_TILE
    tiles_per_seq = seq_len // TOKEN_TILE
    const = lambda i: (0, 0)
    wspec = lambda col: pl.BlockSpec((d, cw), lambda i, col=col: (0, col),
                                     pipeline_mode=pl.Buffered(1))
    cast_in, cast_out, cast_shape = [], [], []
    for w, cols in riders:
        rows = w.shape[0] // steps
        width = w.shape[1] if cols is None else cols[1]
        col = 0 if cols is None else cols[0]
        cast_in.append(pl.BlockSpec((rows, width), lambda i, col=col: (i, col)))
        cast_out.append(pl.BlockSpec((rows, width), lambda i: (i, 0)))
        cast_shape.append(jax.ShapeDtypeStruct((w.shape[0], width), BF16))
    outs = pl.pallas_call(
        functools.partial(_inproj_body, tiles_per_seq, len(riders)),
        grid=(steps,),
        in_specs=[
            pl.BlockSpec((TOKEN_TILE, d), lambda i: (i, 0)),
            pl.BlockSpec((1, d), const),
            wspec(0), wspec(1), wspec(2), wspec(3),
            pl.BlockSpec((CONV_K, cw), const),
            pl.BlockSpec((1, cw), const),
        ] + cast_in,
        out_specs=[
            pl.BlockSpec((TOKEN_TILE, cw), lambda i: (i, 0)),
            pl.BlockSpec((TOKEN_TILE, cw), lambda i: (i, 0)),
            pl.BlockSpec((TOKEN_TILE, d), lambda i: (i, 0)),
        ] + cast_out,
        out_shape=[
            jax.ShapeDtypeStruct((n, cw), F32),
            jax.ShapeDtypeStruct((n, cw), BF16),
            jax.ShapeDtypeStruct((n, d), BF16),
        ] + cast_shape,
        scratch_shapes=[pltpu.VMEM((SUBLANES, cw), F32)],
        compiler_params=pltpu.CompilerParams(
            dimension_semantics=("arbitrary",), vmem_limit_bytes=VMEM_LIMIT_BYTES),
        name="inproj",
    )(x2, norm_g, w_in_bf, w_in_bf, w_in_bf, w_in_bf, conv_w, conv_b, *[w for w, _ in riders])
    return outs[0], outs[1], outs[2], outs[3:]


def _ssm_body(n_cast, u_ref, tz_ref, p_ref, q_ref, pw_ref, *rest):
    cast_src = rest[:n_cast]
    y_ref = rest[n_cast]
    cast_dst = rest[n_cast + 1:2 * n_cast + 1]
    s_scr, xin_scr = rest[2 * n_cast + 1:]
    _cast_riders(cast_src, cast_dst)
    nseq = u_ref.shape[0]
    m = u_ref.shape[1] // CHUNK
    sl = STATE_LANES
    ngl = GROUPS_PER_LANE_BLOCK

    yts = []
    for s in range(nseq):
        pts = [u_ref[s, pl.ds(t, m, stride=CHUNK), :].T.astype(BF16)
               for t in range(CHUNK)]
        ys, s_re, s_im = [], [], []
        for g in range(ngl):
            xg = jnp.concatenate([pt[g * SSM_GROUP:(g + 1) * SSM_GROUP] for pt in pts], axis=0)
            ys.append(_dot(tz_ref[g], xg))
            sg = _dot(p_ref[g], xg)
            s_re.append(sg[:SSM_STATE])
            s_im.append(sg[SSM_STATE:])
        yts.append(ys)
        s_scr[s] = jnp.concatenate(s_re + s_im, axis=0).T

    a16r = pw_ref[0:1, :]
    a16i = pw_ref[1:2, :]

    def step(k, carry):
        out = []
        for s in range(nseq):
            xr, xi = carry[2 * s], carry[2 * s + 1]
            xin_scr[s, pl.ds(k, 1), :sl] = xr
            xin_scr[s, pl.ds(k, 1), sl:] = xi
            row = s_scr[s, pl.ds(k, 1), :]
            out += [a16r * xr - a16i * xi + row[:, :sl], a16r * xi + a16i * xr + row[:, sl:]]
        return tuple(out)

    zero = jnp.zeros((1, sl), F32)
    lax.fori_loop(0, m, step, (zero,) * (2 * nseq), unroll=8)

    for s in range(nseq):
        xint = xin_scr[s].T
        ys = yts[s]
        for g in range(ngl):
            xg = jnp.concatenate([xint[g * SSM_STATE:(g + 1) * SSM_STATE],
                                  xint[sl + g * SSM_STATE:sl + (g + 1) * SSM_STATE]], axis=0)
            ys[g] = ys[g] + _dot(q_ref[g], xg.astype(BF16))
        for t in range(CHUNK):
            yt = jnp.concatenate([y[t * SSM_GROUP:(t + 1) * SSM_GROUP] for y in ys], axis=0)
            y_ref[s, pl.ds(t, m, stride=CHUNK), :] = jax.nn.gelu(yt.T)


def _ssm(u, tz, pm, qm, pw, batch, seq_len, riders, seqs_per_step=2):
    n, width = u.shape
    nblk = width // LANES
    m = seq_len // CHUNK
    ngl = GROUPS_PER_LANE_BLOCK
    u3 = u.reshape(batch, seq_len, width)
    nb = batch // seqs_per_step
    cast_specs = [pl.BlockSpec((w.shape[0] // (nblk * nb), w.shape[1]),
                               lambda c, b: (c * nb + b, 0)) for w in riders]
    outs = pl.pallas_call(
        functools.partial(_ssm_body, len(riders)),
        grid=(nblk, nb),
        in_specs=[
            pl.BlockSpec((seqs_per_step, seq_len, LANES), lambda c, b: (b, 0, c)),
            pl.BlockSpec((ngl, MXU_DIM, MXU_DIM), lambda c, b: (c, 0, 0)),
            pl.BlockSpec((ngl, 2 * SSM_STATE, MXU_DIM), lambda c, b: (c, 0, 0)),
            pl.BlockSpec((ngl, MXU_DIM, 2 * SSM_STATE), lambda c, b: (c, 0, 0)),
            pl.BlockSpec((None, 2, STATE_LANES), lambda c, b: (c, 0, 0)),
        ] + cast_specs,
        out_specs=[pl.BlockSpec((seqs_per_step, seq_len, LANES), lambda c, b: (b, 0, c))]
        + cast_specs,
        out_shape=[jax.ShapeDtypeStruct((batch, seq_len, width), F32)]
        + [jax.ShapeDtypeStruct(w.shape, BF16) for w in riders],
        scratch_shapes=[pltpu.VMEM((seqs_per_step, m, 2 * STATE_LANES), F32),
                        pltpu.VMEM((seqs_per_step, m, 2 * STATE_LANES), F32)],
        compiler_params=pltpu.CompilerParams(
            dimension_semantics=("arbitrary", "arbitrary"),
            vmem_limit_bytes=VMEM_LIMIT_BYTES),
        name="ssm",
    )(u3, tz, pm, qm, pw, *riders)
    return outs[0].reshape(n, width), outs[1:]


def _ssm_operators(a_re, a_im, log_dt, b_re, b_im, c_re, c_im, d_skip):
    hp = lax.Precision.HIGHEST
    g = a_re.shape[0]
    nblk = g // GROUPS_PER_LANE_BLOCK
    dt = jnp.exp(log_dt)[:, None]
    mag = jnp.exp(dt * a_re)
    abar_re = mag * jnp.cos(dt * a_im)
    abar_im = mag * jnp.sin(dt * a_im)
    nr = abar_re - 1.0
    ni = abar_im
    den = a_re * a_re + a_im * a_im
    fr = (nr * a_re + ni * a_im) / den
    fi = (ni * a_re - nr * a_im) / den
    bb_re = fr[..., None] * b_re - fi[..., None] * b_im
    bb_im = fr[..., None] * b_im + fi[..., None] * b_re
    nn = jnp.arange(CHUNK + 1, dtype=F32)
    pmag = jnp.exp(nn * (dt * a_re)[..., None])
    pw_re = pmag * jnp.cos(nn * (dt * a_im)[..., None])
    pw_im = pmag * jnp.sin(nn * (dt * a_im)[..., None])

    col = jnp.arange(MXU_DIM)
    pick = (CHUNK - 1 - col[None, :] // SSM_GROUP == jnp.arange(CHUNK)[:, None]).astype(F32)
    e_re = jnp.einsum("gpn,nc->gpc", pw_re[..., :CHUNK], pick, precision=hp)
    e_im = jnp.einsum("gpn,nc->gpc", pw_im[..., :CHUNK], pick, precision=hp)
    spread = (col[None, :] % SSM_GROUP == jnp.arange(SSM_GROUP)[:, None]).astype(F32)
    bx_re = jnp.einsum("gpi,ic->gpc", bb_re, spread, precision=hp)
    bx_im = jnp.einsum("gpi,ic->gpc", bb_im, spread, precision=hp)
    p_re = e_re * bx_re - e_im * bx_im
    p_im = e_re * bx_im + e_im * bx_re
    pm = jnp.concatenate([p_re, p_im], axis=1).astype(BF16)

    krev = (jnp.einsum("gop,gpc->goc", c_re, p_re, precision=hp)
            - jnp.einsum("gop,gpc->goc", c_im, p_im, precision=hp))
    lag0 = (CHUNK - 1) * SSM_GROUP + jnp.arange(SSM_GROUP)
    krev = krev + d_skip[..., None] * (col[None, :] == lag0[:, None]).astype(F32)
    padded = jnp.concatenate([krev, jnp.zeros_like(krev)], axis=-1)
    tz = jnp.stack([padded[..., (CHUNK - 1 - t) * SSM_GROUP:(CHUNK - 1 - t) * SSM_GROUP + MXU_DIM]
                    for t in range(CHUNK)], axis=1)
    tz = tz.reshape(g, MXU_DIM, MXU_DIM).astype(BF16)

    o_re = pw_re[..., 1:].transpose(0, 2, 1)[:, :, None, :]
    o_im = pw_im[..., 1:].transpose(0, 2, 1)[:, :, None, :]
    cr = c_re[:, None]
    ci = c_im[:, None]
    qm = jnp.concatenate([cr * o_re - ci * o_im, -(cr * o_im + ci * o_re)], axis=-1)
    qm = qm.reshape(g, MXU_DIM, 2 * SSM_STATE).astype(BF16)

    pw = jnp.stack([pw_re[..., CHUNK], pw_im[..., CHUNK]])
    pw = pw.reshape(2, nblk, STATE_LANES).transpose(1, 0, 2)
    return tz, pm, qm, pw


def _merge_body(col_tile, xn_ref, ya_ref, zb_ref, wglu_ref, wm_ref, wssm_ref, wconv_ref, m_ref):
    d = m_ref.shape[1]
    ya = ya_ref[...]
    ya2 = (ya * jax.nn.sigmoid(_dot(ya.astype(BF16), wglu_ref[...]))).astype(BF16)
    xn = xn_ref[...]
    zb = zb_ref[...]
    for c in range(d // col_tile):
        cols = slice(c * col_tile, (c + 1) * col_tile)
        gate_a = jax.nn.sigmoid(_dot(xn, wm_ref[:, cols]))
        gate_b = jax.nn.sigmoid(_dot(xn, wm_ref[:, d + c * col_tile:d + (c + 1) * col_tile]))
        m_ref[:, cols] = (gate_a * _dot(ya2, wssm_ref[:, cols])
                          + gate_b * _dot(zb, wconv_ref[:, cols])).astype(BF16)


def _merge(xn, ya, zb, w_glu_bf, w_merge_bf, w_ssm_bf, w_conv_bf, col_tile=512):
    n, d = xn.shape
    sw = ya.shape[1]
    cw = zb.shape[1]
    resident = lambda shape: pl.BlockSpec(shape, lambda i: (0, 0), pipeline_mode=pl.Buffered(1))
    return pl.pallas_call(
        functools.partial(_merge_body, col_tile),
        grid=(n // TOKEN_TILE,),
        in_specs=[
            pl.BlockSpec((TOKEN_TILE, d), lambda i: (i, 0)),
            pl.BlockSpec((TOKEN_TILE, sw), lambda i: (i, 0)),
            pl.BlockSpec((TOKEN_TILE, cw), lambda i: (i, 0)),
            resident((sw, sw)),
            resident((d, 2 * d)),
            resident((sw, d)),
            resident((cw, d)),
        ],
        out_specs=pl.BlockSpec((TOKEN_TILE, d), lambda i: (i, 0)),
        out_shape=jax.ShapeDtypeStruct((n, d), BF16),
        compiler_params=pltpu.CompilerParams(
            dimension_semantics=("arbitrary",), vmem_limit_bytes=VMEM_LIMIT_BYTES),
        name="merge",
    )(xn, ya, zb, w_glu_bf, w_merge_bf, w_ssm_bf, w_conv_bf)


def _outproj_body(col_tile, x_ref, m_ref, wo_ref, h_ref):
    m = m_ref[...]
    for c in range(h_ref.shape[1] // col_tile):
        cols = slice(c * col_tile, (c + 1) * col_tile)
        h_ref[:, cols] = x_ref[:, cols] + _dot(m, wo_ref[:, cols])


def _outproj(x2, merged, w_o_bf, col_tile=512):
    n, d = x2.shape
    tm = FFN_TOKEN_TILE
    return pl.pallas_call(
        functools.partial(_outproj_body, col_tile),
        grid=(n // tm,),
        in_specs=[
            pl.BlockSpec((tm, d), lambda i: (i, 0)),
            pl.BlockSpec((tm, d), lambda i: (i, 0)),
            pl.BlockSpec((d, d), lambda i: (0, 0), pipeline_mode=pl.Buffered(1)),
        ],
        out_specs=pl.BlockSpec((tm, d), lambda i: (i, 0)),
        out_shape=jax.ShapeDtypeStruct((n, d), F32),
        compiler_params=pltpu.CompilerParams(
            dimension_semantics=("arbitrary",), vmem_limit_bytes=VMEM_LIMIT_BYTES),
        name="outproj",
    )(x2, merged, w_o_bf)


def _ffn_body(tiles_per_seq, final_norm, h_hbm, g_ref, wa_ref, wg_ref, cw_ref, cb_ref,
              wd_ref, gfin_ref, o_ref, hbuf, hn_scr, prev_scr, sem):
    i = pl.program_id(0)
    f = pl.program_id(1)
    tm = o_ref.shape[0]

    def h_copy(tile):
        rows = pl.ds(pl.multiple_of(tile * tm, tm), tm)
        return pltpu.make_async_copy(h_hbm.at[rows], hbuf, sem)

    @pl.when(f == 0)
    def _():
        @pl.when(i == 0)
        def _():
            h_copy(0).start()

        h_copy(i).wait()
        h = hbuf[...]
        hn_scr[...] = _rmsnorm(h, g_ref[...]).astype(BF16)
        o_ref[...] = h

        @pl.when(i + 1 < pl.num_programs(0))
        def _():
            h_copy(i + 1).start()

    @pl.when(i % tiles_per_seq == 0)
    def _():
        prev_scr[f] = jnp.zeros(prev_scr.shape[1:], F32)

    hn = hn_scr[...]
    a = _dot(hn, wa_ref[...])
    conv = _causal_conv3(a, prev_scr[f], cw_ref, cb_ref)
    prev_scr[f] = a[a.shape[0] - SUBLANES:]
    act = (jax.nn.gelu(conv) * _dot(hn, wg_ref[...])).astype(BF16)
    o_ref[...] += _dot(act, wd_ref[...])

    if final_norm:
        @pl.when(f == pl.num_programs(1) - 1)
        def _():
            o_ref[...] = _rmsnorm(o_ref[...], gfin_ref[...])


def _ffn(h1, norm_g, w_up_bf, ffn_conv_w, ffn_conv_b, w_down_bf, norm_final, final_norm,
         seq_len, ff_tile=512):
    n, d = h1.shape
    d_ff = w_down_bf.shape[0]
    nff = d_ff // ff_tile
    tm = FFN_TOKEN_TILE
    tiles_per_seq = seq_len // tm
    const = lambda i, f: (0, 0)
    return pl.pallas_call(
        functools.partial(_ffn_body, tiles_per_seq, final_norm),
        grid=(n // tm, nff),
        in_specs=[
            pl.BlockSpec(memory_space=pl.ANY),
            pl.BlockSpec((1, d), const),
            pl.BlockSpec((d, ff_tile), lambda i, f: (0, f)),
            pl.BlockSpec((d, ff_tile), lambda i, f: (0, nff + f)),
            pl.BlockSpec((CONV_K, ff_tile), lambda i, f: (0, f)),
            pl.BlockSpec((1, ff_tile), lambda i, f: (0, f)),
            pl.BlockSpec((ff_tile, d), lambda i, f: (f, 0)),
            pl.BlockSpec((1, d), const),
        ],
        out_specs=pl.BlockSpec((tm, d), lambda i, f: (i, 0)),
        out_shape=jax.ShapeDtypeStruct((n, d), F32),
        scratch_shapes=[pltpu.VMEM((tm, d), F32),
                        pltpu.VMEM((tm, d), BF16),
                        pltpu.VMEM((nff, SUBLANES, ff_tile), F32),
                        pltpu.SemaphoreType.DMA(())],
        compiler_params=pltpu.CompilerParams(
            dimension_semantics=("arbitrary", "arbitrary"),
            vmem_limit_bytes=VMEM_LIMIT_BYTES),
        name="ffn",
    )(h1, norm_g, w_up_bf, w_up_bf, ffn_conv_w, ffn_conv_b, w_down_bf, norm_final)


def kernel(x, norm_tok, w_in, a_re, a_im, log_dt, b_re, b_im, c_re, c_im, d_skip, w_glu, w_ssm_out, conv_w, conv_b, w_conv_out, w_o, norm_ffn, w_up, ffn_conv_w, ffn_conv_b, w_down, norm_final):
    batch, seq_len, d = x.shape
    depth = w_in.shape[0]
    h = x.reshape(batch * seq_len, d)
    for l in range(depth):
        n_first = w_in.shape[2] - 2 * d
        w_first_bf = w_in[l, :, :n_first].astype(BF16)
        u, zb, xn, (w_merge_bf, w_glu_bf, w_ssm_bf, w_conv_bf, w_o_bf, w_down_bf) = _inproj(
            h, norm_tok[l][None], w_first_bf, conv_w[l], conv_b[l][None], seq_len,
            [(w_in[l], (n_first // (2 * d), 2 * d)), (w_glu[l], None), (w_ssm_out[l], None),
             (w_conv_out[l], None), (w_o[l], None), (w_down[l], None)])
        ops = _ssm_operators(a_re[l], a_im[l], log_dt[l], b_re[l], b_im[l],
                             c_re[l], c_im[l], d_skip[l])
        ya, (w_up_bf,) = _ssm(u, *ops, batch, seq_len, [w_up[l]])
        merged = _merge(xn, ya, zb, w_glu_bf, w_merge_bf, w_ssm_bf, w_conv_bf)
        h1 = _outproj(h, merged, w_o_bf)
        h = _ffn(h1, norm_ffn[l][None], w_up_bf, ffn_conv_w[l], ffn_conv_b[l][None],
                 w_down_bf, norm_final[None], l == depth - 1, seq_len)
    return h.reshape(batch, seq_len, d)
```

```python
import functools

import jax
import jax.numpy as jnp
from jax import lax
from jax.experimental import pallas as pl
from jax.experimental.pallas import tpu as pltpu

SSM_GROUP = 16
SSM_STATE = 64
CONV_K = 3
EPS = 1e-6

LANES = 128
SUBLANES = 8
MXU_DIM = 256
VMEM_LIMIT_BYTES = 56 * 1024 * 1024

TOKEN_TILE = 512
FFN_TOKEN_TILE = 1024
CHUNK = 16
GROUPS_PER_LANE_BLOCK = LANES // SSM_GROUP
STATE_LANES = GROUPS_PER_LANE_BLOCK * SSM_STATE

F32 = jnp.float32
BF16 = jnp.bfloat16


def _rmsnorm(x, gain):
    return x * lax.rsqrt(jnp.mean(x * x, axis=-1, keepdims=True) + EPS) * gain


def _dot(a, b):
    return jnp.dot(a, b, preferred_element_type=F32)


def _causal_conv3(a, prev, w_ref, b_ref):
    w0 = w_ref[0:1, :]
    w1 = w_ref[1:2, :]
    w2 = w_ref[2:3, :]
    bias = b_ref[...]
    full = w2 * a + w1 * pltpu.roll(a, 1, 0) + w0 * pltpu.roll(a, 2, 0) + bias
    top = a[0:SUBLANES]
    rows = lax.broadcasted_iota(jnp.int32, top.shape, 0)
    s1 = jnp.where(rows < 1, pltpu.roll(prev, 1, 0), pltpu.roll(top, 1, 0))
    s2 = jnp.where(rows < 2, pltpu.roll(prev, 2, 0), pltpu.roll(top, 2, 0))
    fix = w2 * top + w1 * s1 + w0 * s2 + bias
    return jnp.concatenate([fix, full[SUBLANES:]], axis=0)


def _cast_riders(src_refs, dst_refs):
    for src, dst in zip(src_refs, dst_refs):
        dst[...] = src[...].astype(BF16)


def _inproj_body(tiles_per_seq, n_cast, x_ref, g_ref, wu_ref, wv_ref, wgb_ref, wgc_ref,
                 cw_ref, cb_ref, *rest):
    cast_src = rest[:n_cast]
    u_ref, zb_ref, xn_ref = rest[n_cast:n_cast + 3]
    cast_dst = rest[n_cast + 3:2 * n_cast + 3]
    prev_ref = rest[2 * n_cast + 3]
    i = pl.program_id(0)

    @pl.when(i % tiles_per_seq == 0)
    def _():
        prev_ref[...] = jnp.zeros_like(prev_ref)

    xn = _rmsnorm(x_ref[...], g_ref[...]).astype(BF16)
    xn_ref[...] = xn
    u_ref[...] = _dot(xn, wu_ref[...])
    cin = _dot(xn, wgc_ref[...]) * _dot(xn, wv_ref[...])
    conv = _causal_conv3(cin, prev_ref[...], cw_ref, cb_ref)
    prev_ref[...] = cin[cin.shape[0] - SUBLANES:]
    zb_ref[...] = (_dot(xn, wgb_ref[...]) * conv).astype(BF16)
    _cast_riders(cast_src, cast_dst)


def _inproj(x2, norm_g, w_in_bf, conv_w, conv_b, seq_len, riders):
    n, d = x2.shape
    cw = conv_w.shape[1]
    steps = n // TOKEN_TILE
    tiles_per_seq = seq_len // TOKEN_TILE
    const = lambda i: (0, 0)
    wspec = lambda col: pl.BlockSpec((d, cw), lambda i, col=col: (0, col),
                                     pipeline_mode=pl.Buffered(1))
    cast_in, cast_out, cast_shape = [], [], []
    for w, cols in riders:
        rows = w.shape[0] // steps
        width = w.shape[1] if cols is None else cols[1]
        col = 0 if cols is None else cols[0]
        cast_in.append(pl.BlockSpec((rows, width), lambda i, col=col: (i, col)))
        cast_out.append(pl.BlockSpec((rows, width), lambda i: (i, 0)))
        cast_shape.append(jax.ShapeDtypeStruct((w.shape[0], width), BF16))
    outs = pl.pallas_call(
        functools.partial(_inproj_body, tiles_per_seq, len(riders)),
        grid=(steps,),
        in_specs=[
            pl.BlockSpec((TOKEN_TILE, d), lambda i: (i, 0)),
            pl.BlockSpec((1, d), const),
            wspec(0), wspec(1), wspec(2), wspec(3),
            pl.BlockSpec((CONV_K, cw), const),
            pl.BlockSpec((1, cw), const),
        ] + cast_in,
        out_specs=[
            pl.BlockSpec((TOKEN_TILE, cw), lambda i: (i, 0)),
            pl.BlockSpec((TOKEN_TILE, cw), lambda i: (i, 0)),
            pl.BlockSpec((TOKEN_TILE, d), lambda i: (i, 0)),
        ] + cast_out,
        out_shape=[
            jax.ShapeDtypeStruct((n, cw), F32),
            jax.ShapeDtypeStruct((n, cw), BF16),
            jax.ShapeDtypeStruct((n, d), BF16),
        ] + cast_shape,
        scratch_shapes=[pltpu.VMEM((SUBLANES, cw), F32)],
        compiler_params=pltpu.CompilerParams(
            dimension_semantics=("arbitrary",), vmem_limit_bytes=VMEM_LIMIT_BYTES),
        name="inproj",
    )(x2, norm_g, w_in_bf, w_in_bf, w_in_bf, w_in_bf, conv_w, conv_b, *[w for w, _ in riders])
    return outs[0], outs[1], outs[2], outs[3:]


def _ssm_body(n_cast, u_ref, tz_ref, p_ref, q_ref, pw_ref, *rest):
    cast_src = rest[:n_cast]
    y_ref = rest[n_cast]
    cast_dst = rest[n_cast + 1:2 * n_cast + 1]
    s_scr, xin_scr = rest[2 * n_cast + 1:]
    _cast_riders(cast_src, cast_dst)
    nseq = u_ref.shape[0]
    m = u_ref.shape[1] // CHUNK
    sl = STATE_LANES
    ngl = GROUPS_PER_LANE_BLOCK

    yts = []
    for s in range(nseq):
        pts = [u_ref[s, pl.ds(t, m, stride=CHUNK), :].T.astype(BF16)
               for t in range(CHUNK)]
        ys, s_re, s_im = [], [], []
        for g in range(ngl):
            xg = jnp.concatenate([pt[g * SSM_GROUP:(g + 1) * SSM_GROUP] for pt in pts], axis=0)
            ys.append(_dot(tz_ref[g], xg))
            sg = _dot(p_ref[g], xg)
            s_re.append(sg[:SSM_STATE])
            s_im.append(sg[SSM_STATE:])
        yts.append(ys)
        s_scr[s] = jnp.concatenate(s_re + s_im, axis=0).T

    a16r = pw_ref[0:1, :]
    a16i = pw_ref[1:2, :]

    def step(k, carry):
        out = []
        for s in range(nseq):
            xr, xi = carry[2 * s], carry[2 * s + 1]
            xin_scr[s, pl.ds(k, 1), :sl] = xr
            xin_scr[s, pl.ds(k, 1), sl:] = xi
            row = s_scr[s, pl.ds(k, 1), :]
            out += [a16r * xr - a16i * xi + row[:, :sl], a16r * xi + a16i * xr + row[:, sl:]]
        return tuple(out)

    zero = jnp.zeros((1, sl), F32)
    lax.fori_loop(0, m, step, (zero,) * (2 * nseq), unroll=8)

    for s in range(nseq):
        xint = xin_scr[s].T
        ys = yts[s]
        for g in range(ngl):
            xg = jnp.concatenate([xint[g * SSM_STATE:(g + 1) * SSM_STATE],
                                  xint[sl + g * SSM_STATE:sl + (g + 1) * SSM_STATE]], axis=0)
            ys[g] = ys[g] + _dot(q_ref[g], xg.astype(BF16))
        for t in range(CHUNK):
            yt = jnp.concatenate([y[t * SSM_GROUP:(t + 1) * SSM_GROUP] for y in ys], axis=0)
            y_ref[s, pl.ds(t, m, stride=CHUNK), :] = jax.nn.gelu(yt.T)


def _ssm(u, tz, pm, qm, pw, batch, seq_len, riders, seqs_per_step=2):
    n, width = u.shape
    nblk = width // LANES
    m = seq_len // CHUNK
    ngl = GROUPS_PER_LANE_BLOCK
    u3 = u.reshape(batch, seq_len, width)
    nb = batch // seqs_per_step
    cast_specs = [pl.BlockSpec((w.shape[0] // (nblk * nb), w.shape[1]),
                               lambda c, b: (c * nb + b, 0)) for w in riders]
    outs = pl.pallas_call(
        functools.partial(_ssm_body, len(riders)),
        grid=(nblk, nb),
        in_specs=[
            pl.BlockSpec((seqs_per_step, seq_len, LANES), lambda c, b: (b, 0, c)),
            pl.BlockSpec((ngl, MXU_DIM, MXU_DIM), lambda c, b: (c, 0, 0)),
            pl.BlockSpec((ngl, 2 * SSM_STATE, MXU_DIM), lambda c, b: (c, 0, 0)),
            pl.BlockSpec((ngl, MXU_DIM, 2 * SSM_STATE), lambda c, b: (c, 0, 0)),
            pl.BlockSpec((None, 2, STATE_LANES), lambda c, b: (c, 0, 0)),
        ] + cast_specs,
        out_specs=[pl.BlockSpec((seqs_per_step, seq_len, LANES), lambda c, b: (b, 0, c))]
        + cast_specs,
        out_shape=[jax.ShapeDtypeStruct((batch, seq_len, width), F32)]
        + [jax.ShapeDtypeStruct(w.shape, BF16) for w in riders],
        scratch_shapes=[pltpu.VMEM((seqs_per_step, m, 2 * STATE_LANES), F32),
                        pltpu.VMEM((seqs_per_step, m, 2 * STATE_LANES), F32)],
        compiler_params=pltpu.CompilerParams(
            dimension_semantics=("arbitrary", "arbitrary"),
            vmem_limit_bytes=VMEM_LIMIT_BYTES),
        name="ssm",
    )(u3, tz, pm, qm, pw, *riders)
    return outs[0].reshape(n, width), outs[1:]


def _ssm_operators(a_re, a_im, log_dt, b_re, b_im, c_re, c_im, d_skip):
    hp = lax.Precision.HIGHEST
    g = a_re.shape[0]
    nblk = g // GROUPS_PER_LANE_BLOCK
    dt = jnp.exp(log_dt)[:, None]
    mag = jnp.exp(dt * a_re)
    abar_re = mag * jnp.cos(dt * a_im)
    abar_im = mag * jnp.sin(dt * a_im)
    nr = abar_re - 1.0
    ni = abar_im
    den = a_re * a_re + a_im * a_im
    fr = (nr * a_re + ni * a_im) / den
    fi = (ni * a_re - nr * a_im) / den
    bb_re = fr[..., None] * b_re - fi[..., None] * b_im
    bb_im = fr[..., None] * b_im + fi[..., None] * b_re
    nn = jnp.arange(CHUNK + 1, dtype=F32)
    pmag = jnp.exp(nn * (dt * a_re)[..., None])
    pw_re = pmag * jnp.cos(nn * (dt * a_im)[..., None])
    pw_im = pmag * jnp.sin(nn * (dt * a_im)[..., None])

    col = jnp.arange(MXU_DIM)
    pick = (CHUNK - 1 - col[None, :] // SSM_GROUP == jnp.arange(CHUNK)[:, None]).astype(F32)
    e = jnp.einsum("rgpn,nc->rgpc", jnp.stack([pw_re, pw_im])[..., :CHUNK], pick,
                   precision=hp)
    spread = (col[None, :] % SSM_GROUP == jnp.arange(SSM_GROUP)[:, None]).astype(F32)
    bx = jnp.einsum("rgpi,ic->rgpc", jnp.stack([bb_re, bb_im]), spread, precision=hp)
    p_re = e[0] * bx[0] - e[1] * bx[1]
    p_im = e[0] * bx[1] + e[1] * bx[0]
    p_ri = jnp.concatenate([p_re, p_im], axis=1)
    pm = p_ri.astype(BF16)

    krev = jnp.einsum("gop,gpc->goc", jnp.concatenate([c_re, -c_im], axis=-1), p_ri,
                      precision=hp)
    lag0 = (CHUNK - 1) * SSM_GROUP + jnp.arange(SSM_GROUP)
    krev = krev + d_skip[..., None] * (col[None, :] == lag0[:, None]).astype(F32)
    padded = jnp.concatenate([krev, jnp.zeros_like(krev)], axis=-1)
    tz = jnp.stack([padded[..., (CHUNK - 1 - t) * SSM_GROUP:(CHUNK - 1 - t) * SSM_GROUP + MXU_DIM]
                    for t in range(CHUNK)], axis=1)
    tz = tz.reshape(g, MXU_DIM, MXU_DIM).astype(BF16)

    o_re = pw_re[..., 1:].transpose(0, 2, 1)[:, :, None, :]
    o_im = pw_im[..., 1:].transpose(0, 2, 1)[:, :, None, :]
    cr = c_re[:, None]
    ci = c_im[:, None]
    qm = jnp.concatenate([cr * o_re - ci * o_im, -(cr * o_im + ci * o_re)], axis=-1)
    qm = qm.reshape(g, MXU_DIM, 2 * SSM_STATE).astype(BF16)

    pw = jnp.stack([pw_re[..., CHUNK], pw_im[..., CHUNK]])
    pw = pw.reshape(2, nblk, STATE_LANES).transpose(1, 0, 2)
    return tz, pm, qm, pw


def _merge_body(col_tile, xn_ref, ya_ref, zb_ref, wglu_ref, wm_ref, wssm_ref, wconv_ref, m_ref):
    d = m_ref.shape[1]
    ya = ya_ref[...]
    ya2 = (ya * jax.nn.sigmoid(_dot(ya.astype(BF16), wglu_ref[...]))).astype(BF16)
    xn = xn_ref[...]
    zb = zb_ref[...]
    for c in range(d // col_tile):
        cols = slice(c * col_tile, (c + 1) * col_tile)
        gate_a = jax.nn.sigmoid(_dot(xn, wm_ref[:, cols]))
        gate_b = jax.nn.sigmoid(_dot(xn, wm_ref[:, d + c * col_tile:d + (c + 1) * col_tile]))
        m_ref[:, cols] = (gate_a * _dot(ya2, wssm_ref[:, cols])
                          + gate_b * _dot(zb, wconv_ref[:, cols])).astype(BF16)


def _merge(xn, ya, zb, w_glu_bf, w_merge_bf, w_ssm_bf, w_conv_bf, col_tile=512):
    n, d = xn.shape
    sw = ya.shape[1]
    cw = zb.shape[1]
    resident = lambda shape: pl.BlockSpec(shape, lambda i: (0, 0), pipeline_mode=pl.Buffered(1))
    return pl.pallas_call(
        functools.partial(_merge_body, col_tile),
        grid=(n // TOKEN_TILE,),
        in_specs=[
            pl.BlockSpec((TOKEN_TILE, d), lambda i: (i, 0)),
            pl.BlockSpec((TOKEN_TILE, sw), lambda i: (i, 0)),
            pl.BlockSpec((TOKEN_TILE, cw), lambda i: (i, 0)),
            resident((sw, sw)),
            resident((d, 2 * d)),
            resident((sw, d)),
            resident((cw, d)),
        ],
        out_specs=pl.BlockSpec((TOKEN_TILE, d), lambda i: (i, 0)),
        out_shape=jax.ShapeDtypeStruct((n, d), BF16),
        compiler_params=pltpu.CompilerParams(
            dimension_semantics=("arbitrary",), vmem_limit_bytes=VMEM_LIMIT_BYTES),
        name="merge",
    )(xn, ya, zb, w_glu_bf, w_merge_bf, w_ssm_bf, w_conv_bf)


def _outproj_body(col_tile, x_ref, m_ref, wo_ref, h_ref):
    m = m_ref[...]
    for c in range(h_ref.shape[1] // col_tile):
        cols = slice(c * col_tile, (c + 1) * col_tile)
        h_ref[:, cols] = x_ref[:, cols] + _dot(m, wo_ref[:, cols])


def _outproj(x2, merged, w_o_bf, col_tile=512):
    n, d = x2.shape
    tm = FFN_TOKEN_TILE
    return pl.pallas_call(
        functools.partial(_outproj_body, col_tile),
        grid=(n // tm,),
        in_specs=[
            pl.BlockSpec((tm, d), lambda i: (i, 0)),
            pl.BlockSpec((tm, d), lambda i: (i, 0)),
            pl.BlockSpec((d, d), lambda i: (0, 0), pipeline_mode=pl.Buffered(1)),
        ],
        out_specs=pl.BlockSpec((tm, d), lambda i: (i, 0)),
        out_shape=jax.ShapeDtypeStruct((n, d), F32),
        compiler_params=pltpu.CompilerParams(
            dimension_semantics=("arbitrary",), vmem_limit_bytes=VMEM_LIMIT_BYTES),
        name="outproj",
    )(x2, merged, w_o_bf)


def _ffn_body(tiles_per_seq, final_norm, h_hbm, g_ref, wa_ref, wg_ref, cw_ref, cb_ref,
              wd_ref, gfin_ref, o_ref, hbuf, hn_scr, prev_scr, sem):
    i = pl.program_id(0)
    f = pl.program_id(1)
    tm = o_ref.shape[0]

    def h_copy(tile):
        rows = pl.ds(pl.multiple_of(tile * tm, tm), tm)
        return pltpu.make_async_copy(h_hbm.at[rows], hbuf, sem)

    @pl.when(f == 0)
    def _():
        @pl.when(i == 0)
        def _():
            h_copy(0).start()

        h_copy(i).wait()
        h = hbuf[...]
        hn_scr[...] = _rmsnorm(h, g_ref[...]).astype(BF16)
        o_ref[...] = h

        @pl.when(i + 1 < pl.num_programs(0))
        def _():
            h_copy(i + 1).start()

    @pl.when(i % tiles_per_seq == 0)
    def _():
        prev_scr[f] = jnp.zeros(prev_scr.shape[1:], F32)

    hn = hn_scr[...]
    a = _dot(hn, wa_ref[...])
    conv = _causal_conv3(a, prev_scr[f], cw_ref, cb_ref)
    prev_scr[f] = a[a.shape[0] - SUBLANES:]
    act = (jax.nn.gelu(conv) * _dot(hn, wg_ref[...])).astype(BF16)
    o_ref[...] += _dot(act, wd_ref[...])

    if final_norm:
        @pl.when(f == pl.num_programs(1) - 1)
        def _():
            o_ref[...] = _rmsnorm(o_ref[...], gfin_ref[...])


def _ffn(h1, norm_g, w_up_bf, ffn_conv_w, ffn_conv_b, w_down_bf, norm_final, final_norm,
         seq_len, ff_tile=512):
    n, d = h1.shape
    d_ff = w_down_bf.shape[0]
    nff = d_ff // ff_tile
    tm = FFN_TOKEN_TILE
    tiles_per_seq = seq_len // tm
    const = lambda i, f: (0, 0)
    return pl.pallas_call(
        functools.partial(_ffn_body, tiles_per_seq, final_norm),
        grid=(n // tm, nff),
        in_specs=[
            pl.BlockSpec(memory_space=pl.ANY),
            pl.BlockSpec((1, d), const),
            pl.BlockSpec((d, ff_tile), lambda i, f: (0, f)),
            pl.BlockSpec((d, ff_tile), lambda i, f: (0, nff + f)),
            pl.BlockSpec((CONV_K, ff_tile), lambda i, f: (0, f)),
            pl.BlockSpec((1, ff_tile), lambda i, f: (0, f)),
            pl.BlockSpec((ff_tile, d), lambda i, f: (f, 0)),
            pl.BlockSpec((1, d), const),
        ],
        out_specs=pl.BlockSpec((tm, d), lambda i, f: (i, 0)),
        out_shape=jax.ShapeDtypeStruct((n, d), F32),
        scratch_shapes=[pltpu.VMEM((tm, d), F32),
                        pltpu.VMEM((tm, d), BF16),
                        pltpu.VMEM((nff, SUBLANES, ff_tile), F32),
                        pltpu.SemaphoreType.DMA(())],
        compiler_params=pltpu.CompilerParams(
            dimension_semantics=("arbitrary", "arbitrary"),
            vmem_limit_bytes=VMEM_LIMIT_BYTES),
        name="ffn",
    )(h1, norm_g, w_up_bf, w_up_bf, ffn_conv_w, ffn_conv_b, w_down_bf, norm_final)


def kernel(x, norm_tok, w_in, a_re, a_im, log_dt, b_re, b_im, c_re, c_im, d_skip, w_glu, w_ssm_out, conv_w, conv_b, w_conv_out, w_o, norm_ffn, w_up, ffn_conv_w, ffn_conv_b, w_down, norm_final):
    batch, seq_len, d = x.shape
    depth = w_in.shape[0]
    h = x.reshape(batch * seq_len, d)
    for l in range(depth):
        n_first = w_in.shape[2] - 2 * d
        w_first_bf = w_in[l, :, :n_first].astype(BF16)
        u, zb, xn, (w_merge_bf, w_glu_bf, w_ssm_bf, w_conv_bf, w_o_bf, w_down_bf) = _inproj(
            h, norm_tok[l][None], w_first_bf, conv_w[l], conv_b[l][None], seq_len,
            [(w_in[l], (n_first // (2 * d), 2 * d)), (w_glu[l], None), (w_ssm_out[l], None),
             (w_conv_out[l], None), (w_o[l], None), (w_down[l], None)])
        ops = _ssm_operators(a_re[l], a_im[l], log_dt[l], b_re[l], b_im[l],
                             c_re[l], c_im[l], d_skip[l])
        ya, (w_up_bf,) = _ssm(u, *ops, batch, seq_len, [w_up[l]])
        merged = _merge(xn, ya, zb, w_glu_bf, w_merge_bf, w_ssm_bf, w_conv_bf)
        h1 = _outproj(h, merged, w_o_bf)
        h = _ffn(h1, norm_ffn[l][None], w_up_bf, ffn_conv_w[l], ffn_conv_b[l][None],
                 w_down_bf, norm_final[None], l == depth - 1, seq_len)
    return h.reshape(batch, seq_len, d)
```

```python
import functools

import jax
import jax.numpy as jnp
from jax import lax
from jax.experimental import pallas as pl
from jax.experimental.pallas import tpu as pltpu

SSM_GROUP = 16
SSM_STATE = 64
CONV_K = 3
EPS = 1e-6

LANES = 128
SUBLANES = 8
MXU_DIM = 256
VMEM_LIMIT_BYTES = 56 * 1024 * 1024

TOKEN_TILE = 512
FFN_TOKEN_TILE = 1024
CHUNK = 16
GROUPS_PER_LANE_BLOCK = LANES // SSM_GROUP
STATE_LANES = GROUPS_PER_LANE_BLOCK * SSM_STATE

F32 = jnp.float32
BF16 = jnp.bfloat16


def _rmsnorm(x, gain):
    return x * lax.rsqrt(jnp.mean(x * x, axis=-1, keepdims=True) + EPS) * gain


def _dot(a, b):
    return jnp.dot(a, b, preferred_element_type=F32)


def _causal_conv3(a, prev, w_ref, b_ref):
    w0 = w_ref[0:1, :]
    w1 = w_ref[1:2, :]
    w2 = w_ref[2:3, :]
    bias = b_ref[...]
    full = w2 * a + w1 * pltpu.roll(a, 1, 0) + w0 * pltpu.roll(a, 2, 0) + bias
    top = a[0:SUBLANES]
    rows = lax.broadcasted_iota(jnp.int32, top.shape, 0)
    s1 = jnp.where(rows < 1, pltpu.roll(prev, 1, 0), pltpu.roll(top, 1, 0))
    s2 = jnp.where(rows < 2, pltpu.roll(prev, 2, 0), pltpu.roll(top, 2, 0))
    fix = w2 * top + w1 * s1 + w0 * s2 + bias
    return jnp.concatenate([fix, full[SUBLANES:]], axis=0)


def _cast_riders(src_refs, dst_refs):
    for src, dst in zip(src_refs, dst_refs):
        dst[...] = src[...].astype(BF16)


def _inproj_body(tiles_per_seq, n_cast, x_ref, g_ref, wu_ref, wv_ref, wgb_ref, wgc_ref,
                 cw_ref, cb_ref, *rest):
    cast_src = rest[:n_cast]
    u_ref, zb_ref, xn_ref = rest[n_cast:n_cast + 3]
    cast_dst = rest[n_cast + 3:2 * n_cast + 3]
    prev_ref = rest[2 * n_cast + 3]
    i = pl.program_id(0)

    @pl.when(i % tiles_per_seq == 0)
    def _():
        prev_ref[...] = jnp.zeros_like(prev_ref)

    xn = _rmsnorm(x_ref[...], g_ref[...]).astype(BF16)
    xn_ref[...] = xn
    u_ref[...] = _dot(xn, wu_ref[...])
    cin = _dot(xn, wgc_ref[...]) * _dot(xn, wv_ref[...])
    conv = _causal_conv3(cin, prev_ref[...], cw_ref, cb_ref)
    prev_ref[...] = cin[cin.shape[0] - SUBLANES:]
    zb_ref[...] = (_dot(xn, wgb_ref[...]) * conv).astype(BF16)
    _cast_riders(cast_src, cast_dst)


def _inproj(x2, norm_g, w_in_bf, conv_w, conv_b, seq_len, riders):
    n, d = x2.shape
    cw = conv_w.shape[1]
    steps = n // TOKEN_TILE
    tiles_per_seq = seq_len // TOKEN_TILE
    const = lambda i: (0, 0)
    wspec = lambda col: pl.BlockSpec((d, cw), lambda i, col=col: (0, col),
                                     pipeline_mode=pl.Buffered(1))
    cast_in, cast_out, cast_shape = [], [], []
    for w, cols in riders:
        rows = w.shape[0] // steps
        width = w.shape[1] if cols is None else cols[1]
        col = 0 if cols is None else cols[0]
        cast_in.append(pl.BlockSpec((rows, width), lambda i, col=col: (i, col)))
        cast_out.append(pl.BlockSpec((rows, width), lambda i: (i, 0)))
        cast_shape.append(jax.ShapeDtypeStruct((w.shape[0], width), BF16))
    outs = pl.pallas_call(
        functools.partial(_inproj_body, tiles_per_seq, len(riders)),
        grid=(steps,),
        in_specs=[
            pl.BlockSpec((TOKEN_TILE, d), lambda i: (i, 0)),
            pl.BlockSpec((1, d), const),
            wspec(0), wspec(1), wspec(2), wspec(3),
            pl.BlockSpec((CONV_K, cw), const),
            pl.BlockSpec((1, cw), const),
        ] + cast_in,
        out_specs=[
            pl.BlockSpec((TOKEN_TILE, cw), lambda i: (i, 0)),
            pl.BlockSpec((TOKEN_TILE, cw), lambda i: (i, 0)),
            pl.BlockSpec((TOKEN_TILE, d), lambda i: (i, 0)),
        ] + cast_out,
        out_shape=[
            jax.ShapeDtypeStruct((n, cw), F32),
            jax.ShapeDtypeStruct((n, cw), BF16),
            jax.ShapeDtypeStruct((n, d), BF16),
        ] + cast_shape,
        scratch_shapes=[pltpu.VMEM((SUBLANES, cw), F32)],
        compiler_params=pltpu.CompilerParams(
            dimension_semantics=("arbitrary",), vmem_limit_bytes=VMEM_LIMIT_BYTES),
        name="inproj",
    )(x2, norm_g, w_in_bf, w_in_bf, w_in_bf, w_in_bf, conv_w, conv_b, *[w for w, _ in riders])
    return outs[0], outs[1], outs[2], outs[3:]


def _ssm_body(n_cast, u_ref, tz_ref, p_ref, q_ref, pw_ref, *rest):
    cast_src = rest[:n_cast]
    y_ref = rest[n_cast]
    cast_dst = rest[n_cast + 1:2 * n_cast + 1]
    s_scr, xin_scr = rest[2 * n_cast + 1:]
    _cast_riders(cast_src, cast_dst)
    nseq = u_ref.shape[0]
    m = u_ref.shape[1] // CHUNK
    sl = STATE_LANES
    ngl = GROUPS_PER_LANE_BLOCK

    yts = []
    for s in range(nseq):
        pts = [u_ref[s, pl.ds(t, m, stride=CHUNK), :].T.astype(BF16)
               for t in range(CHUNK)]
        ys, s_re, s_im = [], [], []
        for g in range(ngl):
            xg = jnp.concatenate([pt[g * SSM_GROUP:(g + 1) * SSM_GROUP] for pt in pts], axis=0)
            ys.append(_dot(tz_ref[g], xg))
            sg = _dot(p_ref[g], xg)
            s_re.append(sg[:SSM_STATE])
            s_im.append(sg[SSM_STATE:])
        yts.append(ys)
        s_scr[s] = jnp.concatenate(s_re + s_im, axis=0).T

    a16r = pw_ref[0:1, :]
    a16i = pw_ref[1:2, :]

    def step(k, carry):
        out = []
        for s in range(nseq):
            xr, xi = carry[2 * s], carry[2 * s + 1]
            xin_scr[s, pl.ds(k, 1), :sl] = xr
            xin_scr[s, pl.ds(k, 1), sl:] = xi
            row = s_scr[s, pl.ds(k, 1), :]
            out += [a16r * xr - a16i * xi + row[:, :sl], a16r * xi + a16i * xr + row[:, sl:]]
        return tuple(out)

    zero = jnp.zeros((1, sl), F32)
    lax.fori_loop(0, m, step, (zero,) * (2 * nseq), unroll=8)

    for s in range(nseq):
        xint = xin_scr[s].T
        ys = yts[s]
        for g in range(ngl):
            xg = jnp.concatenate([xint[g * SSM_STATE:(g + 1) * SSM_STATE],
                                  xint[sl + g * SSM_STATE:sl + (g + 1) * SSM_STATE]], axis=0)
            ys[g] = ys[g] + _dot(q_ref[g], xg.astype(BF16))
        for t in range(CHUNK):
            yt = jnp.concatenate([y[t * SSM_GROUP:(t + 1) * SSM_GROUP] for y in ys], axis=0)
            y_ref[s, pl.ds(t, m, stride=CHUNK), :] = jax.nn.gelu(yt.T)


def _ssm(u, tz, pm, qm, pw, batch, seq_len, riders, seqs_per_step=2):
    n, width = u.shape
    nblk = width // LANES
    m = seq_len // CHUNK
    ngl = GROUPS_PER_LANE_BLOCK
    u3 = u.reshape(batch, seq_len, width)
    nb = batch // seqs_per_step
    cast_specs = [pl.BlockSpec((w.shape[0] // (nblk * nb), w.shape[1]),
                               lambda c, b: (c * nb + b, 0)) for w in riders]
    outs = pl.pallas_call(
        functools.partial(_ssm_body, len(riders)),
        grid=(nblk, nb),
        in_specs=[
            pl.BlockSpec((seqs_per_step, seq_len, LANES), lambda c, b: (b, 0, c)),
            pl.BlockSpec((ngl, MXU_DIM, MXU_DIM), lambda c, b: (c, 0, 0)),
            pl.BlockSpec((ngl, 2 * SSM_STATE, MXU_DIM), lambda c, b: (c, 0, 0)),
            pl.BlockSpec((ngl, MXU_DIM, 2 * SSM_STATE), lambda c, b: (c, 0, 0)),
            pl.BlockSpec((None, 2, STATE_LANES), lambda c, b: (c, 0, 0)),
        ] + cast_specs,
        out_specs=[pl.BlockSpec((seqs_per_step, seq_len, LANES), lambda c, b: (b, 0, c))]
        + cast_specs,
        out_shape=[jax.ShapeDtypeStruct((batch, seq_len, width), F32)]
        + [jax.ShapeDtypeStruct(w.shape, BF16) for w in riders],
        scratch_shapes=[pltpu.VMEM((seqs_per_step, m, 2 * STATE_LANES), F32),
                        pltpu.VMEM((seqs_per_step, m, 2 * STATE_LANES), F32)],
        compiler_params=pltpu.CompilerParams(
            dimension_semantics=("arbitrary", "arbitrary"),
            vmem_limit_bytes=VMEM_LIMIT_BYTES),
        name="ssm",
    )(u3, tz, pm, qm, pw, *riders)
    return outs[0].reshape(n, width), outs[1:]


def _ssm_operators(a_re, a_im, log_dt, b_re, b_im, c_re, c_im, d_skip):
    hp = lax.Precision.HIGH
    g = a_re.shape[0]
    nblk = g // GROUPS_PER_LANE_BLOCK
    dt = jnp.exp(log_dt)[:, None]
    mag = jnp.exp(dt * a_re)
    abar_re = mag * jnp.cos(dt * a_im)
    abar_im = mag * jnp.sin(dt * a_im)
    nr = abar_re - 1.0
    ni = abar_im
    den = a_re * a_re + a_im * a_im
    fr = (nr * a_re + ni * a_im) / den
    fi = (ni * a_re - nr * a_im) / den
    bb_re = fr[..., None] * b_re - fi[..., None] * b_im
    bb_im = fr[..., None] * b_im + fi[..., None] * b_re
    nn = jnp.arange(CHUNK + 1, dtype=F32)
    pmag = jnp.exp(nn * (dt * a_re)[..., None])
    pw_re = pmag * jnp.cos(nn * (dt * a_im)[..., None])
    pw_im = pmag * jnp.sin(nn * (dt * a_im)[..., None])

    col = jnp.arange(MXU_DIM)
    pick = (CHUNK - 1 - col[None, :] // SSM_GROUP == jnp.arange(CHUNK)[:, None]).astype(F32)
    e = jnp.einsum("rgpn,nc->rgpc", jnp.stack([pw_re, pw_im])[..., :CHUNK], pick,
                   precision=hp)
    spread = (col[None, :] % SSM_GROUP == jnp.arange(SSM_GROUP)[:, None]).astype(F32)
    bx = jnp.einsum("rgpi,ic->rgpc", jnp.stack([bb_re, bb_im]), spread, precision=hp)
    p_re = e[0] * bx[0] - e[1] * bx[1]
    p_im = e[0] * bx[1] + e[1] * bx[0]
    p_ri = jnp.concatenate([p_re, p_im], axis=1)
    pm = p_ri.astype(BF16)

    krev = jnp.einsum("gop,gpc->goc", jnp.concatenate([c_re, -c_im], axis=-1), p_ri,
                      precision=hp)
    lag0 = (CHUNK - 1) * SSM_GROUP + jnp.arange(SSM_GROUP)
    krev = krev + d_skip[..., None] * (col[None, :] == lag0[:, None]).astype(F32)
    padded = jnp.concatenate([krev, jnp.zeros_like(krev)], axis=-1)
    tz = jnp.stack([padded[..., (CHUNK - 1 - t) * SSM_GROUP:(CHUNK - 1 - t) * SSM_GROUP + MXU_DIM]
                    for t in range(CHUNK)], axis=1)
    tz = tz.reshape(g, MXU_DIM, MXU_DIM).astype(BF16)

    o_re = pw_re[..., 1:].transpose(0, 2, 1)[:, :, None, :]
    o_im = pw_im[..., 1:].transpose(0, 2, 1)[:, :, None, :]
    cr = c_re[:, None]
    ci = c_im[:, None]
    qm = jnp.concatenate([cr * o_re - ci * o_im, -(cr * o_im + ci * o_re)], axis=-1)
    qm = qm.reshape(g, MXU_DIM, 2 * SSM_STATE).astype(BF16)

    pw = jnp.stack([pw_re[..., CHUNK], pw_im[..., CHUNK]])
    pw = pw.reshape(2, nblk, STATE_LANES).transpose(1, 0, 2)
    return tz, pm, qm, pw


def _merge_body(col_tile, xn_ref, ya_ref, zb_ref, wglu_ref, wm_ref, wssm_ref, wconv_ref, m_ref):
    d = m_ref.shape[1]
    ya = ya_ref[...]
    ya2 = (ya * jax.nn.sigmoid(_dot(ya.astype(BF16), wglu_ref[...]))).astype(BF16)
    xn = xn_ref[...]
    zb = zb_ref[...]
    for c in range(d // col_tile):
        cols = slice(c * col_tile, (c + 1) * col_tile)
        gate_a = jax.nn.sigmoid(_dot(xn, wm_ref[:, cols]))
        gate_b = jax.nn.sigmoid(_dot(xn, wm_ref[:, d + c * col_tile:d + (c + 1) * col_tile]))
        m_ref[:, cols] = (gate_a * _dot(ya2, wssm_ref[:, cols])
                          + gate_b * _dot(zb, wconv_ref[:, cols])).astype(BF16)


def _merge(xn, ya, zb, w_glu_bf, w_merge_bf, w_ssm_bf, w_conv_bf, col_tile=512):
    n, d = xn.shape
    sw = ya.shape[1]
    cw = zb.shape[1]
    resident = lambda shape: pl.BlockSpec(shape, lambda i: (0, 0), pipeline_mode=pl.Buffered(1))
    return pl.pallas_call(
        functools.partial(_merge_body, col_tile),
        grid=(n // TOKEN_TILE,),
        in_specs=[
            pl.BlockSpec((TOKEN_TILE, d), lambda i: (i, 0)),
            pl.BlockSpec((TOKEN_TILE, sw), lambda i: (i, 0)),
            pl.BlockSpec((TOKEN_TILE, cw), lambda i: (i, 0)),
            resident((sw, sw)),
            resident((d, 2 * d)),
            resident((sw, d)),
            resident((cw, d)),
        ],
        out_specs=pl.BlockSpec((TOKEN_TILE, d), lambda i: (i, 0)),
        out_shape=jax.ShapeDtypeStruct((n, d), BF16),
        compiler_params=pltpu.CompilerParams(
            dimension_semantics=("arbitrary",), vmem_limit_bytes=VMEM_LIMIT_BYTES),
        name="merge",
    )(xn, ya, zb, w_glu_bf, w_merge_bf, w_ssm_bf, w_conv_bf)


def _outproj_body(col_tile, x_ref, m_ref, wo_ref, h_ref):
    m = m_ref[...]
    for c in range(h_ref.shape[1] // col_tile):
        cols = slice(c * col_tile, (c + 1) * col_tile)
        h_ref[:, cols] = x_ref[:, cols] + _dot(m, wo_ref[:, cols])


def _outproj(x2, merged, w_o_bf, col_tile=512):
    n, d = x2.shape
    tm = FFN_TOKEN_TILE
    return pl.pallas_call(
        functools.partial(_outproj_body, col_tile),
        grid=(n // tm,),
        in_specs=[
            pl.BlockSpec((tm, d), lambda i: (i, 0)),
            pl.BlockSpec((tm, d), lambda i: (i, 0)),
            pl.BlockSpec((d, d), lambda i: (0, 0), pipeline_mode=pl.Buffered(1)),
        ],
        out_specs=pl.BlockSpec((tm, d), lambda i: (i, 0)),
        out_shape=jax.ShapeDtypeStruct((n, d), F32),
        compiler_params=pltpu.CompilerParams(
            dimension_semantics=("arbitrary",), vmem_limit_bytes=VMEM_LIMIT_BYTES),
        name="outproj",
    )(x2, merged, w_o_bf)


def _ffn_body(tiles_per_seq, final_norm, h_hbm, g_ref, wa_ref, wg_ref, cw_ref, cb_ref,
              wd_ref, gfin_ref, o_ref, hbuf, hn_scr, prev_scr, sem):
    i = pl.program_id(0)
    f = pl.program_id(1)
    tm = o_ref.shape[0]

    def h_copy(tile):
        rows = pl.ds(pl.multiple_of(tile * tm, tm), tm)
        return pltpu.make_async_copy(h_hbm.at[rows], hbuf, sem)

    @pl.when(f == 0)
    def _():
        @pl.when(i == 0)
        def _():
            h_copy(0).start()

        h_copy(i).wait()
        h = hbuf[...]
        hn_scr[...] = _rmsnorm(h, g_ref[...]).astype(BF16)
        o_ref[...] = h

        @pl.when(i + 1 < pl.num_programs(0))
        def _():
            h_copy(i + 1).start()

    @pl.when(i % tiles_per_seq == 0)
    def _():
        prev_scr[f] = jnp.zeros(prev_scr.shape[1:], F32)

    hn = hn_scr[...]
    a = _dot(hn, wa_ref[...])
    conv = _causal_conv3(a, prev_scr[f], cw_ref, cb_ref)
    prev_scr[f] = a[a.shape[0] - SUBLANES:]
    act = (jax.nn.gelu(conv) * _dot(hn, wg_ref[...])).astype(BF16)
    o_ref[...] += _dot(act, wd_ref[...])

    if final_norm:
        @pl.when(f == pl.num_programs(1) - 1)
        def _():
            o_ref[...] = _rmsnorm(o_ref[...], gfin_ref[...])


def _ffn(h1, norm_g, w_up_bf, ffn_conv_w, ffn_conv_b, w_down_bf, norm_final, final_norm,
         seq_len, ff_tile=512):
    n, d = h1.shape
    d_ff = w_down_bf.shape[0]
    nff = d_ff // ff_tile
    tm = FFN_TOKEN_TILE
    tiles_per_seq = seq_len // tm
    const = lambda i, f: (0, 0)
    return pl.pallas_call(
        functools.partial(_ffn_body, tiles_per_seq, final_norm),
        grid=(n // tm, nff),
        in_specs=[
            pl.BlockSpec(memory_space=pl.ANY),
            pl.BlockSpec((1, d), const),
            pl.BlockSpec((d, ff_tile), lambda i, f: (0, f)),
            pl.BlockSpec((d, ff_tile), lambda i, f: (0, nff + f)),
            pl.BlockSpec((CONV_K, ff_tile), lambda i, f: (0, f)),
            pl.BlockSpec((1, ff_tile), lambda i, f: (0, f)),
            pl.BlockSpec((ff_tile, d), lambda i, f: (f, 0)),
            pl.BlockSpec((1, d), const),
        ],
        out_specs=pl.BlockSpec((tm, d), lambda i, f: (i, 0)),
        out_shape=jax.ShapeDtypeStruct((n, d), F32),
        scratch_shapes=[pltpu.VMEM((tm, d), F32),
                        pltpu.VMEM((tm, d), BF16),
                        pltpu.VMEM((nff, SUBLANES, ff_tile), F32),
                        pltpu.SemaphoreType.DMA(())],
        compiler_params=pltpu.CompilerParams(
            dimension_semantics=("arbitrary", "arbitrary"),
            vmem_limit_bytes=VMEM_LIMIT_BYTES),
        name="ffn",
    )(h1, norm_g, w_up_bf, w_up_bf, ffn_conv_w, ffn_conv_b, w_down_bf, norm_final)


def kernel(x, norm_tok, w_in, a_re, a_im, log_dt, b_re, b_im, c_re, c_im, d_skip, w_glu, w_ssm_out, conv_w, conv_b, w_conv_out, w_o, norm_ffn, w_up, ffn_conv_w, ffn_conv_b, w_down, norm_final):
    batch, seq_len, d = x.shape
    depth = w_in.shape[0]
    h = x.reshape(batch * seq_len, d)
    for l in range(depth):
        n_first = w_in.shape[2] - 2 * d
        w_first_bf = w_in[l, :, :n_first].astype(BF16)
        u, zb, xn, (w_merge_bf, w_glu_bf, w_ssm_bf, w_conv_bf, w_o_bf, w_down_bf) = _inproj(
            h, norm_tok[l][None], w_first_bf, conv_w[l], conv_b[l][None], seq_len,
            [(w_in[l], (n_first // (2 * d), 2 * d)), (w_glu[l], None), (w_ssm_out[l], None),
             (w_conv_out[l], None), (w_o[l], None), (w_down[l], None)])
        ops = _ssm_operators(a_re[l], a_im[l], log_dt[l], b_re[l], b_im[l],
                             c_re[l], c_im[l], d_skip[l])
        ya, (w_up_bf,) = _ssm(u, *ops, batch, seq_len, [w_up[l]])
        merged = _merge(xn, ya, zb, w_glu_bf, w_merge_bf, w_ssm_bf, w_conv_bf)
        h1 = _outproj(h, merged, w_o_bf)
        h = _ffn(h1, norm_ffn[l][None], w_up_bf, ffn_conv_w[l], ffn_conv_b[l][None],
                 w_down_bf, norm_final[None], l == depth - 1, seq_len)
    return h.reshape(batch, seq_len, d)
```

```python
import functools

import jax
import jax.numpy as jnp
from jax import lax
from jax.experimental import pallas as pl
from jax.experimental.pallas import tpu as pltpu

SSM_GROUP = 16
SSM_STATE = 64
CONV_K = 3
EPS = 1e-6

LANES = 128
SUBLANES = 8
BF16_SUBLANES = 16
MXU_DIM = 256
VMEM_LIMIT_BYTES = 56 * 1024 * 1024

TOKEN_TILE = 512
FFN_TOKEN_TILE = 1024
CHUNK = 16
GROUPS_PER_LANE_BLOCK = LANES // SSM_GROUP
STATE_LANES = GROUPS_PER_LANE_BLOCK * SSM_STATE

F32 = jnp.float32
BF16 = jnp.bfloat16


def _rmsnorm(x, gain):
    return x * lax.rsqrt(jnp.mean(x * x, axis=-1, keepdims=True) + EPS) * gain


def _dot(a, b):
    return jnp.dot(a, b, preferred_element_type=F32)


def _causal_conv3(a, prev, w_ref, b_ref):
    w0 = w_ref[0:1, :]
    w1 = w_ref[1:2, :]
    w2 = w_ref[2:3, :]
    bias = b_ref[...]
    full = w2 * a + w1 * pltpu.roll(a, 1, 0) + w0 * pltpu.roll(a, 2, 0) + bias
    top = a[0:SUBLANES]
    rows = lax.broadcasted_iota(jnp.int32, top.shape, 0)
    s1 = jnp.where(rows < 1, pltpu.roll(prev, 1, 0), pltpu.roll(top, 1, 0))
    s2 = jnp.where(rows < 2, pltpu.roll(prev, 2, 0), pltpu.roll(top, 2, 0))
    fix = w2 * top + w1 * s1 + w0 * s2 + bias
    return jnp.concatenate([fix, full[SUBLANES:]], axis=0)


def _cast_riders(src_refs, dst_refs):
    for src, dst in zip(src_refs, dst_refs):
        dst[...] = src[...].astype(BF16)


def _inproj_body(tiles_per_seq, n_cast, x_ref, g_ref, wu_ref, wv_ref, wgb_ref, wgc_ref,
                 cw_ref, cb_ref, *rest):
    cast_src = rest[:n_cast]
    u_ref, zb_ref, xn_ref = rest[n_cast:n_cast + 3]
    cast_dst = rest[n_cast + 3:2 * n_cast + 3]
    prev_ref = rest[2 * n_cast + 3]
    i = pl.program_id(0)

    @pl.when(i % tiles_per_seq == 0)
    def _():
        prev_ref[...] = jnp.zeros_like(prev_ref)

    xn = _rmsnorm(x_ref[...], g_ref[...]).astype(BF16)
    xn_ref[...] = xn
    u_ref[...] = _dot(xn, wu_ref[...])
    cin = _dot(xn, wgc_ref[...]) * _dot(xn, wv_ref[...])
    conv = _causal_conv3(cin, prev_ref[...], cw_ref, cb_ref)
    prev_ref[...] = cin[cin.shape[0] - SUBLANES:]
    zb_ref[...] = (_dot(xn, wgb_ref[...]) * conv).astype(BF16)
    _cast_riders(cast_src, cast_dst)


def _inproj(x2, norm_g, w_in_bf, conv_w, conv_b, seq_len, riders):
    n, d = x2.shape
    cw = conv_w.shape[1]
    assert n % TOKEN_TILE == 0 and seq_len % TOKEN_TILE == 0 and w_in_bf.shape == (d, 4 * cw)
    steps = n // TOKEN_TILE
    tiles_per_seq = seq_len // TOKEN_TILE
    const = lambda i: (0, 0)
    wspec = lambda col: pl.BlockSpec((d, cw), lambda i, col=col: (0, col),
                                     pipeline_mode=pl.Buffered(1))
    cast_in, cast_out, cast_shape = [], [], []
    for w, cols in riders:
        assert w.shape[0] % (steps * BF16_SUBLANES) == 0
        rows = w.shape[0] // steps
        width = w.shape[1] if cols is None else cols[1]
        col = 0 if cols is None else cols[0]
        cast_in.append(pl.BlockSpec((rows, width), lambda i, col=col: (i, col)))
        cast_out.append(pl.BlockSpec((rows, width), lambda i: (i, 0)))
        cast_shape.append(jax.ShapeDtypeStruct((w.shape[0], width), BF16))
    outs = pl.pallas_call(
        functools.partial(_inproj_body, tiles_per_seq, len(riders)),
        grid=(steps,),
        in_specs=[
            pl.BlockSpec((TOKEN_TILE, d), lambda i: (i, 0)),
            pl.BlockSpec((1, d), const),
            wspec(0), wspec(1), wspec(2), wspec(3),
            pl.BlockSpec((CONV_K, cw), const),
            pl.BlockSpec((1, cw), const),
        ] + cast_in,
        out_specs=[
            pl.BlockSpec((TOKEN_TILE, cw), lambda i: (i, 0)),
            pl.BlockSpec((TOKEN_TILE, cw), lambda i: (i, 0)),
            pl.BlockSpec((TOKEN_TILE, d), lambda i: (i, 0)),
        ] + cast_out,
        out_shape=[
            jax.ShapeDtypeStruct((n, cw), F32),
            jax.ShapeDtypeStruct((n, cw), BF16),
            jax.ShapeDtypeStruct((n, d), BF16),
        ] + cast_shape,
        scratch_shapes=[pltpu.VMEM((SUBLANES, cw), F32)],
        compiler_params=pltpu.CompilerParams(
            dimension_semantics=("arbitrary",), vmem_limit_bytes=VMEM_LIMIT_BYTES),
        name="inproj",
    )(x2, norm_g, w_in_bf, w_in_bf, w_in_bf, w_in_bf, conv_w, conv_b, *[w for w, _ in riders])
    return outs[0], outs[1], outs[2], outs[3:]


def _ssm_body(n_cast, u_ref, tz_ref, p_ref, q_ref, pw_ref, *rest):
    cast_src = rest[:n_cast]
    y_ref = rest[n_cast]
    cast_dst = rest[n_cast + 1:2 * n_cast + 1]
    s_scr, xin_scr = rest[2 * n_cast + 1:]
    _cast_riders(cast_src, cast_dst)
    nseq = u_ref.shape[0]
    m = u_ref.shape[1] // CHUNK
    sl = STATE_LANES
    ngl = GROUPS_PER_LANE_BLOCK

    yts = []
    for s in range(nseq):
        pts = [u_ref[s, pl.ds(t, m, stride=CHUNK), :].T.astype(BF16)
               for t in range(CHUNK)]
        ys, s_re, s_im = [], [], []
        for g in range(ngl):
            xg = jnp.concatenate([pt[g * SSM_GROUP:(g + 1) * SSM_GROUP] for pt in pts], axis=0)
            ys.append(_dot(tz_ref[g], xg))
            sg = _dot(p_ref[g], xg)
            s_re.append(sg[:SSM_STATE])
            s_im.append(sg[SSM_STATE:])
        yts.append(ys)
        s_scr[s] = jnp.concatenate(s_re + s_im, axis=0).T

    a16r = pw_ref[0:1, :]
    a16i = pw_ref[1:2, :]

    def step(k, carry):
        out = []
        for s in range(nseq):
            xr, xi = carry[2 * s], carry[2 * s + 1]
            xin_scr[s, pl.ds(k, 1), :sl] = xr
            xin_scr[s, pl.ds(k, 1), sl:] = xi
            row = s_scr[s, pl.ds(k, 1), :]
            out += [a16r * xr - a16i * xi + row[:, :sl], a16r * xi + a16i * xr + row[:, sl:]]
        return tuple(out)

    zero = jnp.zeros((1, sl), F32)
    lax.fori_loop(0, m, step, (zero,) * (2 * nseq), unroll=8)

    for s in range(nseq):
        xint = xin_scr[s].T
        ys = yts[s]
        for g in range(ngl):
            xg = jnp.concatenate([xint[g * SSM_STATE:(g + 1) * SSM_STATE],
                                  xint[sl + g * SSM_STATE:sl + (g + 1) * SSM_STATE]], axis=0)
            ys[g] = ys[g] + _dot(q_ref[g], xg.astype(BF16))
        for t in range(CHUNK):
            yt = jnp.concatenate([y[t * SSM_GROUP:(t + 1) * SSM_GROUP] for y in ys], axis=0)
            y_ref[s, pl.ds(t, m, stride=CHUNK), :] = jax.nn.gelu(yt.T)


def _ssm(u, tz, pm, qm, pw, batch, seq_len, riders, seqs_per_step=2):
    n, width = u.shape
    nblk = width // LANES
    m = seq_len // CHUNK
    ngl = GROUPS_PER_LANE_BLOCK
    u3 = u.reshape(batch, seq_len, width)
    assert seq_len % CHUNK == 0 and width % LANES == 0 and batch % seqs_per_step == 0
    nb = batch // seqs_per_step
    assert all(w.shape[0] % (nblk * nb * BF16_SUBLANES) == 0 for w in riders)
    cast_specs = [pl.BlockSpec((w.shape[0] // (nblk * nb), w.shape[1]),
                               lambda c, b: (c * nb + b, 0)) for w in riders]
    outs = pl.pallas_call(
        functools.partial(_ssm_body, len(riders)),
        grid=(nblk, nb),
        in_specs=[
            pl.BlockSpec((seqs_per_step, seq_len, LANES), lambda c, b: (b, 0, c)),
            pl.BlockSpec((ngl, MXU_DIM, MXU_DIM), lambda c, b: (c, 0, 0)),
            pl.BlockSpec((ngl, 2 * SSM_STATE, MXU_DIM), lambda c, b: (c, 0, 0)),
            pl.BlockSpec((ngl, MXU_DIM, 2 * SSM_STATE), lambda c, b: (c, 0, 0)),
            pl.BlockSpec((None, 2, STATE_LANES), lambda c, b: (c, 0, 0)),
        ] + cast_specs,
        out_specs=[pl.BlockSpec((seqs_per_step, seq_len, LANES), lambda c, b: (b, 0, c))]
        + cast_specs,
        out_shape=[jax.ShapeDtypeStruct((batch, seq_len, width), F32)]
        + [jax.ShapeDtypeStruct(w.shape, BF16) for w in riders],
        scratch_shapes=[pltpu.VMEM((seqs_per_step, m, 2 * STATE_LANES), F32),
                        pltpu.VMEM((seqs_per_step, m, 2 * STATE_LANES), F32)],
        compiler_params=pltpu.CompilerParams(
            dimension_semantics=("arbitrary", "arbitrary"),
            vmem_limit_bytes=VMEM_LIMIT_BYTES),
        name="ssm",
    )(u3, tz, pm, qm, pw, *riders)
    return outs[0].reshape(n, width), outs[1:]


def _ssm_operators(a_re, a_im, log_dt, b_re, b_im, c_re, c_im, d_skip):
    hp = lax.Precision.HIGHEST
    g = a_re.shape[0]
    nblk = g // GROUPS_PER_LANE_BLOCK
    dt = jnp.exp(log_dt)[:, None]
    mag = jnp.exp(dt * a_re)
    abar_re = mag * jnp.cos(dt * a_im)
    abar_im = mag * jnp.sin(dt * a_im)
    nr = abar_re - 1.0
    ni = abar_im
    den = a_re * a_re + a_im * a_im
    fr = (nr * a_re + ni * a_im) / den
    fi = (ni * a_re - nr * a_im) / den
    bb_re = fr[..., None] * b_re - fi[..., None] * b_im
    bb_im = fr[..., None] * b_im + fi[..., None] * b_re
    nn = jnp.arange(CHUNK + 1, dtype=F32)
    pmag = jnp.exp(nn * (dt * a_re)[..., None])
    pw_re = pmag * jnp.cos(nn * (dt * a_im)[..., None])
    pw_im = pmag * jnp.sin(nn * (dt * a_im)[..., None])

    col = jnp.arange(MXU_DIM)
    pick = (CHUNK - 1 - col[None, :] // SSM_GROUP == jnp.arange(CHUNK)[:, None]).astype(F32)
    e = jnp.einsum("rgpn,nc->rgpc", jnp.stack([pw_re, pw_im])[..., :CHUNK], pick,
                   precision=hp)
    spread = (col[None, :] % SSM_GROUP == jnp.arange(SSM_GROUP)[:, None]).astype(F32)
    bx = jnp.einsum("rgpi,ic->rgpc", jnp.stack([bb_re, bb_im]), spread, precision=hp)
    p_re = e[0] * bx[0] - e[1] * bx[1]
    p_im = e[0] * bx[1] + e[1] * bx[0]
    p_ri = jnp.concatenate([p_re, p_im], axis=1)
    pm = p_ri.astype(BF16)

    krev = jnp.einsum("gop,gpc->goc", jnp.concatenate([c_re, -c_im], axis=-1), p_ri,
                      precision=hp)
    lag0 = (CHUNK - 1) * SSM_GROUP + jnp.arange(SSM_GROUP)
    krev = krev + d_skip[..., None] * (col[None, :] == lag0[:, None]).astype(F32)
    padded = jnp.concatenate([krev, jnp.zeros_like(krev)], axis=-1)
    tz = jnp.stack([padded[..., (CHUNK - 1 - t) * SSM_GROUP:(CHUNK - 1 - t) * SSM_GROUP + MXU_DIM]
                    for t in range(CHUNK)], axis=1)
    tz = tz.reshape(g, MXU_DIM, MXU_DIM).astype(BF16)

    o_re = pw_re[..., 1:].transpose(0, 2, 1)[:, :, None, :]
    o_im = pw_im[..., 1:].transpose(0, 2, 1)[:, :, None, :]
    cr = c_re[:, None]
    ci = c_im[:, None]
    qm = jnp.concatenate([cr * o_re - ci * o_im, -(cr * o_im + ci * o_re)], axis=-1)
    qm = qm.reshape(g, MXU_DIM, 2 * SSM_STATE).astype(BF16)

    pw = jnp.stack([pw_re[..., CHUNK], pw_im[..., CHUNK]])
    pw = pw.reshape(2, nblk, STATE_LANES).transpose(1, 0, 2)
    return tz, pm, qm, pw


def _merge_body(col_tile, xn_ref, ya_ref, zb_ref, wglu_ref, wm_ref, wssm_ref, wconv_ref, m_ref):
    d = m_ref.shape[1]
    ya = ya_ref[...]
    ya2 = (ya * jax.nn.sigmoid(_dot(ya.astype(BF16), wglu_ref[...]))).astype(BF16)
    xn = xn_ref[...]
    zb = zb_ref[...]
    for c in range(d // col_tile):
        cols = slice(c * col_tile, (c + 1) * col_tile)
        gate_a = jax.nn.sigmoid(_dot(xn, wm_ref[:, cols]))
        gate_b = jax.nn.sigmoid(_dot(xn, wm_ref[:, d + c * col_tile:d + (c + 1) * col_tile]))
        m_ref[:, cols] = (gate_a * _dot(ya2, wssm_ref[:, cols])
                          + gate_b * _dot(zb, wconv_ref[:, cols])).astype(BF16)


def _merge(xn, ya, zb, w_glu_bf, w_merge_bf, w_ssm_bf, w_conv_bf, col_tile=512):
    n, d = xn.shape
    sw = ya.shape[1]
    cw = zb.shape[1]
    assert n % TOKEN_TILE == 0 and d % col_tile == 0 and w_merge_bf.shape == (d, 2 * d)
    resident = lambda shape: pl.BlockSpec(shape, lambda i: (0, 0), pipeline_mode=pl.Buffered(1))
    return pl.pallas_call(
        functools.partial(_merge_body, col_tile),
        grid=(n // TOKEN_TILE,),
        in_specs=[
            pl.BlockSpec((TOKEN_TILE, d), lambda i: (i, 0)),
            pl.BlockSpec((TOKEN_TILE, sw), lambda i: (i, 0)),
            pl.BlockSpec((TOKEN_TILE, cw), lambda i: (i, 0)),
            resident((sw, sw)),
            resident((d, 2 * d)),
            resident((sw, d)),
            resident((cw, d)),
        ],
        out_specs=pl.BlockSpec((TOKEN_TILE, d), lambda i: (i, 0)),
        out_shape=jax.ShapeDtypeStruct((n, d), BF16),
        compiler_params=pltpu.CompilerParams(
            dimension_semantics=("arbitrary",), vmem_limit_bytes=VMEM_LIMIT_BYTES),
        name="merge",
    )(xn, ya, zb, w_glu_bf, w_merge_bf, w_ssm_bf, w_conv_bf)


def _outproj_body(col_tile, x_ref, m_ref, wo_ref, h_ref):
    m = m_ref[...]
    for c in range(h_ref.shape[1] // col_tile):
        cols = slice(c * col_tile, (c + 1) * col_tile)
        h_ref[:, cols] = x_ref[:, cols] + _dot(m, wo_ref[:, cols])


def _outproj(x2, merged, w_o_bf, col_tile=512):
    n, d = x2.shape
    tm = FFN_TOKEN_TILE
    assert n % tm == 0 and d % col_tile == 0
    return pl.pallas_call(
        functools.partial(_outproj_body, col_tile),
        grid=(n // tm,),
        in_specs=[
            pl.BlockSpec((tm, d), lambda i: (i, 0)),
            pl.BlockSpec((tm, d), lambda i: (i, 0)),
            pl.BlockSpec((d, d), lambda i: (0, 0), pipeline_mode=pl.Buffered(1)),
        ],
        out_specs=pl.BlockSpec((tm, d), lambda i: (i, 0)),
        out_shape=jax.ShapeDtypeStruct((n, d), F32),
        compiler_params=pltpu.CompilerParams(
            dimension_semantics=("arbitrary",), vmem_limit_bytes=VMEM_LIMIT_BYTES),
        name="outproj",
    )(x2, merged, w_o_bf)


def _ffn_body(tiles_per_seq, final_norm, h_hbm, g_ref, wa_ref, wg_ref, cw_ref, cb_ref,
              wd_ref, gfin_ref, o_ref, hbuf, hn_scr, prev_scr, sem):
    i = pl.program_id(0)
    f = pl.program_id(1)
    tm = o_ref.shape[0]

    def h_copy(tile):
        rows = pl.ds(pl.multiple_of(tile * tm, tm), tm)
        return pltpu.make_async_copy(h_hbm.at[rows], hbuf, sem)

    @pl.when(f == 0)
    def _():
        @pl.when(i == 0)
        def _():
            h_copy(0).start()

        h_copy(i).wait()
        h = hbuf[...]
        hn_scr[...] = _rmsnorm(h, g_ref[...]).astype(BF16)
        o_ref[...] = h

        @pl.when(i + 1 < pl.num_programs(0))
        def _():
            h_copy(i + 1).start()

    @pl.when(i % tiles_per_seq == 0)
    def _():
        prev_scr[f] = jnp.zeros(prev_scr.shape[1:], F32)

    hn = hn_scr[...]
    a = _dot(hn, wa_ref[...])
    conv = _causal_conv3(a, prev_scr[f], cw_ref, cb_ref)
    prev_scr[f] = a[a.shape[0] - SUBLANES:]
    act = (jax.nn.gelu(conv) * _dot(hn, wg_ref[...])).astype(BF16)
    o_ref[...] += _dot(act, wd_ref[...])

    if final_norm:
        @pl.when(f == pl.num_programs(1) - 1)
        def _():
            o_ref[...] = _rmsnorm(o_ref[...], gfin_ref[...])


def _ffn(h1, norm_g, w_up_bf, ffn_conv_w, ffn_conv_b, w_down_bf, norm_final, final_norm,
         seq_len, ff_tile=512):
    n, d = h1.shape
    d_ff = w_down_bf.shape[0]
    nff = d_ff // ff_tile
    tm = FFN_TOKEN_TILE
    assert n % tm == 0 and seq_len % tm == 0 and d_ff % ff_tile == 0
    assert w_up_bf.shape == (d, 2 * d_ff)
    tiles_per_seq = seq_len // tm
    const = lambda i, f: (0, 0)
    return pl.pallas_call(
        functools.partial(_ffn_body, tiles_per_seq, final_norm),
        grid=(n // tm, nff),
        in_specs=[
            pl.BlockSpec(memory_space=pl.ANY),
            pl.BlockSpec((1, d), const),
            pl.BlockSpec((d, ff_tile), lambda i, f: (0, f)),
            pl.BlockSpec((d, ff_tile), lambda i, f: (0, nff + f)),
            pl.BlockSpec((CONV_K, ff_tile), lambda i, f: (0, f)),
            pl.BlockSpec((1, ff_tile), lambda i, f: (0, f)),
            pl.BlockSpec((ff_tile, d), lambda i, f: (f, 0)),
            pl.BlockSpec((1, d), const),
        ],
        out_specs=pl.BlockSpec((tm, d), lambda i, f: (i, 0)),
        out_shape=jax.ShapeDtypeStruct((n, d), F32),
        scratch_shapes=[pltpu.VMEM((tm, d), F32),
                        pltpu.VMEM((tm, d), BF16),
                        pltpu.VMEM((nff, SUBLANES, ff_tile), F32),
                        pltpu.SemaphoreType.DMA(())],
        compiler_params=pltpu.CompilerParams(
            dimension_semantics=("arbitrary", "arbitrary"),
            vmem_limit_bytes=VMEM_LIMIT_BYTES),
        name="ffn",
    )(h1, norm_g, w_up_bf, w_up_bf, ffn_conv_w, ffn_conv_b, w_down_bf, norm_final)


def kernel(x, norm_tok, w_in, a_re, a_im, log_dt, b_re, b_im, c_re, c_im, d_skip, w_glu, w_ssm_out, conv_w, conv_b, w_conv_out, w_o, norm_ffn, w_up, ffn_conv_w, ffn_conv_b, w_down, norm_final):
    batch, seq_len, d = x.shape
    depth = w_in.shape[0]
    h = x.reshape(batch * seq_len, d)
    for l in range(depth):
        n_first = w_in.shape[2] - 2 * d
        w_first_bf = w_in[l, :, :n_first].astype(BF16)
        u, zb, xn, (w_merge_bf, w_glu_bf, w_ssm_bf, w_conv_bf, w_o_bf, w_down_bf) = _inproj(
            h, norm_tok[l][None], w_first_bf, conv_w[l], conv_b[l][None], seq_len,
            [(w_in[l], (n_first // (2 * d), 2 * d)), (w_glu[l], None), (w_ssm_out[l], None),
             (w_conv_out[l], None), (w_o[l], None), (w_down[l], None)])
        ops = _ssm_operators(a_re[l], a_im[l], log_dt[l], b_re[l], b_im[l],
                             c_re[l], c_im[l], d_skip[l])
        ya, (w_up_bf,) = _ssm(u, *ops, batch, seq_len, [w_up[l]])
        merged = _merge(xn, ya, zb, w_glu_bf, w_merge_bf, w_ssm_bf, w_conv_bf)
        h1 = _outproj(h, merged, w_o_bf)
        h = _ffn(h1, norm_ffn[l][None], w_up_bf, ffn_conv_w[l], ffn_conv_b[l][None],
                 w_down_bf, norm_final[None], l == depth - 1, seq_len)
    return h.reshape(batch, seq_len, d)
```

```python
import functools

import jax
import jax.numpy as jnp
from jax import lax
from jax.experimental import pallas as pl
from jax.experimental.pallas import tpu as pltpu

SSM_GROUP = 16
SSM_STATE = 64
CONV_K = 3
EPS = 1e-6

LANES = 128
SUBLANES = 8
BF16_SUBLANES = 16
MXU_DIM = 256
VMEM_LIMIT_BYTES = 56 * 1024 * 1024

TOKEN_TILE = 512
FFN_TOKEN_TILE = 1024
CHUNK = 16
LOOKAHEAD_SLABS = 8
GROUPS_PER_LANE_BLOCK = LANES // SSM_GROUP
STATE_LANES = GROUPS_PER_LANE_BLOCK * SSM_STATE

F32 = jnp.float32
BF16 = jnp.bfloat16


def _rmsnorm(x, gain):
    return x * lax.rsqrt(jnp.mean(x * x, axis=-1, keepdims=True) + EPS) * gain


def _dot(a, b):
    return jnp.dot(a, b, preferred_element_type=F32)


def _causal_conv3(a, prev, w_ref, b_ref):
    w0 = w_ref[0:1, :]
    w1 = w_ref[1:2, :]
    w2 = w_ref[2:3, :]
    bias = b_ref[...]
    full = w2 * a + w1 * pltpu.roll(a, 1, 0) + w0 * pltpu.roll(a, 2, 0) + bias
    top = a[0:SUBLANES]
    rows = lax.broadcasted_iota(jnp.int32, top.shape, 0)
    s1 = jnp.where(rows < 1, pltpu.roll(prev, 1, 0), pltpu.roll(top, 1, 0))
    s2 = jnp.where(rows < 2, pltpu.roll(prev, 2, 0), pltpu.roll(top, 2, 0))
    fix = w2 * top + w1 * s1 + w0 * s2 + bias
    return jnp.concatenate([fix, full[SUBLANES:]], axis=0)


def _cast_riders(src_refs, dst_refs):
    for src, dst in zip(src_refs, dst_refs):
        dst[...] = src[...].astype(BF16)


def _inproj_body(tiles_per_seq, n_cast, x_ref, g_ref, wu_ref, wv_ref, wgb_ref, wgc_ref,
                 cw_ref, cb_ref, *rest):
    cast_src = rest[:n_cast]
    u_ref, zb_ref, xn_ref = rest[n_cast:n_cast + 3]
    cast_dst = rest[n_cast + 3:2 * n_cast + 3]
    prev_ref = rest[2 * n_cast + 3]
    i = pl.program_id(0)

    @pl.when(i % tiles_per_seq == 0)
    def _():
        prev_ref[...] = jnp.zeros_like(prev_ref)

    xn = _rmsnorm(x_ref[...], g_ref[...]).astype(BF16)
    xn_ref[...] = xn
    u_ref[...] = _dot(xn, wu_ref[...])
    cin = _dot(xn, wgc_ref[...]) * _dot(xn, wv_ref[...])
    conv = _causal_conv3(cin, prev_ref[...], cw_ref, cb_ref)
    prev_ref[...] = cin[cin.shape[0] - SUBLANES:]
    zb_ref[...] = (_dot(xn, wgb_ref[...]) * conv).astype(BF16)
    _cast_riders(cast_src, cast_dst)


def _inproj(x2, norm_g, w_in_bf, conv_w, conv_b, seq_len, riders):
    n, d = x2.shape
    cw = conv_w.shape[1]
    assert n % TOKEN_TILE == 0 and seq_len % TOKEN_TILE == 0 and w_in_bf.shape == (d, 4 * cw)
    steps = n // TOKEN_TILE
    tiles_per_seq = seq_len // TOKEN_TILE
    const = lambda i: (0, 0)
    wspec = lambda col: pl.BlockSpec((d, cw), lambda i, col=col: (0, col),
                                     pipeline_mode=pl.Buffered(1))
    cast_in, cast_out, cast_shape = [], [], []
    for w, cols in riders:
        assert w.shape[0] % (steps * BF16_SUBLANES) == 0
        rows = w.shape[0] // steps
        width = w.shape[1] if cols is None else cols[1]
        col = 0 if cols is None else cols[0]
        cast_in.append(pl.BlockSpec((rows, width), lambda i, col=col: (i, col)))
        cast_out.append(pl.BlockSpec((rows, width), lambda i: (i, 0)))
        cast_shape.append(jax.ShapeDtypeStruct((w.shape[0], width), BF16))
    outs = pl.pallas_call(
        functools.partial(_inproj_body, tiles_per_seq, len(riders)),
        grid=(steps,),
        in_specs=[
            pl.BlockSpec((TOKEN_TILE, d), lambda i: (i, 0)),
            pl.BlockSpec((1, d), const),
            wspec(0), wspec(1), wspec(2), wspec(3),
            pl.BlockSpec((CONV_K, cw), const),
            pl.BlockSpec((1, cw), const),
        ] + cast_in,
        out_specs=[
            pl.BlockSpec((TOKEN_TILE, cw), lambda i: (i, 0)),
            pl.BlockSpec((TOKEN_TILE, cw), lambda i: (i, 0)),
            pl.BlockSpec((TOKEN_TILE, d), lambda i: (i, 0)),
        ] + cast_out,
        out_shape=[
            jax.ShapeDtypeStruct((n, cw), F32),
            jax.ShapeDtypeStruct((n, cw), BF16),
            jax.ShapeDtypeStruct((n, d), BF16),
        ] + cast_shape,
        scratch_shapes=[pltpu.VMEM((SUBLANES, cw), F32)],
        compiler_params=pltpu.CompilerParams(
            dimension_semantics=("arbitrary",), vmem_limit_bytes=VMEM_LIMIT_BYTES),
        name="inproj",
    )(x2, norm_g, w_in_bf, w_in_bf, w_in_bf, w_in_bf, conv_w, conv_b, *[w for w, _ in riders])
    return outs[0], outs[1], outs[2], outs[3:]


def _ssm_body(n_cast, u_ref, tz_ref, p_ref, q_ref, pw_ref, *rest):
    cast_src = rest[:n_cast]
    y_ref = rest[n_cast]
    cast_dst = rest[n_cast + 1:2 * n_cast + 1]
    s_scr, xin_scr = rest[2 * n_cast + 1:]
    _cast_riders(cast_src, cast_dst)
    nseq = u_ref.shape[0]
    m = u_ref.shape[1] // CHUNK
    sl = STATE_LANES
    ngl = GROUPS_PER_LANE_BLOCK

    yts = []
    for s in range(nseq):
        pts = [u_ref[s, pl.ds(t, m, stride=CHUNK), :].T.astype(BF16)
               for t in range(CHUNK)]
        ys, s_re, s_im = [], [], []
        for g in range(ngl):
            xg = jnp.concatenate([pt[g * SSM_GROUP:(g + 1) * SSM_GROUP] for pt in pts], axis=0)
            ys.append(_dot(tz_ref[g], xg))
            sg = _dot(p_ref[g], xg)
            s_re.append(sg[:SSM_STATE])
            s_im.append(sg[SSM_STATE:])
        yts.append(ys)
        s_scr[s] = jnp.concatenate(s_re + s_im, axis=0).T

    a16r = pw_ref[0:1, :]
    a16i = pw_ref[1:2, :]

    def step(k, carry):
        out = []
        for s in range(nseq):
            xr, xi = carry[2 * s], carry[2 * s + 1]
            xin_scr[s, pl.ds(k, 1), :sl] = xr
            xin_scr[s, pl.ds(k, 1), sl:] = xi
            row = s_scr[s, pl.ds(k, 1), :]
            out += [a16r * xr - a16i * xi + row[:, :sl], a16r * xi + a16i * xr + row[:, sl:]]
        return tuple(out)

    zero = jnp.zeros((1, sl), F32)
    lax.fori_loop(0, m, step, (zero,) * (2 * nseq), unroll=8)

    for s in range(nseq):
        xint = xin_scr[s].T
        ys = yts[s]
        for g in range(ngl):
            xg = jnp.concatenate([xint[g * SSM_STATE:(g + 1) * SSM_STATE],
                                  xint[sl + g * SSM_STATE:sl + (g + 1) * SSM_STATE]], axis=0)
            ys[g] = ys[g] + _dot(q_ref[g], xg.astype(BF16))
        for t in range(CHUNK):
            yt = jnp.concatenate([y[t * SSM_GROUP:(t + 1) * SSM_GROUP] for y in ys], axis=0)
            y_ref[s, pl.ds(t, m, stride=CHUNK), :] = jax.nn.gelu(yt.T)


def _ssm(u, tz, pm, qm, pw, batch, seq_len, riders, seqs_per_step=2):
    n, width = u.shape
    nblk = width // LANES
    m = seq_len // CHUNK
    ngl = GROUPS_PER_LANE_BLOCK
    u3 = u.reshape(batch, seq_len, width)
    assert seq_len % CHUNK == 0 and width % LANES == 0 and batch % seqs_per_step == 0
    nb = batch // seqs_per_step
    assert all(w.shape[0] % (nblk * nb * BF16_SUBLANES) == 0 for w in riders)
    cast_specs = [pl.BlockSpec((w.shape[0] // (nblk * nb), w.shape[1]),
                               lambda c, b: (c * nb + b, 0)) for w in riders]
    outs = pl.pallas_call(
        functools.partial(_ssm_body, len(riders)),
        grid=(nblk, nb),
        in_specs=[
            pl.BlockSpec((seqs_per_step, seq_len, LANES), lambda c, b: (b, 0, c)),
            pl.BlockSpec((ngl, MXU_DIM, MXU_DIM), lambda c, b: (c, 0, 0)),
            pl.BlockSpec((ngl, 2 * SSM_STATE, MXU_DIM), lambda c, b: (c, 0, 0)),
            pl.BlockSpec((ngl, MXU_DIM, 2 * SSM_STATE), lambda c, b: (c, 0, 0)),
            pl.BlockSpec((None, 2, STATE_LANES), lambda c, b: (c, 0, 0)),
        ] + cast_specs,
        out_specs=[pl.BlockSpec((seqs_per_step, seq_len, LANES), lambda c, b: (b, 0, c))]
        + cast_specs,
        out_shape=[jax.ShapeDtypeStruct((batch, seq_len, width), F32)]
        + [jax.ShapeDtypeStruct(w.shape, BF16) for w in riders],
        scratch_shapes=[pltpu.VMEM((seqs_per_step, m, 2 * STATE_LANES), F32),
                        pltpu.VMEM((seqs_per_step, m, 2 * STATE_LANES), F32)],
        compiler_params=pltpu.CompilerParams(
            dimension_semantics=("arbitrary", "arbitrary"),
            vmem_limit_bytes=VMEM_LIMIT_BYTES),
        name="ssm",
    )(u3, tz, pm, qm, pw, *riders)
    return outs[0].reshape(n, width), outs[1:]


def _ssm_operators(a_re, a_im, log_dt, b_re, b_im, c_re, c_im, d_skip):
    hp = lax.Precision.HIGHEST
    g = a_re.shape[0]
    nblk = g // GROUPS_PER_LANE_BLOCK
    dt = jnp.exp(log_dt)[:, None]
    mag = jnp.exp(dt * a_re)
    abar_re = mag * jnp.cos(dt * a_im)
    abar_im = mag * jnp.sin(dt * a_im)
    nr = abar_re - 1.0
    ni = abar_im
    den = a_re * a_re + a_im * a_im
    fr = (nr * a_re + ni * a_im) / den
    fi = (ni * a_re - nr * a_im) / den
    bb_re = fr[..., None] * b_re - fi[..., None] * b_im
    bb_im = fr[..., None] * b_im + fi[..., None] * b_re
    nn = jnp.arange(CHUNK + 1, dtype=F32)
    pmag = jnp.exp(nn * (dt * a_re)[..., None])
    pw_re = pmag * jnp.cos(nn * (dt * a_im)[..., None])
    pw_im = pmag * jnp.sin(nn * (dt * a_im)[..., None])

    col = jnp.arange(MXU_DIM)
    pick = (CHUNK - 1 - col[None, :] // SSM_GROUP == jnp.arange(CHUNK)[:, None]).astype(F32)
    e = jnp.einsum("rgpn,nc->rgpc", jnp.stack([pw_re, pw_im])[..., :CHUNK], pick,
                   precision=hp)
    spread = (col[None, :] % SSM_GROUP == jnp.arange(SSM_GROUP)[:, None]).astype(F32)
    bx = jnp.einsum("rgpi,ic->rgpc", jnp.stack([bb_re, bb_im]), spread, precision=hp)
    p_re = e[0] * bx[0] - e[1] * bx[1]
    p_im = e[0] * bx[1] + e[1] * bx[0]
    p_ri = jnp.concatenate([p_re, p_im], axis=1)
    pm = p_ri.astype(BF16)

    krev = jnp.einsum("gop,gpc->goc", jnp.concatenate([c_re, -c_im], axis=-1), p_ri,
                      precision=hp)
    lag0 = (CHUNK - 1) * SSM_GROUP + jnp.arange(SSM_GROUP)
    krev = krev + d_skip[..., None] * (col[None, :] == lag0[:, None]).astype(F32)
    padded = jnp.concatenate([krev, jnp.zeros_like(krev)], axis=-1)
    tz = jnp.stack([padded[..., (CHUNK - 1 - t) * SSM_GROUP:(CHUNK - 1 - t) * SSM_GROUP + MXU_DIM]
                    for t in range(CHUNK)], axis=1)
    tz = tz.reshape(g, MXU_DIM, MXU_DIM).astype(BF16)

    o_re = pw_re[..., 1:].transpose(0, 2, 1)[:, :, None, :]
    o_im = pw_im[..., 1:].transpose(0, 2, 1)[:, :, None, :]
    cr = c_re[:, None]
    ci = c_im[:, None]
    qm = jnp.concatenate([cr * o_re - ci * o_im, -(cr * o_im + ci * o_re)], axis=-1)
    qm = qm.reshape(g, MXU_DIM, 2 * SSM_STATE).astype(BF16)

    pw = jnp.stack([pw_re[..., CHUNK], pw_im[..., CHUNK]])
    pw = pw.reshape(2, nblk, STATE_LANES).transpose(1, 0, 2)
    return tz, pm, qm, pw


def _merge_body(col_tile, xn_ref, ya_ref, zb_ref, wglu_ref, wm_ref, wssm_ref, wconv_ref, m_ref):
    d = m_ref.shape[1]
    ya = ya_ref[...]
    ya2 = (ya * jax.nn.sigmoid(_dot(ya.astype(BF16), wglu_ref[...]))).astype(BF16)
    xn = xn_ref[...]
    zb = zb_ref[...]
    for c in range(d // col_tile):
        cols = slice(c * col_tile, (c + 1) * col_tile)
        gate_a = jax.nn.sigmoid(_dot(xn, wm_ref[:, cols]))
        gate_b = jax.nn.sigmoid(_dot(xn, wm_ref[:, d + c * col_tile:d + (c + 1) * col_tile]))
        m_ref[:, cols] = (gate_a * _dot(ya2, wssm_ref[:, cols])
                          + gate_b * _dot(zb, wconv_ref[:, cols])).astype(BF16)


def _merge(xn, ya, zb, w_glu_bf, w_merge_bf, w_ssm_bf, w_conv_bf, col_tile=512):
    n, d = xn.shape
    sw = ya.shape[1]
    cw = zb.shape[1]
    assert n % TOKEN_TILE == 0 and d % col_tile == 0 and w_merge_bf.shape == (d, 2 * d)
    resident = lambda shape: pl.BlockSpec(shape, lambda i: (0, 0), pipeline_mode=pl.Buffered(1))
    return pl.pallas_call(
        functools.partial(_merge_body, col_tile),
        grid=(n // TOKEN_TILE,),
        in_specs=[
            pl.BlockSpec((TOKEN_TILE, d), lambda i: (i, 0)),
            pl.BlockSpec((TOKEN_TILE, sw), lambda i: (i, 0)),
            pl.BlockSpec((TOKEN_TILE, cw), lambda i: (i, 0)),
            resident((sw, sw)),
            resident((d, 2 * d)),
            resident((sw, d)),
            resident((cw, d)),
        ],
        out_specs=pl.BlockSpec((TOKEN_TILE, d), lambda i: (i, 0)),
        out_shape=jax.ShapeDtypeStruct((n, d), BF16),
        compiler_params=pltpu.CompilerParams(
            dimension_semantics=("arbitrary",), vmem_limit_bytes=VMEM_LIMIT_BYTES),
        name="merge",
    )(xn, ya, zb, w_glu_bf, w_merge_bf, w_ssm_bf, w_conv_bf)


def _outproj_body(col_tile, x_ref, m_ref, wo_ref, h_ref):
    m = m_ref[...]
    for c in range(h_ref.shape[1] // col_tile):
        cols = slice(c * col_tile, (c + 1) * col_tile)
        h_ref[:, cols] = x_ref[:, cols] + _dot(m, wo_ref[:, cols])


def _outproj(x2, merged, w_o_bf, col_tile=512):
    n, d = x2.shape
    tm = FFN_TOKEN_TILE
    assert n % tm == 0 and d % col_tile == 0
    return pl.pallas_call(
        functools.partial(_outproj_body, col_tile),
        grid=(n // tm,),
        in_specs=[
            pl.BlockSpec((tm, d), lambda i: (i, 0)),
            pl.BlockSpec((tm, d), lambda i: (i, 0)),
            pl.BlockSpec((d, d), lambda i: (0, 0), pipeline_mode=pl.Buffered(1)),
        ],
        out_specs=pl.BlockSpec((tm, d), lambda i: (i, 0)),
        out_shape=jax.ShapeDtypeStruct((n, d), F32),
        compiler_params=pltpu.CompilerParams(
            dimension_semantics=("arbitrary",), vmem_limit_bytes=VMEM_LIMIT_BYTES),
        name="outproj",
    )(x2, merged, w_o_bf)


def _ffn_body(tiles_per_seq, final_norm, h_hbm, g_ref, wa_ref, wg_ref, cw_ref, cb_ref,
              wd_ref, gfin_ref, o_ref, hbuf, hn_scr, hn_next, prev_scr, sem):
    i = pl.program_id(0)
    f = pl.program_id(1)
    tm = o_ref.shape[0]
    look0 = pl.num_programs(1) - LOOKAHEAD_SLABS
    slab = tm // LOOKAHEAD_SLABS
    has_next = i + 1 < pl.num_programs(0)

    def h_copy(tile):
        rows = pl.ds(pl.multiple_of(tile * tm, tm), tm)
        return pltpu.make_async_copy(h_hbm.at[rows], hbuf, sem)

    @pl.when(f == 0)
    def _():
        @pl.when(i == 0)
        def _():
            h_copy(0).start()
            h_copy(0).wait()
            hn_scr[...] = _rmsnorm(hbuf[...], g_ref[...]).astype(BF16)

        @pl.when(i > 0)
        def _():
            hn_scr[...] = hn_next[...]

        o_ref[...] = hbuf[...]

        @pl.when(has_next)
        def _():
            h_copy(i + 1).start()

    @pl.when((f == look0) & has_next)
    def _():
        h_copy(i + 1).wait()

    @pl.when(i % tiles_per_seq == 0)
    def _():
        prev_scr[f] = jnp.zeros(prev_scr.shape[1:], F32)

    def chunk(lookahead):
        hn = hn_scr[...]
        if lookahead:
            rows = pl.ds(pl.multiple_of((f - look0) * slab, slab), slab)
            hn_next[rows, :] = _rmsnorm(hbuf[rows, :], g_ref[...]).astype(BF16)
        a = _dot(hn, wa_ref[...])
        conv = _causal_conv3(a, prev_scr[f], cw_ref, cb_ref)
        prev_scr[f] = a[a.shape[0] - SUBLANES:]
        act = (jax.nn.gelu(conv) * _dot(hn, wg_ref[...])).astype(BF16)
        o_ref[...] += _dot(act, wd_ref[...])

    pl.when(f < look0)(functools.partial(chunk, False))
    pl.when(f >= look0)(functools.partial(chunk, True))

    if final_norm:
        @pl.when(f == pl.num_programs(1) - 1)
        def _():
            o_ref[...] = _rmsnorm(o_ref[...], gfin_ref[...])


def _ffn(h1, norm_g, w_up_bf, ffn_conv_w, ffn_conv_b, w_down_bf, norm_final, final_norm,
         seq_len, ff_tile=512):
    n, d = h1.shape
    d_ff = w_down_bf.shape[0]
    nff = d_ff // ff_tile
    tm = FFN_TOKEN_TILE
    assert n % tm == 0 and seq_len % tm == 0 and d_ff % ff_tile == 0
    assert w_up_bf.shape == (d, 2 * d_ff) and nff > LOOKAHEAD_SLABS and tm % LOOKAHEAD_SLABS == 0
    tiles_per_seq = seq_len // tm
    const = lambda i, f: (0, 0)
    return pl.pallas_call(
        functools.partial(_ffn_body, tiles_per_seq, final_norm),
        grid=(n // tm, nff),
        in_specs=[
            pl.BlockSpec(memory_space=pl.ANY),
            pl.BlockSpec((1, d), const),
            pl.BlockSpec((d, ff_tile), lambda i, f: (0, f)),
            pl.BlockSpec((d, ff_tile), lambda i, f: (0, nff + f)),
            pl.BlockSpec((CONV_K, ff_tile), lambda i, f: (0, f)),
            pl.BlockSpec((1, ff_tile), lambda i, f: (0, f)),
            pl.BlockSpec((ff_tile, d), lambda i, f: (f, 0)),
            pl.BlockSpec((1, d), const),
        ],
        out_specs=pl.BlockSpec((tm, d), lambda i, f: (i, 0)),
        out_shape=jax.ShapeDtypeStruct((n, d), F32),
        scratch_shapes=[pltpu.VMEM((tm, d), F32),
                        pltpu.VMEM((tm, d), BF16),
                        pltpu.VMEM((tm, d), BF16),
                        pltpu.VMEM((nff, SUBLANES, ff_tile), F32),
                        pltpu.SemaphoreType.DMA(())],
        compiler_params=pltpu.CompilerParams(
            dimension_semantics=("arbitrary", "arbitrary"),
            vmem_limit_bytes=VMEM_LIMIT_BYTES),
        name="ffn",
    )(h1, norm_g, w_up_bf, w_up_bf, ffn_conv_w, ffn_conv_b, w_down_bf, norm_final)


def kernel(x, norm_tok, w_in, a_re, a_im, log_dt, b_re, b_im, c_re, c_im, d_skip, w_glu, w_ssm_out, conv_w, conv_b, w_conv_out, w_o, norm_ffn, w_up, ffn_conv_w, ffn_conv_b, w_down, norm_final):
    batch, seq_len, d = x.shape
    depth = w_in.shape[0]
    h = x.reshape(batch * seq_len, d)
    for l in range(depth):
        n_first = w_in.shape[2] - 2 * d
        w_first_bf = w_in[l, :, :n_first].astype(BF16)
        u, zb, xn, (w_merge_bf, w_glu_bf, w_ssm_bf, w_conv_bf, w_o_bf, w_down_bf) = _inproj(
            h, norm_tok[l][None], w_first_bf, conv_w[l], conv_b[l][None], seq_len,
            [(w_in[l], (n_first // (2 * d), 2 * d)), (w_glu[l], None), (w_ssm_out[l], None),
             (w_conv_out[l], None), (w_o[l], None), (w_down[l], None)])
        ops = _ssm_operators(a_re[l], a_im[l], log_dt[l], b_re[l], b_im[l],
                             c_re[l], c_im[l], d_skip[l])
        ya, (w_up_bf,) = _ssm(u, *ops, batch, seq_len, [w_up[l]])
        merged = _merge(xn, ya, zb, w_glu_bf, w_merge_bf, w_ssm_bf, w_conv_bf)
        h1 = _outproj(h, merged, w_o_bf)
        h = _ffn(h1, norm_ffn[l][None], w_up_bf, ffn_conv_w[l], ffn_conv_b[l][None],
                 w_down_bf, norm_final[None], l == depth - 1, seq_len)
    return h.reshape(batch, seq_len, d)
```

```python
import functools

import jax
import jax.numpy as jnp
from jax import lax
from jax.experimental import pallas as pl
from jax.experimental.pallas import tpu as pltpu

SSM_GROUP = 16
SSM_STATE = 64
CONV_K = 3
EPS = 1e-6

LANES = 128
SUBLANES = 8
BF16_SUBLANES = 16
MXU_DIM = 256
VMEM_LIMIT_BYTES = 56 * 1024 * 1024

TOKEN_TILE = 512
FFN_TOKEN_TILE = 1024
CHUNK = 16
GROUPS_PER_LANE_BLOCK = LANES // SSM_GROUP
STATE_LANES = GROUPS_PER_LANE_BLOCK * SSM_STATE

F32 = jnp.float32
BF16 = jnp.bfloat16


def _rmsnorm(x, gain):
    return x * lax.rsqrt(jnp.mean(x * x, axis=-1, keepdims=True) + EPS) * gain


def _dot(a, b):
    return jnp.dot(a, b, preferred_element_type=F32)


def _causal_conv3(a, prev, w_ref, b_ref):
    w0 = w_ref[0:1, :]
    w1 = w_ref[1:2, :]
    w2 = w_ref[2:3, :]
    bias = b_ref[...]
    full = w2 * a + w1 * pltpu.roll(a, 1, 0) + w0 * pltpu.roll(a, 2, 0) + bias
    top = a[0:SUBLANES]
    rows = lax.broadcasted_iota(jnp.int32, top.shape, 0)
    s1 = jnp.where(rows < 1, pltpu.roll(prev, 1, 0), pltpu.roll(top, 1, 0))
    s2 = jnp.where(rows < 2, pltpu.roll(prev, 2, 0), pltpu.roll(top, 2, 0))
    fix = w2 * top + w1 * s1 + w0 * s2 + bias
    return jnp.concatenate([fix, full[SUBLANES:]], axis=0)


def _cast_riders(src_refs, dst_refs):
    for src, dst in zip(src_refs, dst_refs):
        dst[...] = src[...].astype(BF16)


def _inproj_body(tiles_per_seq, n_cast, x_ref, g_ref, wu_ref, wv_ref, wgb_ref, wgc_ref,
                 cw_ref, cb_ref, *rest):
    cast_src = rest[:n_cast]
    u_ref, zb_ref, xn_ref = rest[n_cast:n_cast + 3]
    cast_dst = rest[n_cast + 3:2 * n_cast + 3]
    prev_ref = rest[2 * n_cast + 3]
    i = pl.program_id(0)

    @pl.when(i % tiles_per_seq == 0)
    def _():
        prev_ref[...] = jnp.zeros_like(prev_ref)

    xn = _rmsnorm(x_ref[...], g_ref[...]).astype(BF16)
    xn_ref[...] = xn
    u_ref[...] = _dot(xn, wu_ref[...])
    cin = _dot(xn, wgc_ref[...]) * _dot(xn, wv_ref[...])
    conv = _causal_conv3(cin, prev_ref[...], cw_ref, cb_ref)
    prev_ref[...] = cin[cin.shape[0] - SUBLANES:]
    zb_ref[...] = (_dot(xn, wgb_ref[...]) * conv).astype(BF16)
    _cast_riders(cast_src, cast_dst)


def _inproj(x2, norm_g, w_in_bf, conv_w, conv_b, seq_len, riders):
    n, d = x2.shape
    cw = conv_w.shape[1]
    assert n % TOKEN_TILE == 0 and seq_len % TOKEN_TILE == 0 and w_in_bf.shape == (d, 4 * cw)
    steps = n // TOKEN_TILE
    tiles_per_seq = seq_len // TOKEN_TILE
    const = lambda i: (0, 0)
    wspec = lambda col: pl.BlockSpec((d, cw), lambda i, col=col: (0, col),
                                     pipeline_mode=pl.Buffered(1))
    cast_in, cast_out, cast_shape = [], [], []
    for w, cols in riders:
        assert w.shape[0] % (steps * BF16_SUBLANES) == 0
        rows = w.shape[0] // steps
        width = w.shape[1] if cols is None else cols[1]
        col = 0 if cols is None else cols[0]
        cast_in.append(pl.BlockSpec((rows, width), lambda i, col=col: (i, col)))
        cast_out.append(pl.BlockSpec((rows, width), lambda i: (i, 0)))
        cast_shape.append(jax.ShapeDtypeStruct((w.shape[0], width), BF16))
    outs = pl.pallas_call(
        functools.partial(_inproj_body, tiles_per_seq, len(riders)),
        grid=(steps,),
        in_specs=[
            pl.BlockSpec((TOKEN_TILE, d), lambda i: (i, 0)),
            pl.BlockSpec((1, d), const),
            wspec(0), wspec(1), wspec(2), wspec(3),
            pl.BlockSpec((CONV_K, cw), const),
            pl.BlockSpec((1, cw), const),
        ] + cast_in,
        out_specs=[
            pl.BlockSpec((TOKEN_TILE, cw), lambda i: (i, 0)),
            pl.BlockSpec((TOKEN_TILE, cw), lambda i: (i, 0)),
            pl.BlockSpec((TOKEN_TILE, d), lambda i: (i, 0)),
        ] + cast_out,
        out_shape=[
            jax.ShapeDtypeStruct((n, cw), F32),
            jax.ShapeDtypeStruct((n, cw), BF16),
            jax.ShapeDtypeStruct((n, d), BF16),
        ] + cast_shape,
        scratch_shapes=[pltpu.VMEM((SUBLANES, cw), F32)],
        compiler_params=pltpu.CompilerParams(
            dimension_semantics=("arbitrary",), vmem_limit_bytes=VMEM_LIMIT_BYTES),
        name="inproj",
    )(x2, norm_g, w_in_bf, w_in_bf, w_in_bf, w_in_bf, conv_w, conv_b, *[w for w, _ in riders])
    return outs[0], outs[1], outs[2], outs[3:]


def _ssm_body(n_cast, u_ref, tz_ref, p_ref, q_ref, pw_ref, *rest):
    cast_src = rest[:n_cast]
    y_ref = rest[n_cast]
    cast_dst = rest[n_cast + 1:2 * n_cast + 1]
    s_scr, xin_scr = rest[2 * n_cast + 1:]
    _cast_riders(cast_src, cast_dst)
    nseq = u_ref.shape[0]
    m = u_ref.shape[1] // CHUNK
    sl = STATE_LANES
    ngl = GROUPS_PER_LANE_BLOCK

    eye = (lax.broadcasted_iota(jnp.int32, (LANES, LANES), 0)
           == lax.broadcasted_iota(jnp.int32, (LANES, LANES), 1)).astype(BF16)

    def transposed(piece):
        return lax.dot_general(eye, piece.astype(BF16), (((1,), (1,)), ((), ())),
                               preferred_element_type=F32).astype(BF16)

    yts = []
    for s in range(nseq):
        pts = [transposed(u_ref[s, pl.ds(t, m, stride=CHUNK), :])
               for t in range(CHUNK)]
        ys, s_re, s_im = [], [], []
        for g in range(ngl):
            xg = jnp.concatenate([pt[g * SSM_GROUP:(g + 1) * SSM_GROUP] for pt in pts], axis=0)
            ys.append(_dot(tz_ref[g], xg))
            sg = _dot(p_ref[g], xg)
            s_re.append(sg[:SSM_STATE])
            s_im.append(sg[SSM_STATE:])
        yts.append(ys)
        s_scr[s] = jnp.concatenate(s_re + s_im, axis=0).T

    a16r = pw_ref[0:1, :]
    a16i = pw_ref[1:2, :]

    def step(k, carry):
        out = []
        for s in range(nseq):
            xr, xi = carry[2 * s], carry[2 * s + 1]
            xin_scr[s, pl.ds(k, 1), :sl] = xr
            xin_scr[s, pl.ds(k, 1), sl:] = xi
            row = s_scr[s, pl.ds(k, 1), :]
            out += [a16r * xr - a16i * xi + row[:, :sl], a16r * xi + a16i * xr + row[:, sl:]]
        return tuple(out)

    zero = jnp.zeros((1, sl), F32)
    lax.fori_loop(0, m, step, (zero,) * (2 * nseq), unroll=8)

    for s in range(nseq):
        xint = xin_scr[s].T
        ys = yts[s]
        for g in range(ngl):
            xg = jnp.concatenate([xint[g * SSM_STATE:(g + 1) * SSM_STATE],
                                  xint[sl + g * SSM_STATE:sl + (g + 1) * SSM_STATE]], axis=0)
            ys[g] = ys[g] + _dot(q_ref[g], xg.astype(BF16))
        for t in range(CHUNK):
            yt = jnp.concatenate([y[t * SSM_GROUP:(t + 1) * SSM_GROUP] for y in ys], axis=0)
            y_ref[s, pl.ds(t, m, stride=CHUNK), :] = jax.nn.gelu(yt.T)


def _ssm(u, tz, pm, qm, pw, batch, seq_len, riders, seqs_per_step=2):
    n, width = u.shape
    nblk = width // LANES
    m = seq_len // CHUNK
    ngl = GROUPS_PER_LANE_BLOCK
    u3 = u.reshape(batch, seq_len, width)
    assert seq_len % CHUNK == 0 and width % LANES == 0 and batch % seqs_per_step == 0
    nb = batch // seqs_per_step
    assert all(w.shape[0] % (nblk * nb * BF16_SUBLANES) == 0 for w in riders)
    cast_specs = [pl.BlockSpec((w.shape[0] // (nblk * nb), w.shape[1]),
                               lambda c, b: (c * nb + b, 0)) for w in riders]
    outs = pl.pallas_call(
        functools.partial(_ssm_body, len(riders)),
        grid=(nblk, nb),
        in_specs=[
            pl.BlockSpec((seqs_per_step, seq_len, LANES), lambda c, b: (b, 0, c)),
            pl.BlockSpec((ngl, MXU_DIM, MXU_DIM), lambda c, b: (c, 0, 0)),
            pl.BlockSpec((ngl, 2 * SSM_STATE, MXU_DIM), lambda c, b: (c, 0, 0)),
            pl.BlockSpec((ngl, MXU_DIM, 2 * SSM_STATE), lambda c, b: (c, 0, 0)),
            pl.BlockSpec((None, 2, STATE_LANES), lambda c, b: (c, 0, 0)),
        ] + cast_specs,
        out_specs=[pl.BlockSpec((seqs_per_step, seq_len, LANES), lambda c, b: (b, 0, c))]
        + cast_specs,
        out_shape=[jax.ShapeDtypeStruct((batch, seq_len, width), F32)]
        + [jax.ShapeDtypeStruct(w.shape, BF16) for w in riders],
        scratch_shapes=[pltpu.VMEM((seqs_per_step, m, 2 * STATE_LANES), F32),
                        pltpu.VMEM((seqs_per_step, m, 2 * STATE_LANES), F32)],
        compiler_params=pltpu.CompilerParams(
            dimension_semantics=("arbitrary", "arbitrary"),
            vmem_limit_bytes=VMEM_LIMIT_BYTES),
        name="ssm",
    )(u3, tz, pm, qm, pw, *riders)
    return outs[0].reshape(n, width), outs[1:]


def _ssm_operators(a_re, a_im, log_dt, b_re, b_im, c_re, c_im, d_skip):
    hp = lax.Precision.HIGHEST
    g = a_re.shape[0]
    nblk = g // GROUPS_PER_LANE_BLOCK
    dt = jnp.exp(log_dt)[:, None]
    mag = jnp.exp(dt * a_re)
    abar_re = mag * jnp.cos(dt * a_im)
    abar_im = mag * jnp.sin(dt * a_im)
    nr = abar_re - 1.0
    ni = abar_im
    den = a_re * a_re + a_im * a_im
    fr = (nr * a_re + ni * a_im) / den
    fi = (ni * a_re - nr * a_im) / den
    bb_re = fr[..., None] * b_re - fi[..., None] * b_im
    bb_im = fr[..., None] * b_im + fi[..., None] * b_re
    nn = jnp.arange(CHUNK + 1, dtype=F32)
    pmag = jnp.exp(nn * (dt * a_re)[..., None])
    pw_re = pmag * jnp.cos(nn * (dt * a_im)[..., None])
    pw_im = pmag * jnp.sin(nn * (dt * a_im)[..., None])

    col = jnp.arange(MXU_DIM)
    pick = (CHUNK - 1 - col[None, :] // SSM_GROUP == jnp.arange(CHUNK)[:, None]).astype(F32)
    e = jnp.einsum("rgpn,nc->rgpc", jnp.stack([pw_re, pw_im])[..., :CHUNK], pick,
                   precision=hp)
    spread = (col[None, :] % SSM_GROUP == jnp.arange(SSM_GROUP)[:, None]).astype(F32)
    bx = jnp.einsum("rgpi,ic->rgpc", jnp.stack([bb_re, bb_im]), spread, precision=hp)
    p_re = e[0] * bx[0] - e[1] * bx[1]
    p_im = e[0] * bx[1] + e[1] * bx[0]
    p_ri = jnp.concatenate([p_re, p_im], axis=1)
    pm = p_ri.astype(BF16)

    krev = jnp.einsum("gop,gpc->goc", jnp.concatenate([c_re, -c_im], axis=-1), p_ri,
                      precision=hp)
    lag0 = (CHUNK - 1) * SSM_GROUP + jnp.arange(SSM_GROUP)
    krev = krev + d_skip[..., None] * (col[None, :] == lag0[:, None]).astype(F32)
    padded = jnp.concatenate([krev, jnp.zeros_like(krev)], axis=-1)
    tz = jnp.stack([padded[..., (CHUNK - 1 - t) * SSM_GROUP:(CHUNK - 1 - t) * SSM_GROUP + MXU_DIM]
                    for t in range(CHUNK)], axis=1)
    tz = tz.reshape(g, MXU_DIM, MXU_DIM).astype(BF16)

    o_re = pw_re[..., 1:].transpose(0, 2, 1)[:, :, None, :]
    o_im = pw_im[..., 1:].transpose(0, 2, 1)[:, :, None, :]
    cr = c_re[:, None]
    ci = c_im[:, None]
    qm = jnp.concatenate([cr * o_re - ci * o_im, -(cr * o_im + ci * o_re)], axis=-1)
    qm = qm.reshape(g, MXU_DIM, 2 * SSM_STATE).astype(BF16)

    pw = jnp.stack([pw_re[..., CHUNK], pw_im[..., CHUNK]])
    pw = pw.reshape(2, nblk, STATE_LANES).transpose(1, 0, 2)
    return tz, pm, qm, pw


def _merge_body(col_tile, xn_ref, ya_ref, zb_ref, wglu_ref, wm_ref, wssm_ref, wconv_ref, m_ref):
    d = m_ref.shape[1]
    ya = ya_ref[...]
    ya2 = (ya * jax.nn.sigmoid(_dot(ya.astype(BF16), wglu_ref[...]))).astype(BF16)
    xn = xn_ref[...]
    zb = zb_ref[...]
    for c in range(d // col_tile):
        cols = slice(c * col_tile, (c + 1) * col_tile)
        gate_a = jax.nn.sigmoid(_dot(xn, wm_ref[:, cols]))
        gate_b = jax.nn.sigmoid(_dot(xn, wm_ref[:, d + c * col_tile:d + (c + 1) * col_tile]))
        m_ref[:, cols] = (gate_a * _dot(ya2, wssm_ref[:, cols])
                          + gate_b * _dot(zb, wconv_ref[:, cols])).astype(BF16)


def _merge(xn, ya, zb, w_glu_bf, w_merge_bf, w_ssm_bf, w_conv_bf, col_tile=512):
    n, d = xn.shape
    sw = ya.shape[1]
    cw = zb.shape[1]
    assert n % TOKEN_TILE == 0 and d % col_tile == 0 and w_merge_bf.shape == (d, 2 * d)
    resident = lambda shape: pl.BlockSpec(shape, lambda i: (0, 0), pipeline_mode=pl.Buffered(1))
    return pl.pallas_call(
        functools.partial(_merge_body, col_tile),
        grid=(n // TOKEN_TILE,),
        in_specs=[
            pl.BlockSpec((TOKEN_TILE, d), lambda i: (i, 0)),
            pl.BlockSpec((TOKEN_TILE, sw), lambda i: (i, 0)),
            pl.BlockSpec((TOKEN_TILE, cw), lambda i: (i, 0)),
            resident((sw, sw)),
            resident((d, 2 * d)),
            resident((sw, d)),
            resident((cw, d)),
        ],
        out_specs=pl.BlockSpec((TOKEN_TILE, d), lambda i: (i, 0)),
        out_shape=jax.ShapeDtypeStruct((n, d), BF16),
        compiler_params=pltpu.CompilerParams(
            dimension_semantics=("arbitrary",), vmem_limit_bytes=VMEM_LIMIT_BYTES),
        name="merge",
    )(xn, ya, zb, w_glu_bf, w_merge_bf, w_ssm_bf, w_conv_bf)


def _outproj_body(col_tile, x_ref, m_ref, wo_ref, h_ref):
    m = m_ref[...]
    for c in range(h_ref.shape[1] // col_tile):
        cols = slice(c * col_tile, (c + 1) * col_tile)
        h_ref[:, cols] = x_ref[:, cols] + _dot(m, wo_ref[:, cols])


def _outproj(x2, merged, w_o_bf, col_tile=512):
    n, d = x2.shape
    tm = FFN_TOKEN_TILE
    assert n % tm == 0 and d % col_tile == 0
    return pl.pallas_call(
        functools.partial(_outproj_body, col_tile),
        grid=(n // tm,),
        in_specs=[
            pl.BlockSpec((tm, d), lambda i: (i, 0)),
            pl.BlockSpec((tm, d), lambda i: (i, 0)),
            pl.BlockSpec((d, d), lambda i: (0, 0), pipeline_mode=pl.Buffered(1)),
        ],
        out_specs=pl.BlockSpec((tm, d), lambda i: (i, 0)),
        out_shape=jax.ShapeDtypeStruct((n, d), F32),
        compiler_params=pltpu.CompilerParams(
            dimension_semantics=("arbitrary",), vmem_limit_bytes=VMEM_LIMIT_BYTES),
        name="outproj",
    )(x2, merged, w_o_bf)


def _ffn_body(tiles_per_seq, final_norm, h_hbm, g_ref, wa_ref, wg_ref, cw_ref, cb_ref,
              wd_ref, gfin_ref, o_ref, hbuf, hn_scr, prev_scr, sem):
    i = pl.program_id(0)
    f = pl.program_id(1)
    tm = o_ref.shape[0]

    def h_copy(tile):
        rows = pl.ds(pl.multiple_of(tile * tm, tm), tm)
        return pltpu.make_async_copy(h_hbm.at[rows], hbuf, sem)

    @pl.when(f == 0)
    def _():
        @pl.when(i == 0)
        def _():
            h_copy(0).start()

        h_copy(i).wait()
        h = hbuf[...]
        hn_scr[...] = _rmsnorm(h, g_ref[...]).astype(BF16)
        o_ref[...] = h

        @pl.when(i + 1 < pl.num_programs(0))
        def _():
            h_copy(i + 1).start()

    @pl.when(i % tiles_per_seq == 0)
    def _():
        prev_scr[f] = jnp.zeros(prev_scr.shape[1:], F32)

    hn = hn_scr[...]
    a = _dot(hn, wa_ref[...])
    conv = _causal_conv3(a, prev_scr[f], cw_ref, cb_ref)
    prev_scr[f] = a[a.shape[0] - SUBLANES:]
    act = (jax.nn.gelu(conv) * _dot(hn, wg_ref[...])).astype(BF16)
    o_ref[...] += _dot(act, wd_ref[...])

    if final_norm:
        @pl.when(f == pl.num_programs(1) - 1)
        def _():
            o_ref[...] = _rmsnorm(o_ref[...], gfin_ref[...])


def _ffn(h1, norm_g, w_up_bf, ffn_conv_w, ffn_conv_b, w_down_bf, norm_final, final_norm,
         seq_len, ff_tile=512):
    n, d = h1.shape
    d_ff = w_down_bf.shape[0]
    nff = d_ff // ff_tile
    tm = FFN_TOKEN_TILE
    assert n % tm == 0 and seq_len % tm == 0 and d_ff % ff_tile == 0
    assert w_up_bf.shape == (d, 2 * d_ff)
    tiles_per_seq = seq_len // tm
    const = lambda i, f: (0, 0)
    return pl.pallas_call(
        functools.partial(_ffn_body, tiles_per_seq, final_norm),
        grid=(n // tm, nff),
        in_specs=[
            pl.BlockSpec(memory_space=pl.ANY),
            pl.BlockSpec((1, d), const),
            pl.BlockSpec((d, ff_tile), lambda i, f: (0, f)),
            pl.BlockSpec((d, ff_tile), lambda i, f: (0, nff + f)),
            pl.BlockSpec((CONV_K, ff_tile), lambda i, f: (0, f)),
            pl.BlockSpec((1, ff_tile), lambda i, f: (0, f)),
            pl.BlockSpec((ff_tile, d), lambda i, f: (f, 0)),
            pl.BlockSpec((1, d), const),
        ],
        out_specs=pl.BlockSpec((tm, d), lambda i, f: (i, 0)),
        out_shape=jax.ShapeDtypeStruct((n, d), F32),
        scratch_shapes=[pltpu.VMEM((tm, d), F32),
                        pltpu.VMEM((tm, d), BF16),
                        pltpu.VMEM((nff, SUBLANES, ff_tile), F32),
                        pltpu.SemaphoreType.DMA(())],
        compiler_params=pltpu.CompilerParams(
            dimension_semantics=("arbitrary", "arbitrary"),
            vmem_limit_bytes=VMEM_LIMIT_BYTES),
        name="ffn",
    )(h1, norm_g, w_up_bf, w_up_bf, ffn_conv_w, ffn_conv_b, w_down_bf, norm_final)


def kernel(x, norm_tok, w_in, a_re, a_im, log_dt, b_re, b_im, c_re, c_im, d_skip, w_glu, w_ssm_out, conv_w, conv_b, w_conv_out, w_o, norm_ffn, w_up, ffn_conv_w, ffn_conv_b, w_down, norm_final):
    batch, seq_len, d = x.shape
    depth = w_in.shape[0]
    h = x.reshape(batch * seq_len, d)
    for l in range(depth):
        n_first = w_in.shape[2] - 2 * d
        w_first_bf = w_in[l, :, :n_first].astype(BF16)
        u, zb, xn, (w_merge_bf, w_glu_bf, w_ssm_bf, w_conv_bf, w_o_bf, w_down_bf) = _inproj(
            h, norm_tok[l][None], w_first_bf, conv_w[l], conv_b[l][None], seq_len,
            [(w_in[l], (n_first // (2 * d), 2 * d)), (w_glu[l], None), (w_ssm_out[l], None),
             (w_conv_out[l], None), (w_o[l], None), (w_down[l], None)])
        ops = _ssm_operators(a_re[l], a_im[l], log_dt[l], b_re[l], b_im[l],
                             c_re[l], c_im[l], d_skip[l])
        ya, (w_up_bf,) = _ssm(u, *ops, batch, seq_len, [w_up[l]])
        merged = _merge(xn, ya, zb, w_glu_bf, w_merge_bf, w_ssm_bf, w_conv_bf)
        h1 = _outproj(h, merged, w_o_bf)
        h = _ffn(h1, norm_ffn[l][None], w_up_bf, ffn_conv_w[l], ffn_conv_b[l][None],
                 w_down_bf, norm_final[None], l == depth - 1, seq_len)
    return h.reshape(batch, seq_len, d)
```

```python
import functools

import jax
import jax.numpy as jnp
from jax import lax
from jax.experimental import pallas as pl
from jax.experimental.pallas import tpu as pltpu

SSM_GROUP = 16
SSM_STATE = 64
CONV_K = 3
EPS = 1e-6

LANES = 128
SUBLANES = 8
BF16_SUBLANES = 16
MXU_DIM = 256
VMEM_LIMIT_BYTES = 56 * 1024 * 1024

TOKEN_TILE = 512
FFN_TOKEN_TILE = 1024
CHUNK = 16
GROUPS_PER_LANE_BLOCK = LANES // SSM_GROUP
STATE_LANES = GROUPS_PER_LANE_BLOCK * SSM_STATE

F32 = jnp.float32
BF16 = jnp.bfloat16


def _rmsnorm(x, gain):
    return x * lax.rsqrt(jnp.mean(x * x, axis=-1, keepdims=True) + EPS) * gain


def _dot(a, b):
    return jnp.dot(a, b, preferred_element_type=F32)


def _causal_conv3(a, prev, w_ref, b_ref):
    w0 = w_ref[0:1, :]
    w1 = w_ref[1:2, :]
    w2 = w_ref[2:3, :]
    bias = b_ref[...]
    full = w2 * a + w1 * pltpu.roll(a, 1, 0) + w0 * pltpu.roll(a, 2, 0) + bias
    top = a[0:SUBLANES]
    rows = lax.broadcasted_iota(jnp.int32, top.shape, 0)
    s1 = jnp.where(rows < 1, pltpu.roll(prev, 1, 0), pltpu.roll(top, 1, 0))
    s2 = jnp.where(rows < 2, pltpu.roll(prev, 2, 0), pltpu.roll(top, 2, 0))
    fix = w2 * top + w1 * s1 + w0 * s2 + bias
    return jnp.concatenate([fix, full[SUBLANES:]], axis=0)


def _cast_riders(src_refs, dst_refs):
    for src, dst in zip(src_refs, dst_refs):
        dst[...] = src[...].astype(BF16)


def _inproj_body(tiles_per_seq, n_cast, x_ref, g_ref, wu_ref, wv_ref, wgb_ref, wgc_ref,
                 cw_ref, cb_ref, *rest):
    cast_src = rest[:n_cast]
    u_ref, zb_ref, xn_ref = rest[n_cast:n_cast + 3]
    cast_dst = rest[n_cast + 3:2 * n_cast + 3]
    prev_ref = rest[2 * n_cast + 3]
    i = pl.program_id(0)

    @pl.when(i % tiles_per_seq == 0)
    def _():
        prev_ref[...] = jnp.zeros_like(prev_ref)

    xn = _rmsnorm(x_ref[...], g_ref[...]).astype(BF16)
    xn_ref[...] = xn
    u_ref[...] = _dot(xn, wu_ref[...])
    cin = _dot(xn, wgc_ref[...]) * _dot(xn, wv_ref[...])
    conv = _causal_conv3(cin, prev_ref[...], cw_ref, cb_ref)
    prev_ref[...] = cin[cin.shape[0] - SUBLANES:]
    zb_ref[...] = (_dot(xn, wgb_ref[...]) * conv).astype(BF16)
    _cast_riders(cast_src, cast_dst)


def _inproj(x2, norm_g, w_in_bf, conv_w, conv_b, seq_len, riders):
    n, d = x2.shape
    cw = conv_w.shape[1]
    assert n % TOKEN_TILE == 0 and seq_len % TOKEN_TILE == 0 and w_in_bf.shape == (d, 4 * cw)
    steps = n // TOKEN_TILE
    tiles_per_seq = seq_len // TOKEN_TILE
    const = lambda i: (0, 0)
    wspec = lambda col: pl.BlockSpec((d, cw), lambda i, col=col: (0, col),
                                     pipeline_mode=pl.Buffered(1))
    cast_in, cast_out, cast_shape = [], [], []
    for w, cols in riders:
        assert w.shape[0] % (steps * BF16_SUBLANES) == 0
        rows = w.shape[0] // steps
        width = w.shape[1] if cols is None else cols[1]
        col = 0 if cols is None else cols[0]
        cast_in.append(pl.BlockSpec((rows, width), lambda i, col=col: (i, col)))
        cast_out.append(pl.BlockSpec((rows, width), lambda i: (i, 0)))
        cast_shape.append(jax.ShapeDtypeStruct((w.shape[0], width), BF16))
    outs = pl.pallas_call(
        functools.partial(_inproj_body, tiles_per_seq, len(riders)),
        grid=(steps,),
        in_specs=[
            pl.BlockSpec((TOKEN_TILE, d), lambda i: (i, 0)),
            pl.BlockSpec((1, d), const),
            wspec(0), wspec(1), wspec(2), wspec(3),
            pl.BlockSpec((CONV_K, cw), const),
            pl.BlockSpec((1, cw), const),
        ] + cast_in,
        out_specs=[
            pl.BlockSpec((TOKEN_TILE, cw), lambda i: (i, 0)),
            pl.BlockSpec((TOKEN_TILE, cw), lambda i: (i, 0)),
            pl.BlockSpec((TOKEN_TILE, d), lambda i: (i, 0)),
        ] + cast_out,
        out_shape=[
            jax.ShapeDtypeStruct((n, cw), F32),
            jax.ShapeDtypeStruct((n, cw), BF16),
            jax.ShapeDtypeStruct((n, d), BF16),
        ] + cast_shape,
        scratch_shapes=[pltpu.VMEM((SUBLANES, cw), F32)],
        compiler_params=pltpu.CompilerParams(
            dimension_semantics=("arbitrary",), vmem_limit_bytes=VMEM_LIMIT_BYTES),
        name="inproj",
    )(x2, norm_g, w_in_bf, w_in_bf, w_in_bf, w_in_bf, conv_w, conv_b, *[w for w, _ in riders])
    return outs[0], outs[1], outs[2], outs[3:]


def _ssm_body(n_cast, u_ref, tz_ref, p_ref, q_ref, pw_ref, *rest):
    cast_src = rest[:n_cast]
    y_ref = rest[n_cast]
    cast_dst = rest[n_cast + 1:2 * n_cast + 1]
    s_scr, xin_scr = rest[2 * n_cast + 1:]
    _cast_riders(cast_src, cast_dst)
    nseq = u_ref.shape[0]
    m = u_ref.shape[1] // CHUNK
    sl = STATE_LANES
    ngl = GROUPS_PER_LANE_BLOCK

    yts = []
    for s in range(nseq):
        pts = [u_ref[s, pl.ds(t, m, stride=CHUNK), :].T.astype(BF16)
               for t in range(CHUNK)]
        ys, s_re, s_im = [], [], []
        for g in range(ngl):
            xg = jnp.concatenate([pt[g * SSM_GROUP:(g + 1) * SSM_GROUP] for pt in pts], axis=0)
            ys.append(_dot(tz_ref[g], xg))
            sg = _dot(p_ref[g], xg)
            s_re.append(sg[:SSM_STATE])
            s_im.append(sg[SSM_STATE:])
        yts.append(ys)
        s_scr[s] = jnp.concatenate(s_re + s_im, axis=0).T

    a16r = pw_ref[0:1, :]
    a16i = pw_ref[1:2, :]

    def step(k, carry):
        out = []
        for s in range(nseq):
            xr, xi = carry[2 * s], carry[2 * s + 1]
            xin_scr[s, pl.ds(k, 1), :sl] = xr
            xin_scr[s, pl.ds(k, 1), sl:] = xi
            row = s_scr[s, pl.ds(k, 1), :]
            out += [a16r * xr - a16i * xi + row[:, :sl], a16r * xi + a16i * xr + row[:, sl:]]
        return tuple(out)

    zero = jnp.zeros((1, sl), F32)
    lax.fori_loop(0, m, step, (zero,) * (2 * nseq), unroll=8)

    for s in range(nseq):
        xint = xin_scr[s].T
        ys = yts[s]
        for g in range(ngl):
            xg = jnp.concatenate([xint[g * SSM_STATE:(g + 1) * SSM_STATE],
                                  xint[sl + g * SSM_STATE:sl + (g + 1) * SSM_STATE]], axis=0)
            ys[g] = ys[g] + _dot(q_ref[g], xg.astype(BF16))
        for t in range(CHUNK):
            yt = jnp.concatenate([y[t * SSM_GROUP:(t + 1) * SSM_GROUP] for y in ys], axis=0)
            y_ref[s, pl.ds(t, m, stride=CHUNK), :] = jax.nn.gelu(yt.T)


def _ssm(u, tz, pm, qm, pw, batch, seq_len, riders, seqs_per_step=2):
    n, width = u.shape
    nblk = width // LANES
    m = seq_len // CHUNK
    ngl = GROUPS_PER_LANE_BLOCK
    u3 = u.reshape(batch, seq_len, width)
    assert seq_len % CHUNK == 0 and width % LANES == 0 and batch % seqs_per_step == 0
    nb = batch // seqs_per_step
    assert all(w.shape[0] % (nblk * nb * BF16_SUBLANES) == 0 for w in riders)
    cast_specs = [pl.BlockSpec((w.shape[0] // (nblk * nb), w.shape[1]),
                               lambda c, b: (c * nb + b, 0)) for w in riders]
    outs = pl.pallas_call(
        functools.partial(_ssm_body, len(riders)),
        grid=(nblk, nb),
        in_specs=[
            pl.BlockSpec((seqs_per_step, seq_len, LANES), lambda c, b: (b, 0, c)),
            pl.BlockSpec((ngl, MXU_DIM, MXU_DIM), lambda c, b: (c, 0, 0)),
            pl.BlockSpec((ngl, 2 * SSM_STATE, MXU_DIM), lambda c, b: (c, 0, 0)),
            pl.BlockSpec((ngl, MXU_DIM, 2 * SSM_STATE), lambda c, b: (c, 0, 0)),
            pl.BlockSpec((None, 2, STATE_LANES), lambda c, b: (c, 0, 0)),
        ] + cast_specs,
        out_specs=[pl.BlockSpec((seqs_per_step, seq_len, LANES), lambda c, b: (b, 0, c))]
        + cast_specs,
        out_shape=[jax.ShapeDtypeStruct((batch, seq_len, width), F32)]
        + [jax.ShapeDtypeStruct(w.shape, BF16) for w in riders],
        scratch_shapes=[pltpu.VMEM((seqs_per_step, m, 2 * STATE_LANES), F32),
                        pltpu.VMEM((seqs_per_step, m, 2 * STATE_LANES), F32)],
        compiler_params=pltpu.CompilerParams(
            dimension_semantics=("arbitrary", "arbitrary"),
            vmem_limit_bytes=VMEM_LIMIT_BYTES),
        name="ssm",
    )(u3, tz, pm, qm, pw, *riders)
    return outs[0].reshape(n, width), outs[1:]


def _ssm_operators(a_re, a_im, log_dt, b_re, b_im, c_re, c_im, d_skip):
    hp = lax.Precision.HIGHEST
    g = a_re.shape[0]
    nblk = g // GROUPS_PER_LANE_BLOCK
    dt = jnp.exp(log_dt)[:, None]
    mag = jnp.exp(dt * a_re)
    abar_re = mag * jnp.cos(dt * a_im)
    abar_im = mag * jnp.sin(dt * a_im)
    nr = abar_re - 1.0
    ni = abar_im
    den = a_re * a_re + a_im * a_im
    fr = (nr * a_re + ni * a_im) / den
    fi = (ni * a_re - nr * a_im) / den
    bb_re = fr[..., None] * b_re - fi[..., None] * b_im
    bb_im = fr[..., None] * b_im + fi[..., None] * b_re
    nn = jnp.arange(CHUNK + 1, dtype=F32)
    pmag = jnp.exp(nn * (dt * a_re)[..., None])
    pw_re = pmag * jnp.cos(nn * (dt * a_im)[..., None])
    pw_im = pmag * jnp.sin(nn * (dt * a_im)[..., None])

    col = jnp.arange(MXU_DIM)
    pick = (CHUNK - 1 - col[None, :] // SSM_GROUP == jnp.arange(CHUNK)[:, None]).astype(F32)
    e = jnp.einsum("rgpn,nc->rgpc", jnp.stack([pw_re, pw_im])[..., :CHUNK], pick,
                   precision=hp)
    spread = (col[None, :] % SSM_GROUP == jnp.arange(SSM_GROUP)[:, None]).astype(F32)
    bx = jnp.einsum("rgpi,ic->rgpc", jnp.stack([bb_re, bb_im]), spread, precision=hp)
    p_re = e[0] * bx[0] - e[1] * bx[1]
    p_im = e[0] * bx[1] + e[1] * bx[0]
    p_ri = jnp.concatenate([p_re, p_im], axis=1)
    pm = p_ri.astype(BF16)

    krev = jnp.einsum("gop,gpc->goc", jnp.concatenate([c_re, -c_im], axis=-1), p_ri,
                      precision=hp)
    lag0 = (CHUNK - 1) * SSM_GROUP + jnp.arange(SSM_GROUP)
    krev = krev + d_skip[..., None] * (col[None, :] == lag0[:, None]).astype(F32)
    padded = jnp.concatenate([krev, jnp.zeros_like(krev)], axis=-1)
    tz = jnp.stack([padded[..., (CHUNK - 1 - t) * SSM_GROUP:(CHUNK - 1 - t) * SSM_GROUP + MXU_DIM]
                    for t in range(CHUNK)], axis=1)
    tz = tz.reshape(g, MXU_DIM, MXU_DIM).astype(BF16)

    o_re = pw_re[..., 1:].transpose(0, 2, 1)[:, :, None, :]
    o_im = pw_im[..., 1:].transpose(0, 2, 1)[:, :, None, :]
    cr = c_re[:, None]
    ci = c_im[:, None]
    qm = jnp.concatenate([cr * o_re - ci * o_im, -(cr * o_im + ci * o_re)], axis=-1)
    qm = qm.reshape(g, MXU_DIM, 2 * SSM_STATE).astype(BF16)

    pw = jnp.stack([pw_re[..., CHUNK], pw_im[..., CHUNK]])
    pw = pw.reshape(2, nblk, STATE_LANES).transpose(1, 0, 2)
    return tz, pm, qm, pw


def _merge_body(col_tile, n_cast, xn_ref, ya_ref, zb_ref, wglu_ref, wm_ref, wssm_ref, wconv_ref,
                *rest):
    cast_src = rest[:n_cast]
    m_ref = rest[n_cast]
    cast_dst = rest[n_cast + 1:]
    _cast_riders(cast_src, cast_dst)
    d = m_ref.shape[1]
    ya = ya_ref[...]
    ya2 = (ya * jax.nn.sigmoid(_dot(ya.astype(BF16), wglu_ref[...]))).astype(BF16)
    xn = xn_ref[...]
    zb = zb_ref[...]
    for c in range(d // col_tile):
        cols = slice(c * col_tile, (c + 1) * col_tile)
        gate_a = jax.nn.sigmoid(_dot(xn, wm_ref[:, cols]))
        gate_b = jax.nn.sigmoid(_dot(xn, wm_ref[:, d + c * col_tile:d + (c + 1) * col_tile]))
        m_ref[:, cols] = (gate_a * _dot(ya2, wssm_ref[:, cols])
                          + gate_b * _dot(zb, wconv_ref[:, cols])).astype(BF16)


def _merge(xn, ya, zb, w_glu_bf, w_merge_bf, w_ssm_bf, w_conv_bf, riders, col_tile=512):
    n, d = xn.shape
    sw = ya.shape[1]
    cw = zb.shape[1]
    assert n % TOKEN_TILE == 0 and d % col_tile == 0 and w_merge_bf.shape == (d, 2 * d)
    resident = lambda shape: pl.BlockSpec(shape, lambda i: (0, 0), pipeline_mode=pl.Buffered(1))
    steps = n // TOKEN_TILE
    assert all(w.shape[0] % (steps * BF16_SUBLANES) == 0 for w in riders)
    cast_specs = [pl.BlockSpec((w.shape[0] // steps, w.shape[1]), lambda i: (i, 0)) for w in riders]
    outs = pl.pallas_call(
        functools.partial(_merge_body, col_tile, len(riders)),
        grid=(steps,),
        in_specs=[
            pl.BlockSpec((TOKEN_TILE, d), lambda i: (i, 0)),
            pl.BlockSpec((TOKEN_TILE, sw), lambda i: (i, 0)),
            pl.BlockSpec((TOKEN_TILE, cw), lambda i: (i, 0)),
            resident((sw, sw)),
            resident((d, 2 * d)),
            resident((sw, d)),
            resident((cw, d)),
        ] + cast_specs,
        out_specs=[pl.BlockSpec((TOKEN_TILE, d), lambda i: (i, 0))] + cast_specs,
        out_shape=[jax.ShapeDtypeStruct((n, d), BF16)]
        + [jax.ShapeDtypeStruct(w.shape, BF16) for w in riders],
        compiler_params=pltpu.CompilerParams(
            dimension_semantics=("arbitrary",), vmem_limit_bytes=VMEM_LIMIT_BYTES),
        name="merge",
    )(xn, ya, zb, w_glu_bf, w_merge_bf, w_ssm_bf, w_conv_bf, *riders)
    return outs[0], outs[1:]


def _outproj_body(col_tile, x_ref, m_ref, wo_ref, h_ref):
    m = m_ref[...]
    for c in range(h_ref.shape[1] // col_tile):
        cols = slice(c * col_tile, (c + 1) * col_tile)
        h_ref[:, cols] = x_ref[:, cols] + _dot(m, wo_ref[:, cols])


def _outproj(x2, merged, w_o_bf, col_tile=512):
    n, d = x2.shape
    tm = FFN_TOKEN_TILE
    assert n % tm == 0 and d % col_tile == 0
    return pl.pallas_call(
        functools.partial(_outproj_body, col_tile),
        grid=(n // tm,),
        in_specs=[
            pl.BlockSpec((tm, d), lambda i: (i, 0)),
            pl.BlockSpec((tm, d), lambda i: (i, 0)),
            pl.BlockSpec((d, d), lambda i: (0, 0), pipeline_mode=pl.Buffered(1)),
        ],
        out_specs=pl.BlockSpec((tm, d), lambda i: (i, 0)),
        out_shape=jax.ShapeDtypeStruct((n, d), F32),
        compiler_params=pltpu.CompilerParams(
            dimension_semantics=("arbitrary",), vmem_limit_bytes=VMEM_LIMIT_BYTES),
        name="outproj",
    )(x2, merged, w_o_bf)


def _ffn_body(tiles_per_seq, final_norm, h_hbm, g_ref, wa_ref, wg_ref, cw_ref, cb_ref,
              wd_ref, gfin_ref, o_ref, hbuf, hn_scr, prev_scr, sem):
    i = pl.program_id(0)
    f = pl.program_id(1)
    tm = o_ref.shape[0]

    def h_copy(tile):
        rows = pl.ds(pl.multiple_of(tile * tm, tm), tm)
        return pltpu.make_async_copy(h_hbm.at[rows], hbuf, sem)

    @pl.when(f == 0)
    def _():
        @pl.when(i == 0)
        def _():
            h_copy(0).start()

        h_copy(i).wait()
        h = hbuf[...]
        hn_scr[...] = _rmsnorm(h, g_ref[...]).astype(BF16)
        o_ref[...] = h

        @pl.when(i + 1 < pl.num_programs(0))
        def _():
            h_copy(i + 1).start()

    @pl.when(i % tiles_per_seq == 0)
    def _():
        prev_scr[f] = jnp.zeros(prev_scr.shape[1:], F32)

    hn = hn_scr[...]
    a = _dot(hn, wa_ref[...])
    conv = _causal_conv3(a, prev_scr[f], cw_ref, cb_ref)
    prev_scr[f] = a[a.shape[0] - SUBLANES:]
    act = (jax.nn.gelu(conv) * _dot(hn, wg_ref[...])).astype(BF16)
    o_ref[...] += _dot(act, wd_ref[...])

    if final_norm:
        @pl.when(f == pl.num_programs(1) - 1)
        def _():
            o_ref[...] = _rmsnorm(o_ref[...], gfin_ref[...])


def _ffn(h1, norm_g, w_up_bf, ffn_conv_w, ffn_conv_b, w_down_bf, norm_final, final_norm,
         seq_len, ff_tile=512):
    n, d = h1.shape
    d_ff = w_down_bf.shape[0]
    nff = d_ff // ff_tile
    tm = FFN_TOKEN_TILE
    assert n % tm == 0 and seq_len % tm == 0 and d_ff % ff_tile == 0
    assert w_up_bf.shape == (d, 2 * d_ff)
    tiles_per_seq = seq_len // tm
    const = lambda i, f: (0, 0)
    return pl.pallas_call(
        functools.partial(_ffn_body, tiles_per_seq, final_norm),
        grid=(n // tm, nff),
        in_specs=[
            pl.BlockSpec(memory_space=pl.ANY),
            pl.BlockSpec((1, d), const),
            pl.BlockSpec((d, ff_tile), lambda i, f: (0, f)),
            pl.BlockSpec((d, ff_tile), lambda i, f: (0, nff + f)),
            pl.BlockSpec((CONV_K, ff_tile), lambda i, f: (0, f)),
            pl.BlockSpec((1, ff_tile), lambda i, f: (0, f)),
            pl.BlockSpec((ff_tile, d), lambda i, f: (f, 0)),
            pl.BlockSpec((1, d), const),
        ],
        out_specs=pl.BlockSpec((tm, d), lambda i, f: (i, 0)),
        out_shape=jax.ShapeDtypeStruct((n, d), F32),
        scratch_shapes=[pltpu.VMEM((tm, d), F32),
                        pltpu.VMEM((tm, d), BF16),
                        pltpu.VMEM((nff, SUBLANES, ff_tile), F32),
                        pltpu.SemaphoreType.DMA(())],
        compiler_params=pltpu.CompilerParams(
            dimension_semantics=("arbitrary", "arbitrary"),
            vmem_limit_bytes=VMEM_LIMIT_BYTES),
        name="ffn",
    )(h1, norm_g, w_up_bf, w_up_bf, ffn_conv_w, ffn_conv_b, w_down_bf, norm_final)


def kernel(x, norm_tok, w_in, a_re, a_im, log_dt, b_re, b_im, c_re, c_im, d_skip, w_glu, w_ssm_out, conv_w, conv_b, w_conv_out, w_o, norm_ffn, w_up, ffn_conv_w, ffn_conv_b, w_down, norm_final):
    batch, seq_len, d = x.shape
    depth = w_in.shape[0]
    h = x.reshape(batch * seq_len, d)
    for l in range(depth):
        n_first = w_in.shape[2] - 2 * d
        w_first_bf = w_in[l, :, :n_first].astype(BF16)
        u, zb, xn, (w_merge_bf, w_glu_bf, w_ssm_bf, w_conv_bf, w_o_bf, w_down_bf) = _inproj(
            h, norm_tok[l][None], w_first_bf, conv_w[l], conv_b[l][None], seq_len,
            [(w_in[l], (n_first // (2 * d), 2 * d)), (w_glu[l], None), (w_ssm_out[l], None),
             (w_conv_out[l], None), (w_o[l], None), (w_down[l], None)])
        ops = _ssm_operators(a_re[l], a_im[l], log_dt[l], b_re[l], b_im[l],
                             c_re[l], c_im[l], d_skip[l])
        ya, _ = _ssm(u, *ops, batch, seq_len, [], seqs_per_step=batch)
        merged, (w_up_bf,) = _merge(xn, ya, zb, w_glu_bf, w_merge_bf, w_ssm_bf, w_conv_bf,
                                    [w_up[l]])
        h1 = _outproj(h, merged, w_o_bf)
        h = _ffn(h1, norm_ffn[l][None], w_up_bf, ffn_conv_w[l], ffn_conv_b[l][None],
                 w_down_bf, norm_final[None], l == depth - 1, seq_len)
    return h.reshape(batch, seq_len, d)
```

```python
import functools

import jax
import jax.numpy as jnp
from jax import lax
from jax.experimental import pallas as pl
from jax.experimental.pallas import tpu as pltpu

SSM_GROUP = 16
SSM_STATE = 64
CONV_K = 3
EPS = 1e-6

LANES = 128
SUBLANES = 8
BF16_SUBLANES = 16
MXU_DIM = 256
VMEM_LIMIT_BYTES = 56 * 1024 * 1024

TOKEN_TILE = 512
FFN_TOKEN_TILE = 1024
CHUNK = 16
GROUPS_PER_LANE_BLOCK = LANES // SSM_GROUP
STATE_LANES = GROUPS_PER_LANE_BLOCK * SSM_STATE

F32 = jnp.float32
BF16 = jnp.bfloat16


def _rmsnorm(x, gain):
    return x * lax.rsqrt(jnp.mean(x * x, axis=-1, keepdims=True) + EPS) * gain


def _dot(a, b):
    return jnp.dot(a, b, preferred_element_type=F32)


def _causal_conv3(a, prev, w_ref, b_ref):
    w0 = w_ref[0:1, :]
    w1 = w_ref[1:2, :]
    w2 = w_ref[2:3, :]
    bias = b_ref[...]
    full = w2 * a + w1 * pltpu.roll(a, 1, 0) + w0 * pltpu.roll(a, 2, 0) + bias
    top = a[0:SUBLANES]
    rows = lax.broadcasted_iota(jnp.int32, top.shape, 0)
    s1 = jnp.where(rows < 1, pltpu.roll(prev, 1, 0), pltpu.roll(top, 1, 0))
    s2 = jnp.where(rows < 2, pltpu.roll(prev, 2, 0), pltpu.roll(top, 2, 0))
    fix = w2 * top + w1 * s1 + w0 * s2 + bias
    return jnp.concatenate([fix, full[SUBLANES:]], axis=0)


def _cast_riders(src_refs, dst_refs):
    for src, dst in zip(src_refs, dst_refs):
        dst[...] = src[...].astype(BF16)


def _inproj_body(tiles_per_seq, n_cast, x_ref, g_ref, wu_ref, wv_ref, wgb_ref, wgc_ref,
                 cw_ref, cb_ref, *rest):
    cast_src = rest[:n_cast]
    u_ref, zb_ref, xn_ref = rest[n_cast:n_cast + 3]
    cast_dst = rest[n_cast + 3:2 * n_cast + 3]
    prev_ref = rest[2 * n_cast + 3]
    i = pl.program_id(0)

    @pl.when(i % tiles_per_seq == 0)
    def _():
        prev_ref[...] = jnp.zeros_like(prev_ref)

    xn = _rmsnorm(x_ref[...], g_ref[...]).astype(BF16)
    xn_ref[...] = xn
    u_ref[...] = _dot(xn, wu_ref[...])
    cin = _dot(xn, wgc_ref[...]) * _dot(xn, wv_ref[...])
    conv = _causal_conv3(cin, prev_ref[...], cw_ref, cb_ref)
    prev_ref[...] = cin[cin.shape[0] - SUBLANES:]
    zb_ref[...] = (_dot(xn, wgb_ref[...]) * conv).astype(BF16)
    _cast_riders(cast_src, cast_dst)


def _inproj(x2, norm_g, w_in_bf, conv_w, conv_b, seq_len, riders):
    n, d = x2.shape
    cw = conv_w.shape[1]
    assert n % TOKEN_TILE == 0 and seq_len % TOKEN_TILE == 0 and w_in_bf.shape == (d, 4 * cw)
    steps = n // TOKEN_TILE
    tiles_per_seq = seq_len // TOKEN_TILE
    const = lambda i: (0, 0)
    wspec = lambda col: pl.BlockSpec((d, cw), lambda i, col=col: (0, col),
                                     pipeline_mode=pl.Buffered(1))
    cast_in, cast_out, cast_shape = [], [], []
    for w, cols in riders:
        assert w.shape[0] % (steps * BF16_SUBLANES) == 0
        rows = w.shape[0] // steps
        width = w.shape[1] if cols is None else cols[1]
        col = 0 if cols is None else cols[0]
        cast_in.append(pl.BlockSpec((rows, width), lambda i, col=col: (i, col)))
        cast_out.append(pl.BlockSpec((rows, width), lambda i: (i, 0)))
        cast_shape.append(jax.ShapeDtypeStruct((w.shape[0], width), BF16))
    outs = pl.pallas_call(
        functools.partial(_inproj_body, tiles_per_seq, len(riders)),
        grid=(steps,),
        in_specs=[
            pl.BlockSpec((TOKEN_TILE, d), lambda i: (i, 0)),
            pl.BlockSpec((1, d), const),
            wspec(0), wspec(1), wspec(2), wspec(3),
            pl.BlockSpec((CONV_K, cw), const),
            pl.BlockSpec((1, cw), const),
        ] + cast_in,
        out_specs=[
            pl.BlockSpec((TOKEN_TILE, cw), lambda i: (i, 0)),
            pl.BlockSpec((TOKEN_TILE, cw), lambda i: (i, 0)),
            pl.BlockSpec((TOKEN_TILE, d), lambda i: (i, 0)),
        ] + cast_out,
        out_shape=[
            jax.ShapeDtypeStruct((n, cw), F32),
            jax.ShapeDtypeStruct((n, cw), BF16),
            jax.ShapeDtypeStruct((n, d), BF16),
        ] + cast_shape,
        scratch_shapes=[pltpu.VMEM((SUBLANES, cw), F32)],
        compiler_params=pltpu.CompilerParams(
            dimension_semantics=("arbitrary",), vmem_limit_bytes=VMEM_LIMIT_BYTES),
        name="inproj",
    )(x2, norm_g, w_in_bf, w_in_bf, w_in_bf, w_in_bf, conv_w, conv_b, *[w for w, _ in riders])
    return outs[0], outs[1], outs[2], outs[3:]


def _ssm_body(n_cast, u_ref, tz_ref, p_ref, q_ref, pw_ref, *rest):
    cast_src = rest[:n_cast]
    y_ref = rest[n_cast]
    cast_dst = rest[n_cast + 1:2 * n_cast + 1]
    s_scr, xin_scr = rest[2 * n_cast + 1:]
    _cast_riders(cast_src, cast_dst)
    nseq = u_ref.shape[0]
    m = u_ref.shape[1] // CHUNK
    sl = STATE_LANES
    ngl = GROUPS_PER_LANE_BLOCK

    yts = []
    for s in range(nseq):
        pts = [u_ref[s, pl.ds(t, m, stride=CHUNK), :].T.astype(BF16)
               for t in range(CHUNK)]
        ys, s_re, s_im = [], [], []
        for g in range(ngl):
            xg = jnp.concatenate([pt[g * SSM_GROUP:(g + 1) * SSM_GROUP] for pt in pts], axis=0)
            ys.append(_dot(tz_ref[g], xg))
            sg = _dot(p_ref[g], xg)
            s_re.append(sg[:SSM_STATE])
            s_im.append(sg[SSM_STATE:])
        yts.append(ys)
        s_scr[s] = jnp.concatenate(s_re + s_im, axis=0).T

    a16r = pw_ref[0:1, :]
    a16i = pw_ref[1:2, :]

    def step(k, carry):
        out = []
        for s in range(nseq):
            xr, xi = carry[2 * s], carry[2 * s + 1]
            xin_scr[s, pl.ds(k, 1), :sl] = xr
            xin_scr[s, pl.ds(k, 1), sl:] = xi
            row = s_scr[s, pl.ds(k, 1), :]
            out += [a16r * xr - a16i * xi + row[:, :sl], a16r * xi + a16i * xr + row[:, sl:]]
        return tuple(out)

    zero = jnp.zeros((1, sl), F32)
    lax.fori_loop(0, m, step, (zero,) * (2 * nseq), unroll=8)

    for s in range(nseq):
        xint = xin_scr[s].T
        ys = yts[s]
        for g in range(ngl):
            xg = jnp.concatenate([xint[g * SSM_STATE:(g + 1) * SSM_STATE],
                                  xint[sl + g * SSM_STATE:sl + (g + 1) * SSM_STATE]], axis=0)
            ys[g] = ys[g] + _dot(q_ref[g], xg.astype(BF16))
        for t in range(CHUNK):
            yt = jnp.concatenate([y[t * SSM_GROUP:(t + 1) * SSM_GROUP] for y in ys], axis=0)
            y_ref[s, pl.ds(t, m, stride=CHUNK), :] = jax.nn.gelu(yt.T)


def _ssm(u, tz, pm, qm, pw, batch, seq_len, riders, seqs_per_step=2):
    n, width = u.shape
    nblk = width // LANES
    m = seq_len // CHUNK
    ngl = GROUPS_PER_LANE_BLOCK
    u3 = u.reshape(batch, seq_len, width)
    assert seq_len % CHUNK == 0 and width % LANES == 0 and batch % seqs_per_step == 0
    nb = batch // seqs_per_step
    assert all(w.shape[0] % (nblk * nb * BF16_SUBLANES) == 0 for w in riders)
    cast_specs = [pl.BlockSpec((w.shape[0] // (nblk * nb), w.shape[1]),
                               lambda c, b: (c * nb + b, 0)) for w in riders]
    outs = pl.pallas_call(
        functools.partial(_ssm_body, len(riders)),
        grid=(nblk, nb),
        in_specs=[
            pl.BlockSpec((seqs_per_step, seq_len, LANES), lambda c, b: (b, 0, c)),
            pl.BlockSpec((ngl, MXU_DIM, MXU_DIM), lambda c, b: (c, 0, 0)),
            pl.BlockSpec((ngl, 2 * SSM_STATE, MXU_DIM), lambda c, b: (c, 0, 0)),
            pl.BlockSpec((ngl, MXU_DIM, 2 * SSM_STATE), lambda c, b: (c, 0, 0)),
            pl.BlockSpec((None, 2, STATE_LANES), lambda c, b: (c, 0, 0)),
        ] + cast_specs,
        out_specs=[pl.BlockSpec((seqs_per_step, seq_len, LANES), lambda c, b: (b, 0, c))]
        + cast_specs,
        out_shape=[jax.ShapeDtypeStruct((batch, seq_len, width), F32)]
        + [jax.ShapeDtypeStruct(w.shape, BF16) for w in riders],
        scratch_shapes=[pltpu.VMEM((seqs_per_step, m, 2 * STATE_LANES), F32),
                        pltpu.VMEM((seqs_per_step, m, 2 * STATE_LANES), F32)],
        compiler_params=pltpu.CompilerParams(
            dimension_semantics=("arbitrary", "arbitrary"),
            vmem_limit_bytes=VMEM_LIMIT_BYTES),
        name="ssm",
    )(u3, tz, pm, qm, pw, *riders)
    return outs[0].reshape(n, width), outs[1:]


def _ssm_operators(a_re, a_im, log_dt, b_re, b_im, c_re, c_im, d_skip):
    hp = lax.Precision.HIGHEST
    g = a_re.shape[0]
    nblk = g // GROUPS_PER_LANE_BLOCK
    dt = jnp.exp(log_dt)[:, None]
    mag = jnp.exp(dt * a_re)
    abar_re = mag * jnp.cos(dt * a_im)
    abar_im = mag * jnp.sin(dt * a_im)
    nr = abar_re - 1.0
    ni = abar_im
    den = a_re * a_re + a_im * a_im
    fr = (nr * a_re + ni * a_im) / den
    fi = (ni * a_re - nr * a_im) / den
    bb_re = fr[..., None] * b_re - fi[..., None] * b_im
    bb_im = fr[..., None] * b_im + fi[..., None] * b_re
    nn = jnp.arange(CHUNK + 1, dtype=F32)
    pmag = jnp.exp(nn * (dt * a_re)[..., None])
    pw_re = pmag * jnp.cos(nn * (dt * a_im)[..., None])
    pw_im = pmag * jnp.sin(nn * (dt * a_im)[..., None])

    col = jnp.arange(MXU_DIM)
    pick = (CHUNK - 1 - col[None, :] // SSM_GROUP == jnp.arange(CHUNK)[:, None]).astype(F32)
    e = jnp.einsum("rgpn,nc->rgpc", jnp.stack([pw_re, pw_im])[..., :CHUNK], pick,
                   precision=hp)
    spread = (col[None, :] % SSM_GROUP == jnp.arange(SSM_GROUP)[:, None]).astype(F32)
    bx = jnp.einsum("rgpi,ic->rgpc", jnp.stack([bb_re, bb_im]), spread, precision=hp)
    p_re = e[0] * bx[0] - e[1] * bx[1]
    p_im = e[0] * bx[1] + e[1] * bx[0]
    p_ri = jnp.concatenate([p_re, p_im], axis=1)
    pm = p_ri.astype(BF16)

    krev = jnp.einsum("gop,gpc->goc", jnp.concatenate([c_re, -c_im], axis=-1), p_ri,
                      precision=hp)
    lag0 = (CHUNK - 1) * SSM_GROUP + jnp.arange(SSM_GROUP)
    krev = krev + d_skip[..., None] * (col[None, :] == lag0[:, None]).astype(F32)
    padded = jnp.concatenate([krev, jnp.zeros_like(krev)], axis=-1)
    tz = jnp.stack([padded[..., (CHUNK - 1 - t) * SSM_GROUP:(CHUNK - 1 - t) * SSM_GROUP + MXU_DIM]
                    for t in range(CHUNK)], axis=1)
    tz = tz.reshape(g, MXU_DIM, MXU_DIM).astype(BF16)

    o_re = pw_re[..., 1:].transpose(0, 2, 1)[:, :, None, :]
    o_im = pw_im[..., 1:].transpose(0, 2, 1)[:, :, None, :]
    cr = c_re[:, None]
    ci = c_im[:, None]
    qm = jnp.concatenate([cr * o_re - ci * o_im, -(cr * o_im + ci * o_re)], axis=-1)
    qm = qm.reshape(g, MXU_DIM, 2 * SSM_STATE).astype(BF16)

    pw = jnp.stack([pw_re[..., CHUNK], pw_im[..., CHUNK]])
    pw = pw.reshape(2, nblk, STATE_LANES).transpose(1, 0, 2)
    return tz, pm, qm, pw


def _merge_body(col_tile, xn_ref, ya_ref, zb_ref, wglu_ref, wm_ref, wssm_ref, wconv_ref, m_ref):
    d = m_ref.shape[1]
    ya = ya_ref[...]
    ya2 = (ya * jax.nn.sigmoid(_dot(ya.astype(BF16), wglu_ref[...]))).astype(BF16)
    xn = xn_ref[...]
    zb = zb_ref[...]
    for c in range(d // col_tile):
        cols = slice(c * col_tile, (c + 1) * col_tile)
        gate_a = jax.nn.sigmoid(_dot(xn, wm_ref[:, cols]))
        gate_b = jax.nn.sigmoid(_dot(xn, wm_ref[:, d + c * col_tile:d + (c + 1) * col_tile]))
        m_ref[:, cols] = (gate_a * _dot(ya2, wssm_ref[:, cols])
                          + gate_b * _dot(zb, wconv_ref[:, cols])).astype(BF16)


def _merge(xn, ya, zb, w_glu_bf, w_merge_bf, w_ssm_bf, w_conv_bf, col_tile=512):
    n, d = xn.shape
    sw = ya.shape[1]
    cw = zb.shape[1]
    assert n % TOKEN_TILE == 0 and d % col_tile == 0 and w_merge_bf.shape == (d, 2 * d)
    resident = lambda shape: pl.BlockSpec(shape, lambda i: (0, 0), pipeline_mode=pl.Buffered(1))
    return pl.pallas_call(
        functools.partial(_merge_body, col_tile),
        grid=(n // TOKEN_TILE,),
        in_specs=[
            pl.BlockSpec((TOKEN_TILE, d), lambda i: (i, 0)),
            pl.BlockSpec((TOKEN_TILE, sw), lambda i: (i, 0)),
            pl.BlockSpec((TOKEN_TILE, cw), lambda i: (i, 0)),
            resident((sw, sw)),
            resident((d, 2 * d)),
            resident((sw, d)),
            resident((cw, d)),
        ],
        out_specs=pl.BlockSpec((TOKEN_TILE, d), lambda i: (i, 0)),
        out_shape=jax.ShapeDtypeStruct((n, d), BF16),
        compiler_params=pltpu.CompilerParams(
            dimension_semantics=("arbitrary",), vmem_limit_bytes=VMEM_LIMIT_BYTES),
        name="merge",
    )(xn, ya, zb, w_glu_bf, w_merge_bf, w_ssm_bf, w_conv_bf)


def _outproj_body(col_tile, x_ref, m_ref, wo_ref, h_ref):
    m = m_ref[...]
    for c in range(h_ref.shape[1] // col_tile):
        cols = slice(c * col_tile, (c + 1) * col_tile)
        h_ref[:, cols] = x_ref[:, cols] + _dot(m, wo_ref[:, cols])


def _outproj(x2, merged, w_o_bf, col_tile=512):
    n, d = x2.shape
    tm = FFN_TOKEN_TILE
    assert n % tm == 0 and d % col_tile == 0
    return pl.pallas_call(
        functools.partial(_outproj_body, col_tile),
        grid=(n // tm,),
        in_specs=[
            pl.BlockSpec((tm, d), lambda i: (i, 0)),
            pl.BlockSpec((tm, d), lambda i: (i, 0)),
            pl.BlockSpec((d, d), lambda i: (0, 0), pipeline_mode=pl.Buffered(1)),
        ],
        out_specs=pl.BlockSpec((tm, d), lambda i: (i, 0)),
        out_shape=jax.ShapeDtypeStruct((n, d), F32),
        compiler_params=pltpu.CompilerParams(
            dimension_semantics=("arbitrary",), vmem_limit_bytes=VMEM_LIMIT_BYTES),
        name="outproj",
    )(x2, merged, w_o_bf)


def _ffn_body(tiles_per_seq, final_norm, ff_tile, h_hbm, g_ref, wup_hbm, cw_hbm, cb_hbm,
              wd_hbm, gfin_ref, o_ref, hbuf, hn_scr, prev_scr, chunk_idx, sem):
    i = pl.program_id(0)
    tm, d = o_ref.shape
    nff = wd_hbm.shape[0] // ff_tile

    def h_copy(tile):
        rows = pl.ds(pl.multiple_of(tile * tm, tm), tm)
        return pltpu.make_async_copy(h_hbm.at[rows], hbuf, sem)

    @pl.when(i == 0)
    def _():
        h_copy(0).start()

    h_copy(i).wait()
    h = hbuf[...]
    hn_scr[...] = _rmsnorm(h, g_ref[...]).astype(BF16)
    o_ref[...] = h

    @pl.when(i + 1 < pl.num_programs(0))
    def _():
        h_copy(i + 1).start()

    @pl.when(i % tiles_per_seq == 0)
    def _():
        prev_scr[...] = jnp.zeros_like(prev_scr)

    chunk_idx[0] = 0

    def chunk(wa_ref, wg_ref, cw_ref, cb_ref, wd_ref):
        f = chunk_idx[0]
        chunk_idx[0] = f + 1
        hn = hn_scr[...]
        a = _dot(hn, wa_ref[...])
        conv = _causal_conv3(a, prev_scr[f], cw_ref, cb_ref)
        prev_scr[f] = a[a.shape[0] - SUBLANES:]
        act = (jax.nn.gelu(conv) * _dot(hn, wg_ref[...])).astype(BF16)
        o_ref[...] += _dot(act, wd_ref[...])

    pltpu.emit_pipeline(
        chunk,
        grid=(nff,),
        in_specs=[
            pl.BlockSpec((d, ff_tile), lambda f: (0, f)),
            pl.BlockSpec((d, ff_tile), lambda f: (0, nff + f)),
            pl.BlockSpec((CONV_K, ff_tile), lambda f: (0, f)),
            pl.BlockSpec((1, ff_tile), lambda f: (0, f)),
            pl.BlockSpec((ff_tile, d), lambda f: (f, 0)),
        ],
    )(wup_hbm, wup_hbm, cw_hbm, cb_hbm, wd_hbm)

    if final_norm:
        o_ref[...] = _rmsnorm(o_ref[...], gfin_ref[...])


def _ffn(h1, norm_g, w_up_bf, ffn_conv_w, ffn_conv_b, w_down_bf, norm_final, final_norm,
         seq_len, ff_tile=512):
    n, d = h1.shape
    d_ff = w_down_bf.shape[0]
    nff = d_ff // ff_tile
    tm = FFN_TOKEN_TILE
    assert n % tm == 0 and seq_len % tm == 0 and d_ff % ff_tile == 0
    assert w_up_bf.shape == (d, 2 * d_ff)
    tiles_per_seq = seq_len // tm
    const = lambda i: (0, 0)
    hbm = pl.BlockSpec(memory_space=pl.ANY)
    return pl.pallas_call(
        functools.partial(_ffn_body, tiles_per_seq, final_norm, ff_tile),
        grid=(n // tm,),
        in_specs=[hbm, pl.BlockSpec((1, d), const), hbm, hbm, hbm, hbm,
                  pl.BlockSpec((1, d), const)],
        out_specs=pl.BlockSpec((tm, d), lambda i: (i, 0)),
        out_shape=jax.ShapeDtypeStruct((n, d), F32),
        scratch_shapes=[pltpu.VMEM((tm, d), F32),
                        pltpu.VMEM((tm, d), BF16),
                        pltpu.VMEM((nff, SUBLANES, ff_tile), F32),
                        pltpu.SMEM((1,), jnp.int32),
                        pltpu.SemaphoreType.DMA(())],
        compiler_params=pltpu.CompilerParams(
            dimension_semantics=("arbitrary",), vmem_limit_bytes=VMEM_LIMIT_BYTES),
        name="ffn",
    )(h1, norm_g, w_up_bf, ffn_conv_w, ffn_conv_b, w_down_bf, norm_final)


def kernel(x, norm_tok, w_in, a_re, a_im, log_dt, b_re, b_im, c_re, c_im, d_skip, w_glu, w_ssm_out, conv_w, conv_b, w_conv_out, w_o, norm_ffn, w_up, ffn_conv_w, ffn_conv_b, w_down, norm_final):
    batch, seq_len, d = x.shape
    depth = w_in.shape[0]
    h = x.reshape(batch * seq_len, d)
    for l in range(depth):
        n_first = w_in.shape[2] - 2 * d
        w_first_bf = w_in[l, :, :n_first].astype(BF16)
        u, zb, xn, (w_merge_bf, w_glu_bf, w_ssm_bf, w_conv_bf, w_o_bf, w_down_bf) = _inproj(
            h, norm_tok[l][None], w_first_bf, conv_w[l], conv_b[l][None], seq_len,
            [(w_in[l], (n_first // (2 * d), 2 * d)), (w_glu[l], None), (w_ssm_out[l], None),
             (w_conv_out[l], None), (w_o[l], None), (w_down[l], None)])
        ops = _ssm_operators(a_re[l], a_im[l], log_dt[l], b_re[l], b_im[l],
                             c_re[l], c_im[l], d_skip[l])
        ya, (w_up_bf,) = _ssm(u, *ops, batch, seq_len, [w_up[l]])
        merged = _merge(xn, ya, zb, w_glu_bf, w_merge_bf, w_ssm_bf, w_conv_bf)
        h1 = _outproj(h, merged, w_o_bf)
        h = _ffn(h1, norm_ffn[l][None], w_up_bf, ffn_conv_w[l], ffn_conv_b[l][None],
                 w_down_bf, norm_final[None], l == depth - 1, seq_len)
    return h.reshape(batch, seq_len, d)
```

```python
import functools

import jax
import jax.numpy as jnp
from jax import lax
from jax.experimental import pallas as pl
from jax.experimental.pallas import tpu as pltpu

SSM_GROUP = 16
SSM_STATE = 64
CONV_K = 3
EPS = 1e-6

LANES = 128
SUBLANES = 8
BF16_SUBLANES = 16
MXU_DIM = 256
VMEM_LIMIT_BYTES = 56 * 1024 * 1024

TOKEN_TILE = 512
FFN_TOKEN_TILE = 1024
MIX_TOKEN_TILE = 256
CHUNK = 16
GROUPS_PER_LANE_BLOCK = LANES // SSM_GROUP
STATE_LANES = GROUPS_PER_LANE_BLOCK * SSM_STATE

F32 = jnp.float32
BF16 = jnp.bfloat16


def _rmsnorm(x, gain):
    return x * lax.rsqrt(jnp.mean(x * x, axis=-1, keepdims=True) + EPS) * gain


def _dot(a, b):
    return jnp.dot(a, b, preferred_element_type=F32)


def _causal_conv3(a, prev, w_ref, b_ref):
    w0 = w_ref[0:1, :]
    w1 = w_ref[1:2, :]
    w2 = w_ref[2:3, :]
    bias = b_ref[...]
    full = w2 * a + w1 * pltpu.roll(a, 1, 0) + w0 * pltpu.roll(a, 2, 0) + bias
    top = a[0:SUBLANES]
    rows = lax.broadcasted_iota(jnp.int32, top.shape, 0)
    s1 = jnp.where(rows < 1, pltpu.roll(prev, 1, 0), pltpu.roll(top, 1, 0))
    s2 = jnp.where(rows < 2, pltpu.roll(prev, 2, 0), pltpu.roll(top, 2, 0))
    fix = w2 * top + w1 * s1 + w0 * s2 + bias
    return jnp.concatenate([fix, full[SUBLANES:]], axis=0)


def _cast_riders(src_refs, dst_refs):
    for src, dst in zip(src_refs, dst_refs):
        dst[...] = src[...].astype(BF16)


def _inproj_body(tiles_per_seq, n_cast, x_ref, g_ref, wu_ref, wv_ref, wgb_ref, wgc_ref,
                 cw_ref, cb_ref, *rest):
    cast_src = rest[:n_cast]
    u_ref, zb_ref, xn_ref = rest[n_cast:n_cast + 3]
    cast_dst = rest[n_cast + 3:2 * n_cast + 3]
    prev_ref = rest[2 * n_cast + 3]
    i = pl.program_id(0)

    @pl.when(i % tiles_per_seq == 0)
    def _():
        prev_ref[...] = jnp.zeros_like(prev_ref)

    xn = _rmsnorm(x_ref[...], g_ref[...]).astype(BF16)
    xn_ref[...] = xn
    u_ref[...] = _dot(xn, wu_ref[...])
    cin = _dot(xn, wgc_ref[...]) * _dot(xn, wv_ref[...])
    conv = _causal_conv3(cin, prev_ref[...], cw_ref, cb_ref)
    prev_ref[...] = cin[cin.shape[0] - SUBLANES:]
    zb_ref[...] = (_dot(xn, wgb_ref[...]) * conv).astype(BF16)
    _cast_riders(cast_src, cast_dst)


def _inproj(x2, norm_g, w_in_bf, conv_w, conv_b, seq_len, riders):
    n, d = x2.shape
    cw = conv_w.shape[1]
    assert n % TOKEN_TILE == 0 and seq_len % TOKEN_TILE == 0 and w_in_bf.shape == (d, 4 * cw)
    steps = n // TOKEN_TILE
    tiles_per_seq = seq_len // TOKEN_TILE
    const = lambda i: (0, 0)
    wspec = lambda col: pl.BlockSpec((d, cw), lambda i, col=col: (0, col),
                                     pipeline_mode=pl.Buffered(1))
    cast_in, cast_out, cast_shape = [], [], []
    for w, cols in riders:
        assert w.shape[0] % (steps * BF16_SUBLANES) == 0
        rows = w.shape[0] // steps
        width = w.shape[1] if cols is None else cols[1]
        col = 0 if cols is None else cols[0]
        cast_in.append(pl.BlockSpec((rows, width), lambda i, col=col: (i, col)))
        cast_out.append(pl.BlockSpec((rows, width), lambda i: (i, 0)))
        cast_shape.append(jax.ShapeDtypeStruct((w.shape[0], width), BF16))
    outs = pl.pallas_call(
        functools.partial(_inproj_body, tiles_per_seq, len(riders)),
        grid=(steps,),
        in_specs=[
            pl.BlockSpec((TOKEN_TILE, d), lambda i: (i, 0)),
            pl.BlockSpec((1, d), const),
            wspec(0), wspec(1), wspec(2), wspec(3),
            pl.BlockSpec((CONV_K, cw), const),
            pl.BlockSpec((1, cw), const),
        ] + cast_in,
        out_specs=[
            pl.BlockSpec((TOKEN_TILE, cw), lambda i: (i, 0)),
            pl.BlockSpec((TOKEN_TILE, cw), lambda i: (i, 0)),
            pl.BlockSpec((TOKEN_TILE, d), lambda i: (i, 0)),
        ] + cast_out,
        out_shape=[
            jax.ShapeDtypeStruct((n, cw), F32),
            jax.ShapeDtypeStruct((n, cw), BF16),
            jax.ShapeDtypeStruct((n, d), BF16),
        ] + cast_shape,
        scratch_shapes=[pltpu.VMEM((SUBLANES, cw), F32)],
        compiler_params=pltpu.CompilerParams(
            dimension_semantics=("arbitrary",), vmem_limit_bytes=VMEM_LIMIT_BYTES),
        name="inproj",
    )(x2, norm_g, w_in_bf, w_in_bf, w_in_bf, w_in_bf, conv_w, conv_b, *[w for w, _ in riders])
    return outs[0], outs[1], outs[2], outs[3:]


def _ssm_body(n_cast, u_ref, tz_ref, p_ref, q_ref, pw_ref, *rest):
    cast_src = rest[:n_cast]
    y_ref = rest[n_cast]
    cast_dst = rest[n_cast + 1:2 * n_cast + 1]
    s_scr, xin_scr = rest[2 * n_cast + 1:]
    _cast_riders(cast_src, cast_dst)
    nseq = u_ref.shape[0]
    m = u_ref.shape[1] // CHUNK
    sl = STATE_LANES
    ngl = GROUPS_PER_LANE_BLOCK

    yts = []
    for s in range(nseq):
        pts = [u_ref[s, pl.ds(t, m, stride=CHUNK), :].T.astype(BF16)
               for t in range(CHUNK)]
        ys, s_re, s_im = [], [], []
        for g in range(ngl):
            xg = jnp.concatenate([pt[g * SSM_GROUP:(g + 1) * SSM_GROUP] for pt in pts], axis=0)
            ys.append(_dot(tz_ref[g], xg))
            sg = _dot(p_ref[g], xg)
            s_re.append(sg[:SSM_STATE])
            s_im.append(sg[SSM_STATE:])
        yts.append(ys)
        s_scr[s] = jnp.concatenate(s_re + s_im, axis=0).T

    a16r = pw_ref[0:1, :]
    a16i = pw_ref[1:2, :]

    def step(k, carry):
        out = []
        for s in range(nseq):
            xr, xi = carry[2 * s], carry[2 * s + 1]
            xin_scr[s, pl.ds(k, 1), :sl] = xr
            xin_scr[s, pl.ds(k, 1), sl:] = xi
            row = s_scr[s, pl.ds(k, 1), :]
            out += [a16r * xr - a16i * xi + row[:, :sl], a16r * xi + a16i * xr + row[:, sl:]]
        return tuple(out)

    zero = jnp.zeros((1, sl), F32)
    lax.fori_loop(0, m, step, (zero,) * (2 * nseq), unroll=8)

    for s in range(nseq):
        xint = xin_scr[s].T
        ys = yts[s]
        for g in range(ngl):
            xg = jnp.concatenate([xint[g * SSM_STATE:(g + 1) * SSM_STATE],
                                  xint[sl + g * SSM_STATE:sl + (g + 1) * SSM_STATE]], axis=0)
            ys[g] = ys[g] + _dot(q_ref[g], xg.astype(BF16))
        for t in range(CHUNK):
            yt = jnp.concatenate([y[t * SSM_GROUP:(t + 1) * SSM_GROUP] for y in ys], axis=0)
            y_ref[s, pl.ds(t, m, stride=CHUNK), :] = jax.nn.gelu(yt.T)


def _ssm(u, tz, pm, qm, pw, batch, seq_len, riders, seqs_per_step=2):
    n, width = u.shape
    nblk = width // LANES
    m = seq_len // CHUNK
    ngl = GROUPS_PER_LANE_BLOCK
    u3 = u.reshape(batch, seq_len, width)
    assert seq_len % CHUNK == 0 and width % LANES == 0 and batch % seqs_per_step == 0
    nb = batch // seqs_per_step
    assert all(w.shape[0] % (nblk * nb * BF16_SUBLANES) == 0 for w in riders)
    cast_specs = [pl.BlockSpec((w.shape[0] // (nblk * nb), w.shape[1]),
                               lambda c, b: (c * nb + b, 0)) for w in riders]
    outs = pl.pallas_call(
        functools.partial(_ssm_body, len(riders)),
        grid=(nblk, nb),
        in_specs=[
            pl.BlockSpec((seqs_per_step, seq_len, LANES), lambda c, b: (b, 0, c)),
            pl.BlockSpec((ngl, MXU_DIM, MXU_DIM), lambda c, b: (c, 0, 0)),
            pl.BlockSpec((ngl, 2 * SSM_STATE, MXU_DIM), lambda c, b: (c, 0, 0)),
            pl.BlockSpec((ngl, MXU_DIM, 2 * SSM_STATE), lambda c, b: (c, 0, 0)),
            pl.BlockSpec((None, 2, STATE_LANES), lambda c, b: (c, 0, 0)),
        ] + cast_specs,
        out_specs=[pl.BlockSpec((seqs_per_step, seq_len, LANES), lambda c, b: (b, 0, c))]
        + cast_specs,
        out_shape=[jax.ShapeDtypeStruct((batch, seq_len, width), F32)]
        + [jax.ShapeDtypeStruct(w.shape, BF16) for w in riders],
        scratch_shapes=[pltpu.VMEM((seqs_per_step, m, 2 * STATE_LANES), F32),
                        pltpu.VMEM((seqs_per_step, m, 2 * STATE_LANES), F32)],
        compiler_params=pltpu.CompilerParams(
            dimension_semantics=("arbitrary", "arbitrary"),
            vmem_limit_bytes=VMEM_LIMIT_BYTES),
        name="ssm",
    )(u3, tz, pm, qm, pw, *riders)
    return outs[0].reshape(n, width), outs[1:]


def _ssm_operators(a_re, a_im, log_dt, b_re, b_im, c_re, c_im, d_skip):
    hp = lax.Precision.HIGHEST
    g = a_re.shape[0]
    nblk = g // GROUPS_PER_LANE_BLOCK
    dt = jnp.exp(log_dt)[:, None]
    mag = jnp.exp(dt * a_re)
    abar_re = mag * jnp.cos(dt * a_im)
    abar_im = mag * jnp.sin(dt * a_im)
    nr = abar_re - 1.0
    ni = abar_im
    den = a_re * a_re + a_im * a_im
    fr = (nr * a_re + ni * a_im) / den
    fi = (ni * a_re - nr * a_im) / den
    bb_re = fr[..., None] * b_re - fi[..., None] * b_im
    bb_im = fr[..., None] * b_im + fi[..., None] * b_re
    nn = jnp.arange(CHUNK + 1, dtype=F32)
    pmag = jnp.exp(nn * (dt * a_re)[..., None])
    pw_re = pmag * jnp.cos(nn * (dt * a_im)[..., None])
    pw_im = pmag * jnp.sin(nn * (dt * a_im)[..., None])

    col = jnp.arange(MXU_DIM)
    pick = (CHUNK - 1 - col[None, :] // SSM_GROUP == jnp.arange(CHUNK)[:, None]).astype(F32)
    e = jnp.einsum("rgpn,nc->rgpc", jnp.stack([pw_re, pw_im])[..., :CHUNK], pick,
                   precision=hp)
    spread = (col[None, :] % SSM_GROUP == jnp.arange(SSM_GROUP)[:, None]).astype(F32)
    bx = jnp.einsum("rgpi,ic->rgpc", jnp.stack([bb_re, bb_im]), spread, precision=hp)
    p_re = e[0] * bx[0] - e[1] * bx[1]
    p_im = e[0] * bx[1] + e[1] * bx[0]
    p_ri = jnp.concatenate([p_re, p_im], axis=1)
    pm = p_ri.astype(BF16)

    krev = jnp.einsum("gop,gpc->goc", jnp.concatenate([c_re, -c_im], axis=-1), p_ri,
                      precision=hp)
    lag0 = (CHUNK - 1) * SSM_GROUP + jnp.arange(SSM_GROUP)
    krev = krev + d_skip[..., None] * (col[None, :] == lag0[:, None]).astype(F32)
    padded = jnp.concatenate([krev, jnp.zeros_like(krev)], axis=-1)
    tz = jnp.stack([padded[..., (CHUNK - 1 - t) * SSM_GROUP:(CHUNK - 1 - t) * SSM_GROUP + MXU_DIM]
                    for t in range(CHUNK)], axis=1)
    tz = tz.reshape(g, MXU_DIM, MXU_DIM).astype(BF16)

    o_re = pw_re[..., 1:].transpose(0, 2, 1)[:, :, None, :]
    o_im = pw_im[..., 1:].transpose(0, 2, 1)[:, :, None, :]
    cr = c_re[:, None]
    ci = c_im[:, None]
    qm = jnp.concatenate([cr * o_re - ci * o_im, -(cr * o_im + ci * o_re)], axis=-1)
    qm = qm.reshape(g, MXU_DIM, 2 * SSM_STATE).astype(BF16)

    pw = jnp.stack([pw_re[..., CHUNK], pw_im[..., CHUNK]])
    pw = pw.reshape(2, nblk, STATE_LANES).transpose(1, 0, 2)
    return tz, pm, qm, pw


def _merge_body(col_tile, xn_ref, ya_ref, zb_ref, wglu_ref, wm_ref, wssm_ref, wconv_ref, m_ref):
    d = m_ref.shape[1]
    ya = ya_ref[...]
    ya2 = (ya * jax.nn.sigmoid(_dot(ya.astype(BF16), wglu_ref[...]))).astype(BF16)
    xn = xn_ref[...]
    zb = zb_ref[...]
    for c in range(d // col_tile):
        cols = slice(c * col_tile, (c + 1) * col_tile)
        gate_a = jax.nn.sigmoid(_dot(xn, wm_ref[:, cols]))
        gate_b = jax.nn.sigmoid(_dot(xn, wm_ref[:, d + c * col_tile:d + (c + 1) * col_tile]))
        m_ref[:, cols] = (gate_a * _dot(ya2, wssm_ref[:, cols])
                          + gate_b * _dot(zb, wconv_ref[:, cols])).astype(BF16)


def _merge(xn, ya, zb, w_glu_bf, w_merge_bf, w_ssm_bf, w_conv_bf, col_tile=512):
    n, d = xn.shape
    sw = ya.shape[1]
    cw = zb.shape[1]
    assert n % TOKEN_TILE == 0 and d % col_tile == 0 and w_merge_bf.shape == (d, 2 * d)
    resident = lambda shape: pl.BlockSpec(shape, lambda i: (0, 0), pipeline_mode=pl.Buffered(1))
    return pl.pallas_call(
        functools.partial(_merge_body, col_tile),
        grid=(n // TOKEN_TILE,),
        in_specs=[
            pl.BlockSpec((TOKEN_TILE, d), lambda i: (i, 0)),
            pl.BlockSpec((TOKEN_TILE, sw), lambda i: (i, 0)),
            pl.BlockSpec((TOKEN_TILE, cw), lambda i: (i, 0)),
            resident((sw, sw)),
            resident((d, 2 * d)),
            resident((sw, d)),
            resident((cw, d)),
        ],
        out_specs=pl.BlockSpec((TOKEN_TILE, d), lambda i: (i, 0)),
        out_shape=jax.ShapeDtypeStruct((n, d), BF16),
        compiler_params=pltpu.CompilerParams(
            dimension_semantics=("arbitrary",), vmem_limit_bytes=VMEM_LIMIT_BYTES),
        name="merge",
    )(xn, ya, zb, w_glu_bf, w_merge_bf, w_ssm_bf, w_conv_bf)


def _outproj_body(col_tile, x_ref, m_ref, wo_ref, h_ref):
    m = m_ref[...]
    for c in range(h_ref.shape[1] // col_tile):
        cols = slice(c * col_tile, (c + 1) * col_tile)
        h_ref[:, cols] = x_ref[:, cols] + _dot(m, wo_ref[:, cols])


def _outproj(x2, merged, w_o_bf, col_tile=512):
    n, d = x2.shape
    tm = FFN_TOKEN_TILE
    assert n % tm == 0 and d % col_tile == 0
    return pl.pallas_call(
        functools.partial(_outproj_body, col_tile),
        grid=(n // tm,),
        in_specs=[
            pl.BlockSpec((tm, d), lambda i: (i, 0)),
            pl.BlockSpec((tm, d), lambda i: (i, 0)),
            pl.BlockSpec((d, d), lambda i: (0, 0), pipeline_mode=pl.Buffered(1)),
        ],
        out_specs=pl.BlockSpec((tm, d), lambda i: (i, 0)),
        out_shape=jax.ShapeDtypeStruct((n, d), F32),
        compiler_params=pltpu.CompilerParams(
            dimension_semantics=("arbitrary",), vmem_limit_bytes=VMEM_LIMIT_BYTES),
        name="outproj",
    )(x2, merged, w_o_bf)


def _mixout_body(col_tile, x_ref, xn_ref, ya_ref, zb_ref, wglu_ref, wm_ref, wssm_ref, wconv_ref,
                 wo_ref, h_ref, m_scr):
    d = h_ref.shape[1]
    ya = ya_ref[...]
    ya2 = (ya * jax.nn.sigmoid(_dot(ya.astype(BF16), wglu_ref[...]))).astype(BF16)
    xn = xn_ref[...]
    zb = zb_ref[...]
    for c in range(d // col_tile):
        cols = slice(c * col_tile, (c + 1) * col_tile)
        gate_a = jax.nn.sigmoid(_dot(xn, wm_ref[:, cols]))
        gate_b = jax.nn.sigmoid(_dot(xn, wm_ref[:, d + c * col_tile:d + (c + 1) * col_tile]))
        m_scr[:, cols] = (gate_a * _dot(ya2, wssm_ref[:, cols])
                          + gate_b * _dot(zb, wconv_ref[:, cols])).astype(BF16)
    m = m_scr[...]
    for c in range(d // col_tile):
        cols = slice(c * col_tile, (c + 1) * col_tile)
        h_ref[:, cols] = x_ref[:, cols] + _dot(m, wo_ref[:, cols])


def _mixout(x2, xn, ya, zb, w_glu_bf, w_merge_bf, w_ssm_bf, w_conv_bf, w_o_bf, col_tile=512):
    n, d = xn.shape
    sw = ya.shape[1]
    cw = zb.shape[1]
    tm = MIX_TOKEN_TILE
    assert n % tm == 0 and d % col_tile == 0 and w_merge_bf.shape == (d, 2 * d)
    resident = lambda shape: pl.BlockSpec(shape, lambda i: (0, 0), pipeline_mode=pl.Buffered(1))
    return pl.pallas_call(
        functools.partial(_mixout_body, col_tile),
        grid=(n // tm,),
        in_specs=[
            pl.BlockSpec((tm, d), lambda i: (i, 0)),
            pl.BlockSpec((tm, d), lambda i: (i, 0)),
            pl.BlockSpec((tm, sw), lambda i: (i, 0)),
            pl.BlockSpec((tm, cw), lambda i: (i, 0)),
            resident((sw, sw)),
            resident((d, 2 * d)),
            resident((sw, d)),
            resident((cw, d)),
            resident((d, d)),
        ],
        out_specs=pl.BlockSpec((tm, d), lambda i: (i, 0)),
        out_shape=jax.ShapeDtypeStruct((n, d), F32),
        scratch_shapes=[pltpu.VMEM((tm, d), BF16)],
        compiler_params=pltpu.CompilerParams(
            dimension_semantics=("arbitrary",), vmem_limit_bytes=VMEM_LIMIT_BYTES),
        name="mixout",
    )(x2, xn, ya, zb, w_glu_bf, w_merge_bf, w_ssm_bf, w_conv_bf, w_o_bf)


def _ffn_body(tiles_per_seq, final_norm, h_hbm, g_ref, wa_ref, wg_ref, cw_ref, cb_ref,
              wd_ref, gfin_ref, o_ref, hbuf, hn_scr, prev_scr, sem):
    i = pl.program_id(0)
    f = pl.program_id(1)
    tm = o_ref.shape[0]

    def h_copy(tile):
        rows = pl.ds(pl.multiple_of(tile * tm, tm), tm)
        return pltpu.make_async_copy(h_hbm.at[rows], hbuf, sem)

    @pl.when(f == 0)
    def _():
        @pl.when(i == 0)
        def _():
            h_copy(0).start()

        h_copy(i).wait()
        h = hbuf[...]
        hn_scr[...] = _rmsnorm(h, g_ref[...]).astype(BF16)
        o_ref[...] = h

        @pl.when(i + 1 < pl.num_programs(0))
        def _():
            h_copy(i + 1).start()

    @pl.when(i % tiles_per_seq == 0)
    def _():
        prev_scr[f] = jnp.zeros(prev_scr.shape[1:], F32)

    hn = hn_scr[...]
    a = _dot(hn, wa_ref[...])
    conv = _causal_conv3(a, prev_scr[f], cw_ref, cb_ref)
    prev_scr[f] = a[a.shape[0] - SUBLANES:]
    act = (jax.nn.gelu(conv) * _dot(hn, wg_ref[...])).astype(BF16)
    o_ref[...] += _dot(act, wd_ref[...])

    if final_norm:
        @pl.when(f == pl.num_programs(1) - 1)
        def _():
            o_ref[...] = _rmsnorm(o_ref[...], gfin_ref[...])


def _ffn(h1, norm_g, w_up_bf, ffn_conv_w, ffn_conv_b, w_down_bf, norm_final, final_norm,
         seq_len, ff_tile=512):
    n, d = h1.shape
    d_ff = w_down_bf.shape[0]
    nff = d_ff // ff_tile
    tm = FFN_TOKEN_TILE
    assert n % tm == 0 and seq_len % tm == 0 and d_ff % ff_tile == 0
    assert w_up_bf.shape == (d, 2 * d_ff)
    tiles_per_seq = seq_len // tm
    const = lambda i, f: (0, 0)
    return pl.pallas_call(
        functools.partial(_ffn_body, tiles_per_seq, final_norm),
        grid=(n // tm, nff),
        in_specs=[
            pl.BlockSpec(memory_space=pl.ANY),
            pl.BlockSpec((1, d), const),
            pl.BlockSpec((d, ff_tile), lambda i, f: (0, f)),
            pl.BlockSpec((d, ff_tile), lambda i, f: (0, nff + f)),
            pl.BlockSpec((CONV_K, ff_tile), lambda i, f: (0, f)),
            pl.BlockSpec((1, ff_tile), lambda i, f: (0, f)),
            pl.BlockSpec((ff_tile, d), lambda i, f: (f, 0)),
            pl.BlockSpec((1, d), const),
        ],
        out_specs=pl.BlockSpec((tm, d), lambda i, f: (i, 0)),
        out_shape=jax.ShapeDtypeStruct((n, d), F32),
        scratch_shapes=[pltpu.VMEM((tm, d), F32),
                        pltpu.VMEM((tm, d), BF16),
                        pltpu.VMEM((nff, SUBLANES, ff_tile), F32),
                        pltpu.SemaphoreType.DMA(())],
        compiler_params=pltpu.CompilerParams(
            dimension_semantics=("arbitrary", "arbitrary"),
            vmem_limit_bytes=VMEM_LIMIT_BYTES),
        name="ffn",
    )(h1, norm_g, w_up_bf, w_up_bf, ffn_conv_w, ffn_conv_b, w_down_bf, norm_final)


def kernel(x, norm_tok, w_in, a_re, a_im, log_dt, b_re, b_im, c_re, c_im, d_skip, w_glu, w_ssm_out, conv_w, conv_b, w_conv_out, w_o, norm_ffn, w_up, ffn_conv_w, ffn_conv_b, w_down, norm_final):
    batch, seq_len, d = x.shape
    depth = w_in.shape[0]
    h = x.reshape(batch * seq_len, d)
    for l in range(depth):
        n_first = w_in.shape[2] - 2 * d
        w_first_bf = w_in[l, :, :n_first].astype(BF16)
        u, zb, xn, (w_merge_bf, w_glu_bf, w_ssm_bf, w_conv_bf, w_o_bf, w_down_bf) = _inproj(
            h, norm_tok[l][None], w_first_bf, conv_w[l], conv_b[l][None], seq_len,
            [(w_in[l], (n_first // (2 * d), 2 * d)), (w_glu[l], None), (w_ssm_out[l], None),
             (w_conv_out[l], None), (w_o[l], None), (w_down[l], None)])
        ops = _ssm_operators(a_re[l], a_im[l], log_dt[l], b_re[l], b_im[l],
                             c_re[l], c_im[l], d_skip[l])
        ya, (w_up_bf,) = _ssm(u, *ops, batch, seq_len, [w_up[l]])
        h1 = _mixout(h, xn, ya, zb, w_glu_bf, w_merge_bf, w_ssm_bf, w_conv_bf, w_o_bf)
        h = _ffn(h1, norm_ffn[l][None], w_up_bf, ffn_conv_w[l], ffn_conv_b[l][None],
                 w_down_bf, norm_final[None], l == depth - 1, seq_len)
    return h.reshape(batch, seq_len, d)
```

```python
import functools

import jax
import jax.numpy as jnp
from jax import lax
from jax.experimental import pallas as pl
from jax.experimental.pallas import tpu as pltpu

SSM_GROUP = 16
SSM_STATE = 64
CONV_K = 3
EPS = 1e-6

LANES = 128
SUBLANES = 8
BF16_SUBLANES = 16
MXU_DIM = 256
VMEM_LIMIT_BYTES = 56 * 1024 * 1024

TOKEN_TILE = 512
FFN_TOKEN_TILE = 1024
MIX_TOKEN_TILE = 256
CHUNK = 16
GROUPS_PER_LANE_BLOCK = LANES // SSM_GROUP
STATE_LANES = GROUPS_PER_LANE_BLOCK * SSM_STATE

F32 = jnp.float32
BF16 = jnp.bfloat16


def _rmsnorm(x, gain):
    return x * lax.rsqrt(jnp.mean(x * x, axis=-1, keepdims=True) + EPS) * gain


def _dot(a, b):
    return jnp.dot(a, b, preferred_element_type=F32)


def _causal_conv3(a, prev, w_ref, b_ref):
    w0 = w_ref[0:1, :]
    w1 = w_ref[1:2, :]
    w2 = w_ref[2:3, :]
    bias = b_ref[...]
    full = w2 * a + w1 * pltpu.roll(a, 1, 0) + w0 * pltpu.roll(a, 2, 0) + bias
    top = a[0:SUBLANES]
    rows = lax.broadcasted_iota(jnp.int32, top.shape, 0)
    s1 = jnp.where(rows < 1, pltpu.roll(prev, 1, 0), pltpu.roll(top, 1, 0))
    s2 = jnp.where(rows < 2, pltpu.roll(prev, 2, 0), pltpu.roll(top, 2, 0))
    fix = w2 * top + w1 * s1 + w0 * s2 + bias
    return jnp.concatenate([fix, full[SUBLANES:]], axis=0)


def _cast_riders(src_refs, dst_refs):
    for src, dst in zip(src_refs, dst_refs):
        dst[...] = src[...].astype(BF16)


def _inproj_body(tiles_per_seq, n_cast, x_ref, g_ref, wu_ref, wv_ref, wgb_ref, wgc_ref,
                 cw_ref, cb_ref, *rest):
    cast_src = rest[:n_cast]
    u_ref, zb_ref, xn_ref = rest[n_cast:n_cast + 3]
    cast_dst = rest[n_cast + 3:2 * n_cast + 3]
    prev_ref = rest[2 * n_cast + 3]
    i = pl.program_id(0)

    @pl.when(i % tiles_per_seq == 0)
    def _():
        prev_ref[...] = jnp.zeros_like(prev_ref)

    xn = _rmsnorm(x_ref[...], g_ref[...]).astype(BF16)
    xn_ref[...] = xn
    u_ref[...] = _dot(xn, wu_ref[...])
    cin = _dot(xn, wgc_ref[...]) * _dot(xn, wv_ref[...])
    conv = _causal_conv3(cin, prev_ref[...], cw_ref, cb_ref)
    prev_ref[...] = cin[cin.shape[0] - SUBLANES:]
    zb_ref[...] = (_dot(xn, wgb_ref[...]) * conv).astype(BF16)
    _cast_riders(cast_src, cast_dst)


def _inproj(x2, norm_g, w_in_bf, conv_w, conv_b, seq_len, riders):
    n, d = x2.shape
    cw = conv_w.shape[1]
    assert n % TOKEN_TILE == 0 and seq_len % TOKEN_TILE == 0 and w_in_bf.shape == (d, 4 * cw)
    steps = n // TOKEN_TILE
    tiles_per_seq = seq_len // TOKEN_TILE
    const = lambda i: (0, 0)
    wspec = lambda col: pl.BlockSpec((d, cw), lambda i, col=col: (0, col),
                                     pipeline_mode=pl.Buffered(1))
    cast_in, cast_out, cast_shape = [], [], []
    for w, cols in riders:
        assert w.shape[0] % (steps * BF16_SUBLANES) == 0
        rows = w.shape[0] // steps
        width = w.shape[1] if cols is None else cols[1]
        col = 0 if cols is None else cols[0]
        cast_in.append(pl.BlockSpec((rows, width), lambda i, col=col: (i, col)))
        cast_out.append(pl.BlockSpec((rows, width), lambda i: (i, 0)))
        cast_shape.append(jax.ShapeDtypeStruct((w.shape[0], width), BF16))
    outs = pl.pallas_call(
        functools.partial(_inproj_body, tiles_per_seq, len(riders)),
        grid=(steps,),
        in_specs=[
            pl.BlockSpec((TOKEN_TILE, d), lambda i: (i, 0)),
            pl.BlockSpec((1, d), const),
            wspec(0), wspec(1), wspec(2), wspec(3),
            pl.BlockSpec((CONV_K, cw), const),
            pl.BlockSpec((1, cw), const),
        ] + cast_in,
        out_specs=[
            pl.BlockSpec((TOKEN_TILE, cw), lambda i: (i, 0)),
            pl.BlockSpec((TOKEN_TILE, cw), lambda i: (i, 0)),
            pl.BlockSpec((TOKEN_TILE, d), lambda i: (i, 0)),
        ] + cast_out,
        out_shape=[
            jax.ShapeDtypeStruct((n, cw), F32),
            jax.ShapeDtypeStruct((n, cw), BF16),
            jax.ShapeDtypeStruct((n, d), BF16),
        ] + cast_shape,
        scratch_shapes=[pltpu.VMEM((SUBLANES, cw), F32)],
        compiler_params=pltpu.CompilerParams(
            dimension_semantics=("arbitrary",), vmem_limit_bytes=VMEM_LIMIT_BYTES),
        name="inproj",
    )(x2, norm_g, w_in_bf, w_in_bf, w_in_bf, w_in_bf, conv_w, conv_b, *[w for w, _ in riders])
    return outs[0], outs[1], outs[2], outs[3:]


def _ssm_body(n_cast, u_ref, tz_ref, p_ref, q_ref, pw_ref, *rest):
    cast_src = rest[:n_cast]
    y_ref = rest[n_cast]
    cast_dst = rest[n_cast + 1:2 * n_cast + 1]
    s_scr, xin_scr = rest[2 * n_cast + 1:]
    _cast_riders(cast_src, cast_dst)
    nseq = u_ref.shape[0]
    m = u_ref.shape[1] // CHUNK
    sl = STATE_LANES
    ngl = GROUPS_PER_LANE_BLOCK

    yts = []
    for s in range(nseq):
        pts = [u_ref[s, pl.ds(t, m, stride=CHUNK), :].T.astype(BF16)
               for t in range(CHUNK)]
        ys, s_re, s_im = [], [], []
        for g in range(ngl):
            xg = jnp.concatenate([pt[g * SSM_GROUP:(g + 1) * SSM_GROUP] for pt in pts], axis=0)
            ys.append(_dot(tz_ref[g], xg))
            sg = _dot(p_ref[g], xg)
            s_re.append(sg[:SSM_STATE])
            s_im.append(sg[SSM_STATE:])
        yts.append(ys)
        s_scr[s] = jnp.concatenate(s_re + s_im, axis=0).T

    a16r = pw_ref[0:1, :]
    a16i = pw_ref[1:2, :]

    def step(k, carry):
        out = []
        for s in range(nseq):
            xr, xi = carry[2 * s], carry[2 * s + 1]
            xin_scr[s, pl.ds(k, 1), :sl] = xr
            xin_scr[s, pl.ds(k, 1), sl:] = xi
            row = s_scr[s, pl.ds(k, 1), :]
            out += [a16r * xr - a16i * xi + row[:, :sl], a16r * xi + a16i * xr + row[:, sl:]]
        return tuple(out)

    zero = jnp.zeros((1, sl), F32)
    lax.fori_loop(0, m, step, (zero,) * (2 * nseq), unroll=8)

    for s in range(nseq):
        xint = xin_scr[s].T
        ys = yts[s]
        for g in range(ngl):
            xg = jnp.concatenate([xint[g * SSM_STATE:(g + 1) * SSM_STATE],
                                  xint[sl + g * SSM_STATE:sl + (g + 1) * SSM_STATE]], axis=0)
            ys[g] = ys[g] + _dot(q_ref[g], xg.astype(BF16))
        for t in range(CHUNK):
            yt = jnp.concatenate([y[t * SSM_GROUP:(t + 1) * SSM_GROUP] for y in ys], axis=0)
            y_ref[s, pl.ds(t, m, stride=CHUNK), :] = jax.nn.gelu(yt.T)


def _ssm(u, tz, pm, qm, pw, batch, seq_len, riders, seqs_per_step=2):
    n, width = u.shape
    nblk = width // LANES
    m = seq_len // CHUNK
    ngl = GROUPS_PER_LANE_BLOCK
    u3 = u.reshape(batch, seq_len, width)
    assert seq_len % CHUNK == 0 and width % LANES == 0 and batch % seqs_per_step == 0
    nb = batch // seqs_per_step
    assert all(w.shape[0] % (nblk * nb * BF16_SUBLANES) == 0 for w in riders)
    cast_specs = [pl.BlockSpec((w.shape[0] // (nblk * nb), w.shape[1]),
                               lambda c, b: (c * nb + b, 0)) for w in riders]
    outs = pl.pallas_call(
        functools.partial(_ssm_body, len(riders)),
        grid=(nblk, nb),
        in_specs=[
            pl.BlockSpec((seqs_per_step, seq_len, LANES), lambda c, b: (b, 0, c)),
            pl.BlockSpec((ngl, MXU_DIM, MXU_DIM), lambda c, b: (c, 0, 0)),
            pl.BlockSpec((ngl, 2 * SSM_STATE, MXU_DIM), lambda c, b: (c, 0, 0)),
            pl.BlockSpec((ngl, MXU_DIM, 2 * SSM_STATE), lambda c, b: (c, 0, 0)),
            pl.BlockSpec((None, 2, STATE_LANES), lambda c, b: (c, 0, 0)),
        ] + cast_specs,
        out_specs=[pl.BlockSpec((seqs_per_step, seq_len, LANES), lambda c, b: (b, 0, c))]
        + cast_specs,
        out_shape=[jax.ShapeDtypeStruct((batch, seq_len, width), F32)]
        + [jax.ShapeDtypeStruct(w.shape, BF16) for w in riders],
        scratch_shapes=[pltpu.VMEM((seqs_per_step, m, 2 * STATE_LANES), F32),
                        pltpu.VMEM((seqs_per_step, m, 2 * STATE_LANES), F32)],
        compiler_params=pltpu.CompilerParams(
            dimension_semantics=("arbitrary", "arbitrary"),
            vmem_limit_bytes=VMEM_LIMIT_BYTES),
        name="ssm",
    )(u3, tz, pm, qm, pw, *riders)
    return outs[0].reshape(n, width), outs[1:]


def _ssm_operators(a_re, a_im, log_dt, b_re, b_im, c_re, c_im, d_skip):
    hp = lax.Precision.HIGHEST
    g = a_re.shape[0]
    nblk = g // GROUPS_PER_LANE_BLOCK
    dt = jnp.exp(log_dt)[:, None]
    mag = jnp.exp(dt * a_re)
    abar_re = mag * jnp.cos(dt * a_im)
    abar_im = mag * jnp.sin(dt * a_im)
    nr = abar_re - 1.0
    ni = abar_im
    den = a_re * a_re + a_im * a_im
    fr = (nr * a_re + ni * a_im) / den
    fi = (ni * a_re - nr * a_im) / den
    bb_re = fr[..., None] * b_re - fi[..., None] * b_im
    bb_im = fr[..., None] * b_im + fi[..., None] * b_re
    nn = jnp.arange(CHUNK + 1, dtype=F32)
    pmag = jnp.exp(nn * (dt * a_re)[..., None])
    pw_re = pmag * jnp.cos(nn * (dt * a_im)[..., None])
    pw_im = pmag * jnp.sin(nn * (dt * a_im)[..., None])

    col = jnp.arange(MXU_DIM)
    pick = (CHUNK - 1 - col[None, :] // SSM_GROUP == jnp.arange(CHUNK)[:, None]).astype(F32)
    e = jnp.einsum("rgpn,nc->rgpc", jnp.stack([pw_re, pw_im])[..., :CHUNK], pick,
                   precision=hp)
    spread = (col[None, :] % SSM_GROUP == jnp.arange(SSM_GROUP)[:, None]).astype(F32)
    bx = jnp.einsum("rgpi,ic->rgpc", jnp.stack([bb_re, bb_im]), spread, precision=hp)
    p_re = e[0] * bx[0] - e[1] * bx[1]
    p_im = e[0] * bx[1] + e[1] * bx[0]
    p_ri = jnp.concatenate([p_re, p_im], axis=1)
    pm = p_ri.astype(BF16)

    krev = jnp.einsum("gop,gpc->goc", jnp.concatenate([c_re, -c_im], axis=-1), p_ri,
                      precision=hp)
    lag0 = (CHUNK - 1) * SSM_GROUP + jnp.arange(SSM_GROUP)
    krev = krev + d_skip[..., None] * (col[None, :] == lag0[:, None]).astype(F32)
    padded = jnp.concatenate([krev, jnp.zeros_like(krev)], axis=-1)
    tz = jnp.stack([padded[..., (CHUNK - 1 - t) * SSM_GROUP:(CHUNK - 1 - t) * SSM_GROUP + MXU_DIM]
                    for t in range(CHUNK)], axis=1)
    tz = tz.reshape(g, MXU_DIM, MXU_DIM).astype(BF16)

    o_re = pw_re[..., 1:].transpose(0, 2, 1)[:, :, None, :]
    o_im = pw_im[..., 1:].transpose(0, 2, 1)[:, :, None, :]
    cr = c_re[:, None]
    ci = c_im[:, None]
    qm = jnp.concatenate([cr * o_re - ci * o_im, -(cr * o_im + ci * o_re)], axis=-1)
    qm = qm.reshape(g, MXU_DIM, 2 * SSM_STATE).astype(BF16)

    pw = jnp.stack([pw_re[..., CHUNK], pw_im[..., CHUNK]])
    pw = pw.reshape(2, nblk, STATE_LANES).transpose(1, 0, 2)
    return tz, pm, qm, pw


def _mixout_body(col_tile, x_ref, xn_ref, ya_ref, zb_ref, wglu_ref, wm_ref, wssm_ref, wconv_ref,
                 wo_ref, h_ref, m_scr):
    d = h_ref.shape[1]
    ya = ya_ref[...]
    ya2 = (ya * jax.nn.sigmoid(_dot(ya.astype(BF16), wglu_ref[...]))).astype(BF16)
    xn = xn_ref[...]
    zb = zb_ref[...]
    for c in range(d // col_tile):
        cols = slice(c * col_tile, (c + 1) * col_tile)
        gate_a = jax.nn.sigmoid(_dot(xn, wm_ref[:, cols]))
        gate_b = jax.nn.sigmoid(_dot(xn, wm_ref[:, d + c * col_tile:d + (c + 1) * col_tile]))
        m_scr[:, cols] = (gate_a * _dot(ya2, wssm_ref[:, cols])
                          + gate_b * _dot(zb, wconv_ref[:, cols])).astype(BF16)
    m = m_scr[...]
    for c in range(d // col_tile):
        cols = slice(c * col_tile, (c + 1) * col_tile)
        h_ref[:, cols] = x_ref[:, cols] + _dot(m, wo_ref[:, cols])


def _mixout(x2, xn, ya, zb, w_glu_bf, w_merge_bf, w_ssm_bf, w_conv_bf, w_o_bf, col_tile=512):
    n, d = xn.shape
    sw = ya.shape[1]
    cw = zb.shape[1]
    tm = MIX_TOKEN_TILE
    assert n % tm == 0 and d % col_tile == 0 and w_merge_bf.shape == (d, 2 * d)
    resident = lambda shape: pl.BlockSpec(shape, lambda i: (0, 0), pipeline_mode=pl.Buffered(1))
    return pl.pallas_call(
        functools.partial(_mixout_body, col_tile),
        grid=(n // tm,),
        in_specs=[
            pl.BlockSpec((tm, d), lambda i: (i, 0)),
            pl.BlockSpec((tm, d), lambda i: (i, 0)),
            pl.BlockSpec((tm, sw), lambda i: (i, 0)),
            pl.BlockSpec((tm, cw), lambda i: (i, 0)),
            resident((sw, sw)),
            resident((d, 2 * d)),
            resident((sw, d)),
            resident((cw, d)),
            resident((d, d)),
        ],
        out_specs=pl.BlockSpec((tm, d), lambda i: (i, 0)),
        out_shape=jax.ShapeDtypeStruct((n, d), F32),
        scratch_shapes=[pltpu.VMEM((tm, d), BF16)],
        compiler_params=pltpu.CompilerParams(
            dimension_semantics=("arbitrary",), vmem_limit_bytes=VMEM_LIMIT_BYTES),
        name="mixout",
    )(x2, xn, ya, zb, w_glu_bf, w_merge_bf, w_ssm_bf, w_conv_bf, w_o_bf)


def _ffn_body(tiles_per_seq, final_norm, h_hbm, g_ref, wa_ref, wg_ref, cw_ref, cb_ref,
              wd_ref, gfin_ref, o_ref, hbuf, hn_scr, prev_scr, sem):
    i = pl.program_id(0)
    f = pl.program_id(1)
    tm = o_ref.shape[0]

    def h_copy(tile):
        rows = pl.ds(pl.multiple_of(tile * tm, tm), tm)
        return pltpu.make_async_copy(h_hbm.at[rows], hbuf, sem)

    @pl.when(f == 0)
    def _():
        @pl.when(i == 0)
        def _():
            h_copy(0).start()

        h_copy(i).wait()
        h = hbuf[...]
        hn_scr[...] = _rmsnorm(h, g_ref[...]).astype(BF16)
        o_ref[...] = h

        @pl.when(i + 1 < pl.num_programs(0))
        def _():
            h_copy(i + 1).start()

    @pl.when(i % tiles_per_seq == 0)
    def _():
        prev_scr[f] = jnp.zeros(prev_scr.shape[1:], F32)

    hn = hn_scr[...]
    a = _dot(hn, wa_ref[...])
    conv = _causal_conv3(a, prev_scr[f], cw_ref, cb_ref)
    prev_scr[f] = a[a.shape[0] - SUBLANES:]
    act = (jax.nn.gelu(conv) * _dot(hn, wg_ref[...])).astype(BF16)
    o_ref[...] += _dot(act, wd_ref[...])

    if final_norm:
        @pl.when(f == pl.num_programs(1) - 1)
        def _():
            o_ref[...] = _rmsnorm(o_ref[...], gfin_ref[...])


def _ffn(h1, norm_g, w_up_bf, ffn_conv_w, ffn_conv_b, w_down_bf, norm_final, final_norm,
         seq_len, ff_tile=512):
    n, d = h1.shape
    d_ff = w_down_bf.shape[0]
    nff = d_ff // ff_tile
    tm = FFN_TOKEN_TILE
    assert n % tm == 0 and seq_len % tm == 0 and d_ff % ff_tile == 0
    assert w_up_bf.shape == (d, 2 * d_ff)
    tiles_per_seq = seq_len // tm
    const = lambda i, f: (0, 0)
    return pl.pallas_call(
        functools.partial(_ffn_body, tiles_per_seq, final_norm),
        grid=(n // tm, nff),
        in_specs=[
            pl.BlockSpec(memory_space=pl.ANY),
            pl.BlockSpec((1, d), const),
            pl.BlockSpec((d, ff_tile), lambda i, f: (0, f)),
            pl.BlockSpec((d, ff_tile), lambda i, f: (0, nff + f)),
            pl.BlockSpec((CONV_K, ff_tile), lambda i, f: (0, f)),
            pl.BlockSpec((1, ff_tile), lambda i, f: (0, f)),
            pl.BlockSpec((ff_tile, d), lambda i, f: (f, 0)),
            pl.BlockSpec((1, d), const),
        ],
        out_specs=pl.BlockSpec((tm, d), lambda i, f: (i, 0)),
        out_shape=jax.ShapeDtypeStruct((n, d), F32),
        scratch_shapes=[pltpu.VMEM((tm, d), F32),
                        pltpu.VMEM((tm, d), BF16),
                        pltpu.VMEM((nff, SUBLANES, ff_tile), F32),
                        pltpu.SemaphoreType.DMA(())],
        compiler_params=pltpu.CompilerParams(
            dimension_semantics=("arbitrary", "arbitrary"),
            vmem_limit_bytes=VMEM_LIMIT_BYTES),
        name="ffn",
    )(h1, norm_g, w_up_bf, w_up_bf, ffn_conv_w, ffn_conv_b, w_down_bf, norm_final)


def kernel(x, norm_tok, w_in, a_re, a_im, log_dt, b_re, b_im, c_re, c_im, d_skip, w_glu, w_ssm_out, conv_w, conv_b, w_conv_out, w_o, norm_ffn, w_up, ffn_conv_w, ffn_conv_b, w_down, norm_final):
    batch, seq_len, d = x.shape
    depth = w_in.shape[0]
    h = x.reshape(batch * seq_len, d)
    for l in range(depth):
        n_first = w_in.shape[2] - 2 * d
        w_first_bf = w_in[l, :, :n_first].astype(BF16)
        u, zb, xn, (w_merge_bf, w_glu_bf, w_ssm_bf, w_conv_bf, w_o_bf, w_down_bf) = _inproj(
            h, norm_tok[l][None], w_first_bf, conv_w[l], conv_b[l][None], seq_len,
            [(w_in[l], (n_first // (2 * d), 2 * d)), (w_glu[l], None), (w_ssm_out[l], None),
             (w_conv_out[l], None), (w_o[l], None), (w_down[l], None)])
        ops = _ssm_operators(a_re[l], a_im[l], log_dt[l], b_re[l], b_im[l],
                             c_re[l], c_im[l], d_skip[l])
        ya, (w_up_bf,) = _ssm(u, *ops, batch, seq_len, [w_up[l]])
        h1 = _mixout(h, xn, ya, zb, w_glu_bf, w_merge_bf, w_ssm_bf, w_conv_bf, w_o_bf)
        h = _ffn(h1, norm_ffn[l][None], w_up_bf, ffn_conv_w[l], ffn_conv_b[l][None],
                 w_down_bf, norm_final[None], l == depth - 1, seq_len)
    return h.reshape(batch, seq_len, d)
```

```python
import functools

import jax
import jax.numpy as jnp
from jax import lax
from jax.experimental import pallas as pl
from jax.experimental.pallas import tpu as pltpu

SSM_GROUP = 16
SSM_STATE = 64
CONV_K = 3
EPS = 1e-6

LANES = 128
SUBLANES = 8
BF16_SUBLANES = 16
MXU_DIM = 256
VMEM_LIMIT_BYTES = 56 * 1024 * 1024

TOKEN_TILE = 512
FFN_TOKEN_TILE = 1024
MIX_TOKEN_TILE = 256
CHUNK = 16
GROUPS_PER_LANE_BLOCK = LANES // SSM_GROUP
STATE_LANES = GROUPS_PER_LANE_BLOCK * SSM_STATE

F32 = jnp.float32
BF16 = jnp.bfloat16


def _rmsnorm(x, gain):
    return x * lax.rsqrt(jnp.mean(x * x, axis=-1, keepdims=True) + EPS) * gain


def _dot(a, b):
    return jnp.dot(a, b, preferred_element_type=F32)


def _causal_conv3(a, prev, w_ref, b_ref):
    w0 = w_ref[0:1, :]
    w1 = w_ref[1:2, :]
    w2 = w_ref[2:3, :]
    bias = b_ref[...]
    full = w2 * a + w1 * pltpu.roll(a, 1, 0) + w0 * pltpu.roll(a, 2, 0) + bias
    top = a[0:SUBLANES]
    rows = lax.broadcasted_iota(jnp.int32, top.shape, 0)
    s1 = jnp.where(rows < 1, pltpu.roll(prev, 1, 0), pltpu.roll(top, 1, 0))
    s2 = jnp.where(rows < 2, pltpu.roll(prev, 2, 0), pltpu.roll(top, 2, 0))
    fix = w2 * top + w1 * s1 + w0 * s2 + bias
    return jnp.concatenate([fix, full[SUBLANES:]], axis=0)


def _cast_riders(src_refs, dst_refs):
    for src, dst in zip(src_refs, dst_refs):
        dst[...] = src[...].astype(BF16)


def _inproj_body(tiles_per_seq, n_cast, x_ref, g_ref, wu_ref, wv_ref, wgb_ref, wgc_ref,
                 cw_ref, cb_ref, *rest):
    cast_src = rest[:n_cast]
    u_ref, zb_ref, xn_ref = rest[n_cast:n_cast + 3]
    cast_dst = rest[n_cast + 3:2 * n_cast + 3]
    prev_ref = rest[2 * n_cast + 3]
    i = pl.program_id(0)

    @pl.when(i % tiles_per_seq == 0)
    def _():
        prev_ref[...] = jnp.zeros_like(prev_ref)

    xn = _rmsnorm(x_ref[...], g_ref[...]).astype(BF16)
    xn_ref[...] = xn
    u_ref[...] = _dot(xn, wu_ref[...])
    cin = _dot(xn, wgc_ref[...]) * _dot(xn, wv_ref[...])
    conv = _causal_conv3(cin, prev_ref[...], cw_ref, cb_ref)
    prev_ref[...] = cin[cin.shape[0] - SUBLANES:]
    zb_ref[...] = (_dot(xn, wgb_ref[...]) * conv).astype(BF16)
    _cast_riders(cast_src, cast_dst)


def _inproj(x2, norm_g, w_in_bf, conv_w, conv_b, seq_len, riders):
    n, d = x2.shape
    cw = conv_w.shape[1]
    assert n % TOKEN_TILE == 0 and seq_len % TOKEN_TILE == 0 and w_in_bf.shape == (d, 4 * cw)
    steps = n // TOKEN_TILE
    tiles_per_seq = seq_len // TOKEN_TILE
    const = lambda i: (0, 0)
    wspec = lambda col: pl.BlockSpec((d, cw), lambda i, col=col: (0, col),
                                     pipeline_mode=pl.Buffered(1))
    cast_in, cast_out, cast_shape = [], [], []
    for w, cols in riders:
        assert w.shape[0] % (steps * BF16_SUBLANES) == 0
        rows = w.shape[0] // steps
        width = w.shape[1] if cols is None else cols[1]
        col = 0 if cols is None else cols[0]
        cast_in.append(pl.BlockSpec((rows, width), lambda i, col=col: (i, col)))
        cast_out.append(pl.BlockSpec((rows, width), lambda i: (i, 0)))
        cast_shape.append(jax.ShapeDtypeStruct((w.shape[0], width), BF16))
    outs = pl.pallas_call(
        functools.partial(_inproj_body, tiles_per_seq, len(riders)),
        grid=(steps,),
        in_specs=[
            pl.BlockSpec((TOKEN_TILE, d), lambda i: (i, 0)),
            pl.BlockSpec((1, d), const),
            wspec(0), wspec(1), wspec(2), wspec(3),
            pl.BlockSpec((CONV_K, cw), const),
            pl.BlockSpec((1, cw), const),
        ] + cast_in,
        out_specs=[
            pl.BlockSpec((TOKEN_TILE, cw), lambda i: (i, 0)),
            pl.BlockSpec((TOKEN_TILE, cw), lambda i: (i, 0)),
            pl.BlockSpec((TOKEN_TILE, d), lambda i: (i, 0)),
        ] + cast_out,
        out_shape=[
            jax.ShapeDtypeStruct((n, cw), F32),
            jax.ShapeDtypeStruct((n, cw), BF16),
            jax.ShapeDtypeStruct((n, d), BF16),
        ] + cast_shape,
        scratch_shapes=[pltpu.VMEM((SUBLANES, cw), F32)],
        compiler_params=pltpu.CompilerParams(
            dimension_semantics=("arbitrary",), vmem_limit_bytes=VMEM_LIMIT_BYTES),
        name="inproj",
    )(x2, norm_g, w_in_bf, w_in_bf, w_in_bf, w_in_bf, conv_w, conv_b, *[w for w, _ in riders])
    return outs[0], outs[1], outs[2], outs[3:]


def _ssm_body(n_cast, u_ref, tz_ref, p_ref, q_ref, pw_ref, *rest):
    cast_src = rest[:n_cast]
    y_ref = rest[n_cast]
    cast_dst = rest[n_cast + 1:2 * n_cast + 1]
    s_scr, xin_scr = rest[2 * n_cast + 1:]
    _cast_riders(cast_src, cast_dst)
    nseq = u_ref.shape[0]
    m = u_ref.shape[1] // CHUNK
    sl = STATE_LANES
    ngl = GROUPS_PER_LANE_BLOCK

    yts = []
    for s in range(nseq):
        pts = [u_ref[s, pl.ds(t, m, stride=CHUNK), :].T.astype(BF16)
               for t in range(CHUNK)]
        ys, s_re, s_im = [], [], []
        for g in range(ngl):
            xg = jnp.concatenate([pt[g * SSM_GROUP:(g + 1) * SSM_GROUP] for pt in pts], axis=0)
            ys.append(_dot(tz_ref[g], xg))
            sg = _dot(p_ref[g], xg)
            s_re.append(sg[:SSM_STATE])
            s_im.append(sg[SSM_STATE:])
        yts.append(ys)
        s_scr[s] = jnp.concatenate(s_re + s_im, axis=0).T

    a16r = pw_ref[0:1, :]
    a16i = pw_ref[1:2, :]

    def step(k, carry):
        out = []
        for s in range(nseq):
            xr, xi = carry[2 * s], carry[2 * s + 1]
            xin_scr[s, pl.ds(k, 1), :sl] = xr
            xin_scr[s, pl.ds(k, 1), sl:] = xi
            row = s_scr[s, pl.ds(k, 1), :]
            out += [a16r * xr - a16i * xi + row[:, :sl], a16r * xi + a16i * xr + row[:, sl:]]
        return tuple(out)

    zero = jnp.zeros((1, sl), F32)
    lax.fori_loop(0, m, step, (zero,) * (2 * nseq), unroll=8)

    for s in range(nseq):
        xint = xin_scr[s].T
        ys = yts[s]
        for g in range(ngl):
            xg = jnp.concatenate([xint[g * SSM_STATE:(g + 1) * SSM_STATE],
                                  xint[sl + g * SSM_STATE:sl + (g + 1) * SSM_STATE]], axis=0)
            ys[g] = ys[g] + _dot(q_ref[g], xg.astype(BF16))
        for t in range(CHUNK):
            yt = jnp.concatenate([y[t * SSM_GROUP:(t + 1) * SSM_GROUP] for y in ys], axis=0)
            y_ref[s, pl.ds(t, m, stride=CHUNK), :] = jax.nn.gelu(yt.T)


def _ssm(u, tz, pm, qm, pw, batch, seq_len, riders, seqs_per_step=2):
    n, width = u.shape
    nblk = width // LANES
    m = seq_len // CHUNK
    ngl = GROUPS_PER_LANE_BLOCK
    u3 = u.reshape(batch, seq_len, width)
    assert seq_len % CHUNK == 0 and width % LANES == 0 and batch % seqs_per_step == 0
    nb = batch // seqs_per_step
    assert all(w.shape[0] % (nblk * nb * BF16_SUBLANES) == 0 for w in riders)
    cast_specs = [pl.BlockSpec((w.shape[0] // (nblk * nb), w.shape[1]),
                               lambda c, b: (c * nb + b, 0)) for w in riders]
    outs = pl.pallas_call(
        functools.partial(_ssm_body, len(riders)),
        grid=(nblk, nb),
        in_specs=[
            pl.BlockSpec((seqs_per_step, seq_len, LANES), lambda c, b: (b, 0, c)),
            pl.BlockSpec((ngl, MXU_DIM, MXU_DIM), lambda c, b: (c, 0, 0)),
            pl.BlockSpec((ngl, 2 * SSM_STATE, MXU_DIM), lambda c, b: (c, 0, 0)),
            pl.BlockSpec((ngl, MXU_DIM, 2 * SSM_STATE), lambda c, b: (c, 0, 0)),
            pl.BlockSpec((None, 2, STATE_LANES), lambda c, b: (c, 0, 0)),
        ] + cast_specs,
        out_specs=[pl.BlockSpec((seqs_per_step, seq_len, LANES), lambda c, b: (b, 0, c))]
        + cast_specs,
        out_shape=[jax.ShapeDtypeStruct((batch, seq_len, width), F32)]
        + [jax.ShapeDtypeStruct(w.shape, BF16) for w in riders],
        scratch_shapes=[pltpu.VMEM((seqs_per_step, m, 2 * STATE_LANES), F32),
                        pltpu.VMEM((seqs_per_step, m, 2 * STATE_LANES), F32)],
        compiler_params=pltpu.CompilerParams(
            dimension_semantics=("arbitrary", "arbitrary"),
            vmem_limit_bytes=VMEM_LIMIT_BYTES),
        name="ssm",
    )(u3, tz, pm, qm, pw, *riders)
    return outs[0].reshape(n, width), outs[1:]


def _ssm_operators(a_re, a_im, log_dt, b_re, b_im, c_re, c_im, d_skip):
    hp = lax.Precision.HIGHEST
    g = a_re.shape[0]
    nblk = g // GROUPS_PER_LANE_BLOCK
    dt = jnp.exp(log_dt)[:, None]
    mag = jnp.exp(dt * a_re)
    abar_re = mag * jnp.cos(dt * a_im)
    abar_im = mag * jnp.sin(dt * a_im)
    nr = abar_re - 1.0
    ni = abar_im
    den = a_re * a_re + a_im * a_im
    fr = (nr * a_re + ni * a_im) / den
    fi = (ni * a_re - nr * a_im) / den
    bb_re = fr[..., None] * b_re - fi[..., None] * b_im
    bb_im = fr[..., None] * b_im + fi[..., None] * b_re
    nn = jnp.arange(CHUNK + 1, dtype=F32)
    pmag = jnp.exp(nn * (dt * a_re)[..., None])
    pw_re = pmag * jnp.cos(nn * (dt * a_im)[..., None])
    pw_im = pmag * jnp.sin(nn * (dt * a_im)[..., None])

    col = jnp.arange(MXU_DIM)
    pick = (CHUNK - 1 - col[None, :] // SSM_GROUP == jnp.arange(CHUNK)[:, None]).astype(F32)
    e = jnp.einsum("rgpn,nc->rgpc", jnp.stack([pw_re, pw_im])[..., :CHUNK], pick,
                   precision=hp)
    spread = (col[None, :] % SSM_GROUP == jnp.arange(SSM_GROUP)[:, None]).astype(F32)
    bx = jnp.einsum("rgpi,ic->rgpc", jnp.stack([bb_re, bb_im]), spread, precision=hp)
    p_re = e[0] * bx[0] - e[1] * bx[1]
    p_im = e[0] * bx[1] + e[1] * bx[0]
    p_ri = jnp.concatenate([p_re, p_im], axis=1)
    pm = p_ri.astype(BF16)

    krev = jnp.einsum("gop,gpc->goc", jnp.concatenate([c_re, -c_im], axis=-1), p_ri,
                      precision=hp)
    lag0 = (CHUNK - 1) * SSM_GROUP + jnp.arange(SSM_GROUP)
    krev = krev + d_skip[..., None] * (col[None, :] == lag0[:, None]).astype(F32)
    padded = jnp.concatenate([krev, jnp.zeros_like(krev)], axis=-1)
    tz = jnp.stack([padded[..., (CHUNK - 1 - t) * SSM_GROUP:(CHUNK - 1 - t) * SSM_GROUP + MXU_DIM]
                    for t in range(CHUNK)], axis=1)
    tz = tz.reshape(g, MXU_DIM, MXU_DIM).astype(BF16)

    o_re = pw_re[..., 1:].transpose(0, 2, 1)[:, :, None, :]
    o_im = pw_im[..., 1:].transpose(0, 2, 1)[:, :, None, :]
    cr = c_re[:, None]
    ci = c_im[:, None]
    qm = jnp.concatenate([cr * o_re - ci * o_im, -(cr * o_im + ci * o_re)], axis=-1)
    qm = qm.reshape(g, MXU_DIM, 2 * SSM_STATE).astype(BF16)

    pw = jnp.stack([pw_re[..., CHUNK], pw_im[..., CHUNK]])
    pw = pw.reshape(2, nblk, STATE_LANES).transpose(1, 0, 2)
    return tz, pm, qm, pw


def _mixout_body(col_tile, x_ref, xn_ref, ya_ref, zb_ref, wglu_ref, wm_ref, wssm_ref, wconv_ref,
                 wo_ref, g_ref, h_ref, hn_ref, m_scr):
    d = h_ref.shape[1]
    ya = ya_ref[...]
    ya2 = (ya * jax.nn.sigmoid(_dot(ya.astype(BF16), wglu_ref[...]))).astype(BF16)
    xn = xn_ref[...]
    zb = zb_ref[...]
    for c in range(d // col_tile):
        cols = slice(c * col_tile, (c + 1) * col_tile)
        gate_a = jax.nn.sigmoid(_dot(xn, wm_ref[:, cols]))
        gate_b = jax.nn.sigmoid(_dot(xn, wm_ref[:, d + c * col_tile:d + (c + 1) * col_tile]))
        m_scr[:, cols] = (gate_a * _dot(ya2, wssm_ref[:, cols])
                          + gate_b * _dot(zb, wconv_ref[:, cols])).astype(BF16)
    m = m_scr[...]
    ssq = jnp.zeros((h_ref.shape[0], 1), F32)
    for c in range(d // col_tile):
        cols = slice(c * col_tile, (c + 1) * col_tile)
        hc = x_ref[:, cols] + _dot(m, wo_ref[:, cols])
        h_ref[:, cols] = hc
        ssq = ssq + jnp.sum(hc * hc, axis=-1, keepdims=True)
    inv = lax.rsqrt(ssq / d + EPS)
    for c in range(d // col_tile):
        cols = slice(c * col_tile, (c + 1) * col_tile)
        hn_ref[:, cols] = (h_ref[:, cols] * inv * g_ref[:, cols]).astype(BF16)


def _mixout(x2, xn, ya, zb, w_glu_bf, w_merge_bf, w_ssm_bf, w_conv_bf, w_o_bf, norm_g,
            col_tile=512):
    n, d = xn.shape
    sw = ya.shape[1]
    cw = zb.shape[1]
    tm = MIX_TOKEN_TILE
    assert n % tm == 0 and d % col_tile == 0 and w_merge_bf.shape == (d, 2 * d)
    resident = lambda shape: pl.BlockSpec(shape, lambda i: (0, 0), pipeline_mode=pl.Buffered(1))
    return pl.pallas_call(
        functools.partial(_mixout_body, col_tile),
        grid=(n // tm,),
        in_specs=[
            pl.BlockSpec((tm, d), lambda i: (i, 0)),
            pl.BlockSpec((tm, d), lambda i: (i, 0)),
            pl.BlockSpec((tm, sw), lambda i: (i, 0)),
            pl.BlockSpec((tm, cw), lambda i: (i, 0)),
            resident((sw, sw)),
            resident((d, 2 * d)),
            resident((sw, d)),
            resident((cw, d)),
            resident((d, d)),
            resident((1, d)),
        ],
        out_specs=[pl.BlockSpec((tm, d), lambda i: (i, 0)),
                   pl.BlockSpec((tm, d), lambda i: (i, 0))],
        out_shape=[jax.ShapeDtypeStruct((n, d), F32), jax.ShapeDtypeStruct((n, d), BF16)],
        scratch_shapes=[pltpu.VMEM((tm, d), BF16)],
        compiler_params=pltpu.CompilerParams(
            dimension_semantics=("arbitrary",), vmem_limit_bytes=VMEM_LIMIT_BYTES),
        name="mixout",
    )(x2, xn, ya, zb, w_glu_bf, w_merge_bf, w_ssm_bf, w_conv_bf, w_o_bf, norm_g)


def _ffn_body(tiles_per_seq, final_norm, h_hbm, hn_ref, wa_ref, wg_ref, cw_ref, cb_ref,
              wd_ref, gfin_ref, o_ref, hbuf, prev_scr, sem):
    i = pl.program_id(0)
    f = pl.program_id(1)
    tm = o_ref.shape[0]

    def h_copy(tile):
        rows = pl.ds(pl.multiple_of(tile * tm, tm), tm)
        return pltpu.make_async_copy(h_hbm.at[rows], hbuf, sem)

    @pl.when(f == 0)
    def _():
        @pl.when(i == 0)
        def _():
            h_copy(0).start()

        h_copy(i).wait()
        o_ref[...] = hbuf[...]

        @pl.when(i + 1 < pl.num_programs(0))
        def _():
            h_copy(i + 1).start()

    @pl.when(i % tiles_per_seq == 0)
    def _():
        prev_scr[f] = jnp.zeros(prev_scr.shape[1:], F32)

    hn = hn_ref[...]
    a = _dot(hn, wa_ref[...])
    conv = _causal_conv3(a, prev_scr[f], cw_ref, cb_ref)
    prev_scr[f] = a[a.shape[0] - SUBLANES:]
    act = (jax.nn.gelu(conv) * _dot(hn, wg_ref[...])).astype(BF16)
    o_ref[...] += _dot(act, wd_ref[...])

    if final_norm:
        @pl.when(f == pl.num_programs(1) - 1)
        def _():
            o_ref[...] = _rmsnorm(o_ref[...], gfin_ref[...])


def _ffn(h1, hn, w_up_bf, ffn_conv_w, ffn_conv_b, w_down_bf, norm_final, final_norm,
         seq_len, ff_tile=512):
    n, d = h1.shape
    d_ff = w_down_bf.shape[0]
    nff = d_ff // ff_tile
    tm = FFN_TOKEN_TILE
    assert n % tm == 0 and seq_len % tm == 0 and d_ff % ff_tile == 0
    assert w_up_bf.shape == (d, 2 * d_ff)
    tiles_per_seq = seq_len // tm
    const = lambda i, f: (0, 0)
    return pl.pallas_call(
        functools.partial(_ffn_body, tiles_per_seq, final_norm),
        grid=(n // tm, nff),
        in_specs=[
            pl.BlockSpec(memory_space=pl.ANY),
            pl.BlockSpec((tm, d), lambda i, f: (i, 0)),
            pl.BlockSpec((d, ff_tile), lambda i, f: (0, f)),
            pl.BlockSpec((d, ff_tile), lambda i, f: (0, nff + f)),
            pl.BlockSpec((CONV_K, ff_tile), lambda i, f: (0, f)),
            pl.BlockSpec((1, ff_tile), lambda i, f: (0, f)),
            pl.BlockSpec((ff_tile, d), lambda i, f: (f, 0)),
            pl.BlockSpec((1, d), const),
        ],
        out_specs=pl.BlockSpec((tm, d), lambda i, f: (i, 0)),
        out_shape=jax.ShapeDtypeStruct((n, d), F32),
        scratch_shapes=[pltpu.VMEM((tm, d), F32),
                        pltpu.VMEM((nff, SUBLANES, ff_tile), F32),
                        pltpu.SemaphoreType.DMA(())],
        compiler_params=pltpu.CompilerParams(
            dimension_semantics=("arbitrary", "arbitrary"),
            vmem_limit_bytes=VMEM_LIMIT_BYTES),
        name="ffn",
    )(h1, hn, w_up_bf, w_up_bf, ffn_conv_w, ffn_conv_b, w_down_bf, norm_final)


def kernel(x, norm_tok, w_in, a_re, a_im, log_dt, b_re, b_im, c_re, c_im, d_skip, w_glu, w_ssm_out, conv_w, conv_b, w_conv_out, w_o, norm_ffn, w_up, ffn_conv_w, ffn_conv_b, w_down, norm_final):
    batch, seq_len, d = x.shape
    depth = w_in.shape[0]
    h = x.reshape(batch * seq_len, d)
    for l in range(depth):
        n_first = w_in.shape[2] - 2 * d
        w_first_bf = w_in[l, :, :n_first].astype(BF16)
        u, zb, xn, (w_merge_bf, w_glu_bf, w_ssm_bf, w_conv_bf, w_o_bf, w_down_bf) = _inproj(
            h, norm_tok[l][None], w_first_bf, conv_w[l], conv_b[l][None], seq_len,
            [(w_in[l], (n_first // (2 * d), 2 * d)), (w_glu[l], None), (w_ssm_out[l], None),
             (w_conv_out[l], None), (w_o[l], None), (w_down[l], None)])
        ops = _ssm_operators(a_re[l], a_im[l], log_dt[l], b_re[l], b_im[l],
                             c_re[l], c_im[l], d_skip[l])
        ya, (w_up_bf,) = _ssm(u, *ops, batch, seq_len, [w_up[l]])
        h1, hn = _mixout(h, xn, ya, zb, w_glu_bf, w_merge_bf, w_ssm_bf, w_conv_bf, w_o_bf,
                         norm_ffn[l][None])
        h = _ffn(h1, hn, w_up_bf, ffn_conv_w[l], ffn_conv_b[l][None],
                 w_down_bf, norm_final[None], l == depth - 1, seq_len)
    return h.reshape(batch, seq_len, d)
```

```python
import functools

import jax
import jax.numpy as jnp
from jax import lax
from jax.experimental import pallas as pl
from jax.experimental.pallas import tpu as pltpu

SSM_GROUP = 16
SSM_STATE = 64
CONV_K = 3
EPS = 1e-6

LANES = 128
SUBLANES = 8
BF16_SUBLANES = 16
MXU_DIM = 256
VMEM_LIMIT_BYTES = 56 * 1024 * 1024

TOKEN_TILE = 512
FFN_TOKEN_TILE = 1024
MIX_TOKEN_TILE = 256
CHUNK = 16
GROUPS_PER_LANE_BLOCK = LANES // SSM_GROUP
STATE_LANES = GROUPS_PER_LANE_BLOCK * SSM_STATE

F32 = jnp.float32
BF16 = jnp.bfloat16


def _rmsnorm(x, gain):
    return x * lax.rsqrt(jnp.mean(x * x, axis=-1, keepdims=True) + EPS) * gain


def _dot(a, b):
    return jnp.dot(a, b, preferred_element_type=F32)


def _causal_conv3(a, prev, w_ref, b_ref):
    w0 = w_ref[0:1, :]
    w1 = w_ref[1:2, :]
    w2 = w_ref[2:3, :]
    bias = b_ref[...]
    full = w2 * a + w1 * pltpu.roll(a, 1, 0) + w0 * pltpu.roll(a, 2, 0) + bias
    top = a[0:SUBLANES]
    rows = lax.broadcasted_iota(jnp.int32, top.shape, 0)
    s1 = jnp.where(rows < 1, pltpu.roll(prev, 1, 0), pltpu.roll(top, 1, 0))
    s2 = jnp.where(rows < 2, pltpu.roll(prev, 2, 0), pltpu.roll(top, 2, 0))
    fix = w2 * top + w1 * s1 + w0 * s2 + bias
    return jnp.concatenate([fix, full[SUBLANES:]], axis=0)


def _cast_riders(src_refs, dst_refs):
    for src, dst in zip(src_refs, dst_refs):
        dst[...] = src[...].astype(BF16)


def _inproj_body(tiles_per_seq, n_cast, x_ref, g_ref, wu_ref, wv_ref, wgb_ref, wgc_ref,
                 cw_ref, cb_ref, *rest):
    cast_src = rest[:n_cast]
    u_ref, zb_ref, xn_ref = rest[n_cast:n_cast + 3]
    cast_dst = rest[n_cast + 3:2 * n_cast + 3]
    prev_ref = rest[2 * n_cast + 3]
    i = pl.program_id(0)

    @pl.when(i % tiles_per_seq == 0)
    def _():
        prev_ref[...] = jnp.zeros_like(prev_ref)

    xn = _rmsnorm(x_ref[...], g_ref[...]).astype(BF16)
    xn_ref[...] = xn
    u_ref[...] = _dot(xn, wu_ref[...])
    cin = _dot(xn, wgc_ref[...]) * _dot(xn, wv_ref[...])
    conv = _causal_conv3(cin, prev_ref[...], cw_ref, cb_ref)
    prev_ref[...] = cin[cin.shape[0] - SUBLANES:]
    zb_ref[...] = (_dot(xn, wgb_ref[...]) * conv).astype(BF16)
    _cast_riders(cast_src, cast_dst)


def _inproj(x2, norm_g, w_in_bf, conv_w, conv_b, seq_len, riders):
    n, d = x2.shape
    cw = conv_w.shape[1]
    assert n % TOKEN_TILE == 0 and seq_len % TOKEN_TILE == 0 and w_in_bf.shape == (d, 4 * cw)
    steps = n // TOKEN_TILE
    tiles_per_seq = seq_len // TOKEN_TILE
    const = lambda i: (0, 0)
    wspec = lambda col: pl.BlockSpec((d, cw), lambda i, col=col: (0, col),
                                     pipeline_mode=pl.Buffered(1))
    cast_in, cast_out, cast_shape = [], [], []
    for w, cols in riders:
        assert w.shape[0] % (steps * BF16_SUBLANES) == 0
        rows = w.shape[0] // steps
        width = w.shape[1] if cols is None else cols[1]
        col = 0 if cols is None else cols[0]
        cast_in.append(pl.BlockSpec((rows, width), lambda i, col=col: (i, col)))
        cast_out.append(pl.BlockSpec((rows, width), lambda i: (i, 0)))
        cast_shape.append(jax.ShapeDtypeStruct((w.shape[0], width), BF16))
    outs = pl.pallas_call(
        functools.partial(_inproj_body, tiles_per_seq, len(riders)),
        grid=(steps,),
        in_specs=[
            pl.BlockSpec((TOKEN_TILE, d), lambda i: (i, 0)),
            pl.BlockSpec((1, d), const),
            wspec(0), wspec(1), wspec(2), wspec(3),
            pl.BlockSpec((CONV_K, cw), const),
            pl.BlockSpec((1, cw), const),
        ] + cast_in,
        out_specs=[
            pl.BlockSpec((TOKEN_TILE, cw), lambda i: (i, 0)),
            pl.BlockSpec((TOKEN_TILE, cw), lambda i: (i, 0)),
            pl.BlockSpec((TOKEN_TILE, d), lambda i: (i, 0)),
        ] + cast_out,
        out_shape=[
            jax.ShapeDtypeStruct((n, cw), F32),
            jax.ShapeDtypeStruct((n, cw), BF16),
            jax.ShapeDtypeStruct((n, d), BF16),
        ] + cast_shape,
        scratch_shapes=[pltpu.VMEM((SUBLANES, cw), F32)],
        compiler_params=pltpu.CompilerParams(
            dimension_semantics=("arbitrary",), vmem_limit_bytes=VMEM_LIMIT_BYTES),
        name="inproj",
    )(x2, norm_g, w_in_bf, w_in_bf, w_in_bf, w_in_bf, conv_w, conv_b, *[w for w, _ in riders])
    return outs[0], outs[1], outs[2], outs[3:]


def _ssm_body(n_cast, u_ref, tz_ref, p_ref, q_ref, pw_ref, *rest):
    cast_src = rest[:n_cast]
    y_ref = rest[n_cast]
    cast_dst = rest[n_cast + 1:2 * n_cast + 1]
    s_scr, xin_scr = rest[2 * n_cast + 1:]
    _cast_riders(cast_src, cast_dst)
    nseq = u_ref.shape[0]
    m = u_ref.shape[1] // CHUNK
    sl = STATE_LANES
    ngl = GROUPS_PER_LANE_BLOCK

    yts = []
    for s in range(nseq):
        pts = [u_ref[s, pl.ds(t, m, stride=CHUNK), :].T.astype(BF16)
               for t in range(CHUNK)]
        ys, s_re, s_im = [], [], []
        for g in range(ngl):
            xg = jnp.concatenate([pt[g * SSM_GROUP:(g + 1) * SSM_GROUP] for pt in pts], axis=0)
            ys.append(_dot(tz_ref[g], xg))
            sg = _dot(p_ref[g], xg)
            s_re.append(sg[:SSM_STATE])
            s_im.append(sg[SSM_STATE:])
        yts.append(ys)
        s_scr[s] = jnp.concatenate(s_re + s_im, axis=0).T

    a16r = pw_ref[0:1, :]
    a16i = pw_ref[1:2, :]

    def step(k, carry):
        out = []
        for s in range(nseq):
            xr, xi = carry[2 * s], carry[2 * s + 1]
            xin_scr[s, pl.ds(k, 1), :sl] = xr
            xin_scr[s, pl.ds(k, 1), sl:] = xi
            row = s_scr[s, pl.ds(k, 1), :]
            out += [a16r * xr - a16i * xi + row[:, :sl], a16r * xi + a16i * xr + row[:, sl:]]
        return tuple(out)

    zero = jnp.zeros((1, sl), F32)
    lax.fori_loop(0, m, step, (zero,) * (2 * nseq), unroll=8)

    for s in range(nseq):
        xint = xin_scr[s].T
        ys = yts[s]
        for g in range(ngl):
            xg = jnp.concatenate([xint[g * SSM_STATE:(g + 1) * SSM_STATE],
                                  xint[sl + g * SSM_STATE:sl + (g + 1) * SSM_STATE]], axis=0)
            ys[g] = ys[g] + _dot(q_ref[g], xg.astype(BF16))
        for t in range(CHUNK):
            yt = jnp.concatenate([y[t * SSM_GROUP:(t + 1) * SSM_GROUP] for y in ys], axis=0)
            y_ref[s, pl.ds(t, m, stride=CHUNK), :] = jax.nn.gelu(yt.T)


def _ssm(u, tz, pm, qm, pw, batch, seq_len, riders, seqs_per_step=2):
    n, width = u.shape
    nblk = width // LANES
    m = seq_len // CHUNK
    ngl = GROUPS_PER_LANE_BLOCK
    u3 = u.reshape(batch, seq_len, width)
    assert seq_len % CHUNK == 0 and width % LANES == 0 and batch % seqs_per_step == 0
    nb = batch // seqs_per_step
    assert all(w.shape[0] % (nblk * nb * BF16_SUBLANES) == 0 for w in riders)
    cast_specs = [pl.BlockSpec((w.shape[0] // (nblk * nb), w.shape[1]),
                               lambda c, b: (c * nb + b, 0)) for w in riders]
    outs = pl.pallas_call(
        functools.partial(_ssm_body, len(riders)),
        grid=(nblk, nb),
        in_specs=[
            pl.BlockSpec((seqs_per_step, seq_len, LANES), lambda c, b: (b, 0, c)),
            pl.BlockSpec((ngl, MXU_DIM, MXU_DIM), lambda c, b: (c, 0, 0)),
            pl.BlockSpec((ngl, 2 * SSM_STATE, MXU_DIM), lambda c, b: (c, 0, 0)),
            pl.BlockSpec((ngl, MXU_DIM, 2 * SSM_STATE), lambda c, b: (c, 0, 0)),
            pl.BlockSpec((None, 2, STATE_LANES), lambda c, b: (c, 0, 0)),
        ] + cast_specs,
        out_specs=[pl.BlockSpec((seqs_per_step, seq_len, LANES), lambda c, b: (b, 0, c))]
        + cast_specs,
        out_shape=[jax.ShapeDtypeStruct((batch, seq_len, width), F32)]
        + [jax.ShapeDtypeStruct(w.shape, BF16) for w in riders],
        scratch_shapes=[pltpu.VMEM((seqs_per_step, m, 2 * STATE_LANES), F32),
                        pltpu.VMEM((seqs_per_step, m, 2 * STATE_LANES), F32)],
        compiler_params=pltpu.CompilerParams(
            dimension_semantics=("arbitrary", "arbitrary"),
            vmem_limit_bytes=VMEM_LIMIT_BYTES),
        name="ssm",
    )(u3, tz, pm, qm, pw, *riders)
    return outs[0].reshape(n, width), outs[1:]


def _ssm_operators(a_re, a_im, log_dt, b_re, b_im, c_re, c_im, d_skip):
    hp = lax.Precision.HIGHEST
    g = a_re.shape[0]
    nblk = g // GROUPS_PER_LANE_BLOCK
    dt = jnp.exp(log_dt)[:, None]
    mag = jnp.exp(dt * a_re)
    abar_re = mag * jnp.cos(dt * a_im)
    abar_im = mag * jnp.sin(dt * a_im)
    nr = abar_re - 1.0
    ni = abar_im
    den = a_re * a_re + a_im * a_im
    fr = (nr * a_re + ni * a_im) / den
    fi = (ni * a_re - nr * a_im) / den
    bb_re = fr[..., None] * b_re - fi[..., None] * b_im
    bb_im = fr[..., None] * b_im + fi[..., None] * b_re
    nn = jnp.arange(CHUNK + 1, dtype=F32)
    pmag = jnp.exp(nn * (dt * a_re)[..., None])
    pw_re = pmag * jnp.cos(nn * (dt * a_im)[..., None])
    pw_im = pmag * jnp.sin(nn * (dt * a_im)[..., None])

    col = jnp.arange(MXU_DIM)
    pick = (CHUNK - 1 - col[None, :] // SSM_GROUP == jnp.arange(CHUNK)[:, None]).astype(F32)
    e = jnp.einsum("rgpn,nc->rgpc", jnp.stack([pw_re, pw_im])[..., :CHUNK], pick,
                   precision=hp)
    spread = (col[None, :] % SSM_GROUP == jnp.arange(SSM_GROUP)[:, None]).astype(F32)
    bx = jnp.einsum("rgpi,ic->rgpc", jnp.stack([bb_re, bb_im]), spread, precision=hp)
    p_re = e[0] * bx[0] - e[1] * bx[1]
    p_im = e[0] * bx[1] + e[1] * bx[0]
    p_ri = jnp.concatenate([p_re, p_im], axis=1)
    pm = p_ri.astype(BF16)

    krev = jnp.einsum("gop,gpc->goc", jnp.concatenate([c_re, -c_im], axis=-1), p_ri,
                      precision=hp)
    lag0 = (CHUNK - 1) * SSM_GROUP + jnp.arange(SSM_GROUP)
    krev = krev + d_skip[..., None] * (col[None, :] == lag0[:, None]).astype(F32)
    padded = jnp.concatenate([krev, jnp.zeros_like(krev)], axis=-1)
    tz = jnp.stack([padded[..., (CHUNK - 1 - t) * SSM_GROUP:(CHUNK - 1 - t) * SSM_GROUP + MXU_DIM]
                    for t in range(CHUNK)], axis=1)
    tz = tz.reshape(g, MXU_DIM, MXU_DIM).astype(BF16)

    o_re = pw_re[..., 1:].transpose(0, 2, 1)[:, :, None, :]
    o_im = pw_im[..., 1:].transpose(0, 2, 1)[:, :, None, :]
    cr = c_re[:, None]
    ci = c_im[:, None]
    qm = jnp.concatenate([cr * o_re - ci * o_im, -(cr * o_im + ci * o_re)], axis=-1)
    qm = qm.reshape(g, MXU_DIM, 2 * SSM_STATE).astype(BF16)

    pw = jnp.stack([pw_re[..., CHUNK], pw_im[..., CHUNK]])
    pw = pw.reshape(2, nblk, STATE_LANES).transpose(1, 0, 2)
    return tz, pm, qm, pw


def _mixout_body(col_tile, x_ref, xn_ref, ya_ref, zb_ref, wglu_ref, wm_ref, wssm_ref, wconv_ref,
                 wo_ref, g_ref, h_ref, hn_ref, m_scr):
    d = h_ref.shape[1]
    ya = ya_ref[...]
    ya2 = (ya * jax.nn.sigmoid(_dot(ya.astype(BF16), wglu_ref[...]))).astype(BF16)
    xn = xn_ref[...]
    zb = zb_ref[...]
    for c in range(d // col_tile):
        cols = slice(c * col_tile, (c + 1) * col_tile)
        gate_a = jax.nn.sigmoid(_dot(xn, wm_ref[:, cols]))
        gate_b = jax.nn.sigmoid(_dot(xn, wm_ref[:, d + c * col_tile:d + (c + 1) * col_tile]))
        m_scr[:, cols] = (gate_a * _dot(ya2, wssm_ref[:, cols])
                          + gate_b * _dot(zb, wconv_ref[:, cols])).astype(BF16)
    m = m_scr[...]
    ssq = jnp.zeros((h_ref.shape[0], 1), F32)
    for c in range(d // col_tile):
        cols = slice(c * col_tile, (c + 1) * col_tile)
        hc = x_ref[:, cols] + _dot(m, wo_ref[:, cols])
        h_ref[:, cols] = hc
        ssq = ssq + jnp.sum(hc * hc, axis=-1, keepdims=True)
    inv = lax.rsqrt(ssq / d + EPS)
    for c in range(d // col_tile):
        cols = slice(c * col_tile, (c + 1) * col_tile)
        hn_ref[:, cols] = (h_ref[:, cols] * inv * g_ref[:, cols]).astype(BF16)


def _mixout(x2, xn, ya, zb, w_glu_bf, w_merge_bf, w_ssm_bf, w_conv_bf, w_o_bf, norm_g,
            col_tile=512):
    n, d = xn.shape
    sw = ya.shape[1]
    cw = zb.shape[1]
    tm = MIX_TOKEN_TILE
    assert n % tm == 0 and d % col_tile == 0 and w_merge_bf.shape == (d, 2 * d)
    resident = lambda shape: pl.BlockSpec(shape, lambda i: (0, 0), pipeline_mode=pl.Buffered(1))
    return pl.pallas_call(
        functools.partial(_mixout_body, col_tile),
        grid=(n // tm,),
        in_specs=[
            pl.BlockSpec((tm, d), lambda i: (i, 0)),
            pl.BlockSpec((tm, d), lambda i: (i, 0)),
            pl.BlockSpec((tm, sw), lambda i: (i, 0)),
            pl.BlockSpec((tm, cw), lambda i: (i, 0)),
            resident((sw, sw)),
            resident((d, 2 * d)),
            resident((sw, d)),
            resident((cw, d)),
            resident((d, d)),
            resident((1, d)),
        ],
        out_specs=[pl.BlockSpec((tm, d), lambda i: (i, 0)),
                   pl.BlockSpec((tm, d), lambda i: (i, 0))],
        out_shape=[jax.ShapeDtypeStruct((n, d), F32), jax.ShapeDtypeStruct((n, d), BF16)],
        scratch_shapes=[pltpu.VMEM((tm, d), BF16)],
        compiler_params=pltpu.CompilerParams(
            dimension_semantics=("arbitrary",), vmem_limit_bytes=VMEM_LIMIT_BYTES),
        name="mixout",
    )(x2, xn, ya, zb, w_glu_bf, w_merge_bf, w_ssm_bf, w_conv_bf, w_o_bf, norm_g)


def _ffn_body(tiles_per_seq, final_norm, h_hbm, hn_ref, wa_ref, wg_ref, cw_ref, cb_ref,
              wd_ref, gfin_ref, o_ref, hbuf, prev_scr, sem):
    i = pl.program_id(0)
    f = pl.program_id(1)
    tm = o_ref.shape[0]

    def h_copy(tile):
        rows = pl.ds(pl.multiple_of(tile * tm, tm), tm)
        return pltpu.make_async_copy(h_hbm.at[rows], hbuf, sem)

    @pl.when(f == 0)
    def _():
        @pl.when(i == 0)
        def _():
            h_copy(0).start()

        h_copy(i).wait()
        o_ref[...] = hbuf[...]

        @pl.when(i + 1 < pl.num_programs(0))
        def _():
            h_copy(i + 1).start()

    @pl.when(i % tiles_per_seq == 0)
    def _():
        prev_scr[f] = jnp.zeros(prev_scr.shape[1:], F32)

    def chunk(normalize):
        hn = hn_ref[...]
        a = _dot(hn, wa_ref[...])
        conv = _causal_conv3(a, prev_scr[f], cw_ref, cb_ref)
        prev_scr[f] = a[a.shape[0] - SUBLANES:]
        act = (jax.nn.gelu(conv) * _dot(hn, wg_ref[...])).astype(BF16)
        if not normalize:
            o_ref[...] += _dot(act, wd_ref[...])
            return
        for r in range(tm // MXU_DIM):
            rows = slice(r * MXU_DIM, (r + 1) * MXU_DIM)
            y = o_ref[rows, :] + _dot(act[rows], wd_ref[...])
            o_ref[rows, :] = _rmsnorm(y, gfin_ref[...])

    if final_norm:
        last = pl.num_programs(1) - 1
        pl.when(f < last)(lambda: chunk(False))
        pl.when(f == last)(lambda: chunk(True))
    else:
        chunk(False)


def _ffn(h1, hn, w_up_bf, ffn_conv_w, ffn_conv_b, w_down_bf, norm_final, final_norm,
         seq_len, ff_tile=512):
    n, d = h1.shape
    d_ff = w_down_bf.shape[0]
    nff = d_ff // ff_tile
    tm = FFN_TOKEN_TILE
    assert n % tm == 0 and seq_len % tm == 0 and d_ff % ff_tile == 0
    assert w_up_bf.shape == (d, 2 * d_ff)
    tiles_per_seq = seq_len // tm
    const = lambda i, f: (0, 0)
    return pl.pallas_call(
        functools.partial(_ffn_body, tiles_per_seq, final_norm),
        grid=(n // tm, nff),
        in_specs=[
            pl.BlockSpec(memory_space=pl.ANY),
            pl.BlockSpec((tm, d), lambda i, f: (i, 0)),
            pl.BlockSpec((d, ff_tile), lambda i, f: (0, f)),
            pl.BlockSpec((d, ff_tile), lambda i, f: (0, nff + f)),
            pl.BlockSpec((CONV_K, ff_tile), lambda i, f: (0, f)),
            pl.BlockSpec((1, ff_tile), lambda i, f: (0, f)),
            pl.BlockSpec((ff_tile, d), lambda i, f: (f, 0)),
            pl.BlockSpec((1, d), const),
        ],
        out_specs=pl.BlockSpec((tm, d), lambda i, f: (i, 0)),
        out_shape=jax.ShapeDtypeStruct((n, d), F32),
        scratch_shapes=[pltpu.VMEM((tm, d), F32),
                        pltpu.VMEM((nff, SUBLANES, ff_tile), F32),
                        pltpu.SemaphoreType.DMA(())],
        compiler_params=pltpu.CompilerParams(
            dimension_semantics=("arbitrary", "arbitrary"),
            vmem_limit_bytes=VMEM_LIMIT_BYTES),
        name="ffn",
    )(h1, hn, w_up_bf, w_up_bf, ffn_conv_w, ffn_conv_b, w_down_bf, norm_final)


def kernel(x, norm_tok, w_in, a_re, a_im, log_dt, b_re, b_im, c_re, c_im, d_skip, w_glu, w_ssm_out, conv_w, conv_b, w_conv_out, w_o, norm_ffn, w_up, ffn_conv_w, ffn_conv_b, w_down, norm_final):
    batch, seq_len, d = x.shape
    depth = w_in.shape[0]
    h = x.reshape(batch * seq_len, d)
    for l in range(depth):
        n_first = w_in.shape[2] - 2 * d
        w_first_bf = w_in[l, :, :n_first].astype(BF16)
        u, zb, xn, (w_merge_bf, w_glu_bf, w_ssm_bf, w_conv_bf, w_o_bf, w_down_bf) = _inproj(
            h, norm_tok[l][None], w_first_bf, conv_w[l], conv_b[l][None], seq_len,
            [(w_in[l], (n_first // (2 * d), 2 * d)), (w_glu[l], None), (w_ssm_out[l], None),
             (w_conv_out[l], None), (w_o[l], None), (w_down[l], None)])
        ops = _ssm_operators(a_re[l], a_im[l], log_dt[l], b_re[l], b_im[l],
                             c_re[l], c_im[l], d_skip[l])
        ya, (w_up_bf,) = _ssm(u, *ops, batch, seq_len, [w_up[l]])
        h1, hn = _mixout(h, xn, ya, zb, w_glu_bf, w_merge_bf, w_ssm_bf, w_conv_bf, w_o_bf,
                         norm_ffn[l][None])
        h = _ffn(h1, hn, w_up_bf, ffn_conv_w[l], ffn_conv_b[l][None],
                 w_down_bf, norm_final[None], l == depth - 1, seq_len)
    return h.reshape(batch, seq_len, d)
```

```python
import functools

import jax
import jax.numpy as jnp
from jax import lax
from jax.experimental import pallas as pl
from jax.experimental.pallas import tpu as pltpu

SSM_GROUP = 16
SSM_STATE = 64
CONV_K = 3
EPS = 1e-6

LANES = 128
SUBLANES = 8
BF16_SUBLANES = 16
MXU_DIM = 256
VMEM_LIMIT_BYTES = 56 * 1024 * 1024

TOKEN_TILE = 512
FFN_TOKEN_TILE = 1024
MIX_TOKEN_TILE = 256
CHUNK = 16
GROUPS_PER_LANE_BLOCK = LANES // SSM_GROUP
STATE_LANES = GROUPS_PER_LANE_BLOCK * SSM_STATE

F32 = jnp.float32
BF16 = jnp.bfloat16


def _rmsnorm(x, gain):
    return x * lax.rsqrt(jnp.mean(x * x, axis=-1, keepdims=True) + EPS) * gain


def _dot(a, b):
    return jnp.dot(a, b, preferred_element_type=F32)


def _causal_conv3(a, prev, w_ref, b_ref):
    w0 = w_ref[0:1, :]
    w1 = w_ref[1:2, :]
    w2 = w_ref[2:3, :]
    bias = b_ref[...]
    full = w2 * a + w1 * pltpu.roll(a, 1, 0) + w0 * pltpu.roll(a, 2, 0) + bias
    top = a[0:SUBLANES]
    rows = lax.broadcasted_iota(jnp.int32, top.shape, 0)
    s1 = jnp.where(rows < 1, pltpu.roll(prev, 1, 0), pltpu.roll(top, 1, 0))
    s2 = jnp.where(rows < 2, pltpu.roll(prev, 2, 0), pltpu.roll(top, 2, 0))
    fix = w2 * top + w1 * s1 + w0 * s2 + bias
    return jnp.concatenate([fix, full[SUBLANES:]], axis=0)


def _cast_riders(src_refs, dst_refs):
    for src, dst in zip(src_refs, dst_refs):
        dst[...] = src[...].astype(BF16)


def _inproj_body(tiles_per_seq, n_cast, x_ref, g_ref, wu_ref, wv_ref, wgb_ref, wgc_ref,
                 cw_ref, cb_ref, *rest):
    cast_src = rest[:n_cast]
    u_ref, zb_ref, xn_ref = rest[n_cast:n_cast + 3]
    cast_dst = rest[n_cast + 3:2 * n_cast + 3]
    prev_ref = rest[2 * n_cast + 3]
    i = pl.program_id(0)

    @pl.when(i % tiles_per_seq == 0)
    def _():
        prev_ref[...] = jnp.zeros_like(prev_ref)

    xn = _rmsnorm(x_ref[...], g_ref[...]).astype(BF16)
    xn_ref[...] = xn
    u_ref[...] = _dot(xn, wu_ref[...])
    cin = _dot(xn, wgc_ref[...]) * _dot(xn, wv_ref[...])
    conv = _causal_conv3(cin, prev_ref[...], cw_ref, cb_ref)
    prev_ref[...] = cin[cin.shape[0] - SUBLANES:]
    zb_ref[...] = (_dot(xn, wgb_ref[...]) * conv).astype(BF16)
    _cast_riders(cast_src, cast_dst)


def _inproj(x2, norm_g, w_in_bf, conv_w, conv_b, seq_len, riders):
    n, d = x2.shape
    cw = conv_w.shape[1]
    assert n % TOKEN_TILE == 0 and seq_len % TOKEN_TILE == 0 and w_in_bf.shape == (d, 4 * cw)
    steps = n // TOKEN_TILE
    tiles_per_seq = seq_len // TOKEN_TILE
    const = lambda i: (0, 0)
    wspec = lambda col: pl.BlockSpec((d, cw), lambda i, col=col: (0, col),
                                     pipeline_mode=pl.Buffered(1))
    cast_in, cast_out, cast_shape = [], [], []
    for w, cols in riders:
        assert w.shape[0] % (steps * BF16_SUBLANES) == 0
        rows = w.shape[0] // steps
        width = w.shape[1] if cols is None else cols[1]
        col = 0 if cols is None else cols[0]
        cast_in.append(pl.BlockSpec((rows, width), lambda i, col=col: (i, col)))
        cast_out.append(pl.BlockSpec((rows, width), lambda i: (i, 0)))
        cast_shape.append(jax.ShapeDtypeStruct((w.shape[0], width), BF16))
    outs = pl.pallas_call(
        functools.partial(_inproj_body, tiles_per_seq, len(riders)),
        grid=(steps,),
        in_specs=[
            pl.BlockSpec((TOKEN_TILE, d), lambda i: (i, 0)),
            pl.BlockSpec((1, d), const),
            wspec(0), wspec(1), wspec(2), wspec(3),
            pl.BlockSpec((CONV_K, cw), const),
            pl.BlockSpec((1, cw), const),
        ] + cast_in,
        out_specs=[
            pl.BlockSpec((TOKEN_TILE, cw), lambda i: (i, 0)),
            pl.BlockSpec((TOKEN_TILE, cw), lambda i: (i, 0)),
            pl.BlockSpec((TOKEN_TILE, d), lambda i: (i, 0)),
        ] + cast_out,
        out_shape=[
            jax.ShapeDtypeStruct((n, cw), F32),
            jax.ShapeDtypeStruct((n, cw), BF16),
            jax.ShapeDtypeStruct((n, d), BF16),
        ] + cast_shape,
        scratch_shapes=[pltpu.VMEM((SUBLANES, cw), F32)],
        compiler_params=pltpu.CompilerParams(
            dimension_semantics=("arbitrary",), vmem_limit_bytes=VMEM_LIMIT_BYTES),
        name="inproj",
    )(x2, norm_g, w_in_bf, w_in_bf, w_in_bf, w_in_bf, conv_w, conv_b, *[w for w, _ in riders])
    return outs[0], outs[1], outs[2], outs[3:]


def _ssm_body(n_cast, u_ref, tz_ref, p_ref, q_ref, pw_ref, *rest):
    cast_src = rest[:n_cast]
    y_ref = rest[n_cast]
    cast_dst = rest[n_cast + 1:2 * n_cast + 1]
    s_scr, xin_scr = rest[2 * n_cast + 1:]
    _cast_riders(cast_src, cast_dst)
    nseq = u_ref.shape[0]
    m = u_ref.shape[1] // CHUNK
    sl = STATE_LANES
    ngl = GROUPS_PER_LANE_BLOCK

    yts = []
    for s in range(nseq):
        pts = [u_ref[s, pl.ds(t, m, stride=CHUNK), :].T.astype(BF16)
               for t in range(CHUNK)]
        ys, s_re, s_im = [], [], []
        for g in range(ngl):
            xg = jnp.concatenate([pt[g * SSM_GROUP:(g + 1) * SSM_GROUP] for pt in pts], axis=0)
            ys.append(_dot(tz_ref[g], xg))
            sg = _dot(p_ref[g], xg)
            s_re.append(sg[:SSM_STATE])
            s_im.append(sg[SSM_STATE:])
        yts.append(ys)
        s_scr[s] = jnp.concatenate(s_re + s_im, axis=0).T

    a16r = pw_ref[0:1, :]
    a16i = pw_ref[1:2, :]

    def step(k, carry):
        out = []
        for s in range(nseq):
            xr, xi = carry[2 * s], carry[2 * s + 1]
            xin_scr[s, pl.ds(k, 1), :sl] = xr
            xin_scr[s, pl.ds(k, 1), sl:] = xi
            row = s_scr[s, pl.ds(k, 1), :]
            out += [a16r * xr - a16i * xi + row[:, :sl], a16r * xi + a16i * xr + row[:, sl:]]
        return tuple(out)

    zero = jnp.zeros((1, sl), F32)
    lax.fori_loop(0, m, step, (zero,) * (2 * nseq), unroll=8)

    for s in range(nseq):
        xint = xin_scr[s].T
        ys = yts[s]
        for g in range(ngl):
            xg = jnp.concatenate([xint[g * SSM_STATE:(g + 1) * SSM_STATE],
                                  xint[sl + g * SSM_STATE:sl + (g + 1) * SSM_STATE]], axis=0)
            ys[g] = ys[g] + _dot(q_ref[g], xg.astype(BF16))
        for t in range(CHUNK):
            yt = jnp.concatenate([y[t * SSM_GROUP:(t + 1) * SSM_GROUP] for y in ys], axis=0)
            y_ref[s, pl.ds(t, m, stride=CHUNK), :] = jax.nn.gelu(yt.T)


def _ssm(u, tz, pm, qm, pw, batch, seq_len, riders, seqs_per_step=2):
    n, width = u.shape
    nblk = width // LANES
    m = seq_len // CHUNK
    ngl = GROUPS_PER_LANE_BLOCK
    u3 = u.reshape(batch, seq_len, width)
    assert seq_len % CHUNK == 0 and width % LANES == 0 and batch % seqs_per_step == 0
    nb = batch // seqs_per_step
    assert all(w.shape[0] % (nblk * nb * BF16_SUBLANES) == 0 for w in riders)
    cast_specs = [pl.BlockSpec((w.shape[0] // (nblk * nb), w.shape[1]),
                               lambda c, b: (c * nb + b, 0)) for w in riders]
    outs = pl.pallas_call(
        functools.partial(_ssm_body, len(riders)),
        grid=(nblk, nb),
        in_specs=[
            pl.BlockSpec((seqs_per_step, seq_len, LANES), lambda c, b: (b, 0, c)),
            pl.BlockSpec((ngl, MXU_DIM, MXU_DIM), lambda c, b: (c, 0, 0)),
            pl.BlockSpec((ngl, 2 * SSM_STATE, MXU_DIM), lambda c, b: (c, 0, 0)),
            pl.BlockSpec((ngl, MXU_DIM, 2 * SSM_STATE), lambda c, b: (c, 0, 0)),
            pl.BlockSpec((None, 2, STATE_LANES), lambda c, b: (c, 0, 0)),
        ] + cast_specs,
        out_specs=[pl.BlockSpec((seqs_per_step, seq_len, LANES), lambda c, b: (b, 0, c))]
        + cast_specs,
        out_shape=[jax.ShapeDtypeStruct((batch, seq_len, width), F32)]
        + [jax.ShapeDtypeStruct(w.shape, BF16) for w in riders],
        scratch_shapes=[pltpu.VMEM((seqs_per_step, m, 2 * STATE_LANES), F32),
                        pltpu.VMEM((seqs_per_step, m, 2 * STATE_LANES), F32)],
        compiler_params=pltpu.CompilerParams(
            dimension_semantics=("arbitrary", "arbitrary"),
            vmem_limit_bytes=VMEM_LIMIT_BYTES),
        name="ssm",
    )(u3, tz, pm, qm, pw, *riders)
    return outs[0].reshape(n, width), outs[1:]


def _ssm_operators(a_re, a_im, log_dt, b_re, b_im, c_re, c_im, d_skip):
    hp = lax.Precision.HIGHEST
    g = a_re.shape[0]
    nblk = g // GROUPS_PER_LANE_BLOCK
    dt = jnp.exp(log_dt)[:, None]
    mag = jnp.exp(dt * a_re)
    abar_re = mag * jnp.cos(dt * a_im)
    abar_im = mag * jnp.sin(dt * a_im)
    nr = abar_re - 1.0
    ni = abar_im
    den = a_re * a_re + a_im * a_im
    fr = (nr * a_re + ni * a_im) / den
    fi = (ni * a_re - nr * a_im) / den
    bb_re = fr[..., None] * b_re - fi[..., None] * b_im
    bb_im = fr[..., None] * b_im + fi[..., None] * b_re
    nn = jnp.arange(CHUNK + 1, dtype=F32)
    pmag = jnp.exp(nn * (dt * a_re)[..., None])
    pw_re = pmag * jnp.cos(nn * (dt * a_im)[..., None])
    pw_im = pmag * jnp.sin(nn * (dt * a_im)[..., None])

    col = jnp.arange(MXU_DIM)
    pick = (CHUNK - 1 - col[None, :] // SSM_GROUP == jnp.arange(CHUNK)[:, None]).astype(F32)
    e = jnp.einsum("rgpn,nc->rgpc", jnp.stack([pw_re, pw_im])[..., :CHUNK], pick,
                   precision=hp)
    spread = (col[None, :] % SSM_GROUP == jnp.arange(SSM_GROUP)[:, None]).astype(F32)
    bx = jnp.einsum("rgpi,ic->rgpc", jnp.stack([bb_re, bb_im]), spread, precision=hp)
    p_re = e[0] * bx[0] - e[1] * bx[1]
    p_im = e[0] * bx[1] + e[1] * bx[0]
    p_ri = jnp.concatenate([p_re, p_im], axis=1)
    pm = p_ri.astype(BF16)

    krev = jnp.einsum("gop,gpc->goc", jnp.concatenate([c_re, -c_im], axis=-1), p_ri,
                      precision=hp)
    lag0 = (CHUNK - 1) * SSM_GROUP + jnp.arange(SSM_GROUP)
    krev = krev + d_skip[..., None] * (col[None, :] == lag0[:, None]).astype(F32)
    padded = jnp.concatenate([krev, jnp.zeros_like(krev)], axis=-1)
    tz = jnp.stack([padded[..., (CHUNK - 1 - t) * SSM_GROUP:(CHUNK - 1 - t) * SSM_GROUP + MXU_DIM]
                    for t in range(CHUNK)], axis=1)
    tz = tz.reshape(g, MXU_DIM, MXU_DIM).astype(BF16)

    o_re = pw_re[..., 1:].transpose(0, 2, 1)[:, :, None, :]
    o_im = pw_im[..., 1:].transpose(0, 2, 1)[:, :, None, :]
    cr = c_re[:, None]
    ci = c_im[:, None]
    qm = jnp.concatenate([cr * o_re - ci * o_im, -(cr * o_im + ci * o_re)], axis=-1)
    qm = qm.reshape(g, MXU_DIM, 2 * SSM_STATE).astype(BF16)

    pw = jnp.stack([pw_re[..., CHUNK], pw_im[..., CHUNK]])
    pw = pw.reshape(2, nblk, STATE_LANES).transpose(1, 0, 2)
    return tz, pm, qm, pw


def _mixout_body(col_tile, x_ref, xn_ref, ya_ref, zb_ref, wglu_ref, wm_ref, wssm_ref, wconv_ref,
                 wo_ref, g_ref, h_ref, hn_ref, m_scr):
    d = h_ref.shape[1]
    ya = ya_ref[...]
    ya2 = (ya * jax.nn.sigmoid(_dot(ya.astype(BF16), wglu_ref[...]))).astype(BF16)
    xn = xn_ref[...]
    zb = zb_ref[...]
    for c in range(d // col_tile):
        cols = slice(c * col_tile, (c + 1) * col_tile)
        gate_a = jax.nn.sigmoid(_dot(xn, wm_ref[:, cols]))
        gate_b = jax.nn.sigmoid(_dot(xn, wm_ref[:, d + c * col_tile:d + (c + 1) * col_tile]))
        m_scr[:, cols] = (gate_a * _dot(ya2, wssm_ref[:, cols])
                          + gate_b * _dot(zb, wconv_ref[:, cols])).astype(BF16)
    m = m_scr[...]
    ssq = jnp.zeros((h_ref.shape[0], 1), F32)
    for c in range(d // col_tile):
        cols = slice(c * col_tile, (c + 1) * col_tile)
        hc = x_ref[:, cols] + _dot(m, wo_ref[:, cols])
        h_ref[:, cols] = hc
        ssq = ssq + jnp.sum(hc * hc, axis=-1, keepdims=True)
    inv = lax.rsqrt(ssq / d + EPS)
    for c in range(d // col_tile):
        cols = slice(c * col_tile, (c + 1) * col_tile)
        hn_ref[:, cols] = (h_ref[:, cols] * inv * g_ref[:, cols]).astype(BF16)


def _mixout(x2, xn, ya, zb, w_glu_bf, w_merge_bf, w_ssm_bf, w_conv_bf, w_o_bf, norm_g,
            col_tile=512):
    n, d = xn.shape
    sw = ya.shape[1]
    cw = zb.shape[1]
    tm = MIX_TOKEN_TILE
    assert n % tm == 0 and d % col_tile == 0 and w_merge_bf.shape == (d, 2 * d)
    resident = lambda shape: pl.BlockSpec(shape, lambda i: (0, 0), pipeline_mode=pl.Buffered(1))
    return pl.pallas_call(
        functools.partial(_mixout_body, col_tile),
        grid=(n // tm,),
        in_specs=[
            pl.BlockSpec((tm, d), lambda i: (i, 0)),
            pl.BlockSpec((tm, d), lambda i: (i, 0)),
            pl.BlockSpec((tm, sw), lambda i: (i, 0)),
            pl.BlockSpec((tm, cw), lambda i: (i, 0)),
            resident((sw, sw)),
            resident((d, 2 * d)),
            resident((sw, d)),
            resident((cw, d)),
            resident((d, d)),
            resident((1, d)),
        ],
        out_specs=[pl.BlockSpec((tm, d), lambda i: (i, 0)),
                   pl.BlockSpec((tm, d), lambda i: (i, 0))],
        out_shape=[jax.ShapeDtypeStruct((n, d), F32), jax.ShapeDtypeStruct((n, d), BF16)],
        scratch_shapes=[pltpu.VMEM((tm, d), BF16)],
        compiler_params=pltpu.CompilerParams(
            dimension_semantics=("arbitrary",), vmem_limit_bytes=VMEM_LIMIT_BYTES),
        name="mixout",
    )(x2, xn, ya, zb, w_glu_bf, w_merge_bf, w_ssm_bf, w_conv_bf, w_o_bf, norm_g)


def _ffn_body(tiles_per_seq, final_norm, h_hbm, hn_ref, wa_ref, wg_ref, cw_ref, cb_ref,
              wd_ref, gfin_ref, o_ref, hbuf, prev_scr, sem):
    i = pl.program_id(0)
    f = pl.program_id(1)
    tm = o_ref.shape[0]

    def h_copy(tile):
        rows = pl.ds(pl.multiple_of(tile * tm, tm), tm)
        return pltpu.make_async_copy(h_hbm.at[rows], hbuf, sem)

    @pl.when(f == 0)
    def _():
        @pl.when(i == 0)
        def _():
            h_copy(0).start()

        h_copy(i).wait()

    @pl.when(i % tiles_per_seq == 0)
    def _():
        prev_scr[f] = jnp.zeros(prev_scr.shape[1:], F32)

    def chunk(first):
        hn = hn_ref[...]
        a = _dot(hn, wa_ref[...])
        conv = _causal_conv3(a, prev_scr[f], cw_ref, cb_ref)
        prev_scr[f] = a[a.shape[0] - SUBLANES:]
        act = (jax.nn.gelu(conv) * _dot(hn, wg_ref[...])).astype(BF16)
        if first:
            o_ref[...] = hbuf[...] + _dot(act, wd_ref[...])

            @pl.when(i + 1 < pl.num_programs(0))
            def _():
                h_copy(i + 1).start()
        else:
            o_ref[...] += _dot(act, wd_ref[...])

    pl.when(f == 0)(lambda: chunk(True))
    pl.when(f > 0)(lambda: chunk(False))

    if final_norm:
        @pl.when(f == pl.num_programs(1) - 1)
        def _():
            o_ref[...] = _rmsnorm(o_ref[...], gfin_ref[...])


def _ffn(h1, hn, w_up_bf, ffn_conv_w, ffn_conv_b, w_down_bf, norm_final, final_norm,
         seq_len, ff_tile=512):
    n, d = h1.shape
    d_ff = w_down_bf.shape[0]
    nff = d_ff // ff_tile
    tm = FFN_TOKEN_TILE
    assert n % tm == 0 and seq_len % tm == 0 and d_ff % ff_tile == 0
    assert w_up_bf.shape == (d, 2 * d_ff)
    tiles_per_seq = seq_len // tm
    const = lambda i, f: (0, 0)
    return pl.pallas_call(
        functools.partial(_ffn_body, tiles_per_seq, final_norm),
        grid=(n // tm, nff),
        in_specs=[
            pl.BlockSpec(memory_space=pl.ANY),
            pl.BlockSpec((tm, d), lambda i, f: (i, 0)),
            pl.BlockSpec((d, ff_tile), lambda i, f: (0, f)),
            pl.BlockSpec((d, ff_tile), lambda i, f: (0, nff + f)),
            pl.BlockSpec((CONV_K, ff_tile), lambda i, f: (0, f)),
            pl.BlockSpec((1, ff_tile), lambda i, f: (0, f)),
            pl.BlockSpec((ff_tile, d), lambda i, f: (f, 0)),
            pl.BlockSpec((1, d), const),
        ],
        out_specs=pl.BlockSpec((tm, d), lambda i, f: (i, 0)),
        out_shape=jax.ShapeDtypeStruct((n, d), F32),
        scratch_shapes=[pltpu.VMEM((tm, d), F32),
                        pltpu.VMEM((nff, SUBLANES, ff_tile), F32),
                        pltpu.SemaphoreType.DMA(())],
        compiler_params=pltpu.CompilerParams(
            dimension_semantics=("arbitrary", "arbitrary"),
            vmem_limit_bytes=VMEM_LIMIT_BYTES),
        name="ffn",
    )(h1, hn, w_up_bf, w_up_bf, ffn_conv_w, ffn_conv_b, w_down_bf, norm_final)


def kernel(x, norm_tok, w_in, a_re, a_im, log_dt, b_re, b_im, c_re, c_im, d_skip, w_glu, w_ssm_out, conv_w, conv_b, w_conv_out, w_o, norm_ffn, w_up, ffn_conv_w, ffn_conv_b, w_down, norm_final):
    batch, seq_len, d = x.shape
    depth = w_in.shape[0]
    h = x.reshape(batch * seq_len, d)
    for l in range(depth):
        n_first = w_in.shape[2] - 2 * d
        w_first_bf = w_in[l, :, :n_first].astype(BF16)
        u, zb, xn, (w_merge_bf, w_glu_bf, w_ssm_bf, w_conv_bf, w_o_bf, w_down_bf) = _inproj(
            h, norm_tok[l][None], w_first_bf, conv_w[l], conv_b[l][None], seq_len,
            [(w_in[l], (n_first // (2 * d), 2 * d)), (w_glu[l], None), (w_ssm_out[l], None),
             (w_conv_out[l], None), (w_o[l], None), (w_down[l], None)])
        ops = _ssm_operators(a_re[l], a_im[l], log_dt[l], b_re[l], b_im[l],
                             c_re[l], c_im[l], d_skip[l])
        ya, (w_up_bf,) = _ssm(u, *ops, batch, seq_len, [w_up[l]])
        h1, hn = _mixout(h, xn, ya, zb, w_glu_bf, w_merge_bf, w_ssm_bf, w_conv_bf, w_o_bf,
                         norm_ffn[l][None])
        h = _ffn(h1, hn, w_up_bf, ffn_conv_w[l], ffn_conv_b[l][None],
                 w_down_bf, norm_final[None], l == depth - 1, seq_len)
    return h.reshape(batch, seq_len, d)
```
